```python
import math
import jax, jax.numpy as jnp
from jax import lax
import numpy as np

D_MODEL = 1024
BATCH = 4
SEQ = 8192
DEPTH = 1

CHUNK = 64
MEM_LEN = 256
EPS = 1e-6
NEG = -1e30

FOX_HEAD_DIM = 64
D_FOX = 3 * D_MODEL // 4
FOX_HEADS = D_FOX // FOX_HEAD_DIM
Q_BLOCK = 128

D_S5 = 3 * D_MODEL // 4
S5_GROUP = 16
S5_GROUPS = D_S5 // S5_GROUP
S5_STATE = 64

D_MEM = D_MODEL // 2
MEM_HEADS = 4
MEM_HEAD_DIM = D_MEM // MEM_HEADS

N_BRANCH = 3
IN_SIZES = (D_FOX, D_FOX, D_FOX, FOX_HEADS, D_FOX,
            D_S5, D_S5,
            D_MEM, D_MEM,
            N_BRANCH * D_MODEL)
N_IN = sum(IN_SIZES)

kernel_name = "hybrid_fox_s5_memory_gated_block"


def rms_norm(x, g):
    xf = x.astype(jnp.float32)
    y = xf * lax.rsqrt(jnp.mean(xf * xf, axis=-1, keepdims=True) + EPS)
    return (y * g.astype(jnp.float32)).astype(x.dtype)


def split_cols(z):
    offs = np.cumsum(np.array(IN_SIZES))[:-1].tolist()
    return jnp.split(z, offs, axis=-1)


def forgetting_attention(q, k, v, log_f):
    Bn, L, H, Dh = q.shape
    nb = L // Q_BLOCK
    F = jnp.cumsum(log_f, axis=1).transpose(0, 2, 1)
    kh = k.astype(jnp.float32).transpose(0, 2, 1, 3)
    vh = v.astype(jnp.float32).transpose(0, 2, 1, 3)
    qb = (q.astype(jnp.float32) * (Dh ** -0.5)).reshape(Bn, nb, Q_BLOCK, H, Dh).transpose(1, 0, 3, 2, 4)
    Fq = F.reshape(Bn, H, nb, Q_BLOCK).transpose(2, 0, 1, 3)
    starts = jnp.arange(nb, dtype=jnp.int32) * Q_BLOCK
    key_pos = jnp.arange(L, dtype=jnp.int32)

    def block(args):
        qi, Fi, s0 = args
        s = jnp.einsum('bhqd,bhkd->bhqk', qi, kh)
        s = s + Fi[..., None] - F[:, :, None, :]
        qpos = s0 + jnp.arange(Q_BLOCK, dtype=jnp.int32)
        s = jnp.where(key_pos[None, :] <= qpos[:, None], s, NEG)
        p = jax.nn.softmax(s, axis=-1)
        return jnp.einsum('bhqk,bhkd->bhqd', p, vh)

    out = lax.map(block, (qb, Fq, starts))
    return out.transpose(1, 0, 3, 2, 4).reshape(Bn, L, H * Dh).astype(q.dtype)


def s5_ssm(u, lam_re, lam_im, log_step, b_re, b_im, c_re, c_im, d_skip):
    Bn, L, _ = u.shape
    f32 = jnp.float32
    uf = u.astype(f32).reshape(Bn, L, S5_GROUPS, S5_GROUP)
    step = jnp.exp(log_step.astype(f32))[:, None]
    lr, li = lam_re.astype(f32), lam_im.astype(f32)
    mag = jnp.exp(lr * step)
    ab_re, ab_im = mag * jnp.cos(li * step), mag * jnp.sin(li * step)
    den = lr * lr + li * li
    nr, ni = ab_re - 1.0, ab_im
    f_re = (nr * lr + ni * li) / den
    f_im = (ni * lr - nr * li) / den
    br, bi = b_re.astype(f32), b_im.astype(f32)
    bb_re = f_re[..., None] * br - f_im[..., None] * bi
    bb_im = f_re[..., None] * bi + f_im[..., None] * br
    x_re = jnp.einsum('gph,blgh->blgp', bb_re, uf)
    x_im = jnp.einsum('gph,blgh->blgp', bb_im, uf)
    a_re = jnp.broadcast_to(ab_re[None, None], (1, L, S5_GROUPS, S5_STATE))
    a_im = jnp.broadcast_to(ab_im[None, None], (1, L, S5_GROUPS, S5_STATE))

    def combine(e1, e2):
        a1r, a1i, b1r, b1i = e1
        a2r, a2i, b2r, b2i = e2
        return (a2r * a1r - a2i * a1i,
                a2r * a1i + a2i * a1r,
                a2r * b1r - a2i * b1i + b2r,
                a2r * b1i + a2i * b1r + b2i)

    _, _, h_re, h_im = lax.associative_scan(combine, (a_re, a_im, x_re, x_im), axis=1)
    y = (jnp.einsum('ghp,blgp->blgh', c_re.astype(f32), h_re)
         - jnp.einsum('ghp,blgp->blgh', c_im.astype(f32), h_im))
    y = y + d_skip.astype(f32).reshape(S5_GROUPS, S5_GROUP) * uf
    return y.reshape(Bn, L, D_S5).astype(u.dtype)


def memory_attention(q, mk, mv):
    Bn, L = q.shape[:2]
    s = jnp.einsum('blhd,bmhd->bhlm', q.astype(jnp.float32), mk.astype(jnp.float32)) * (MEM_HEAD_DIM ** -0.5)
    p = jax.nn.softmax(s, axis=-1)
    o = jnp.einsum('bhlm,bmhd->blhd', p, mv.astype(jnp.float32))
    return o.reshape(Bn, L, D_MEM).astype(q.dtype)


def setup_inputs(seed: int = 0) -> dict:
    key = jax.random.key(seed)
    ks = jax.random.split(key, 24)
    f32 = jnp.float32
    nrm = lambda k, shape, scale: jax.random.normal(k, shape, f32) * scale
    x = nrm(ks[0], (BATCH, SEQ, D_MODEL), 1.0)
    mem = nrm(ks[1], (BATCH, MEM_LEN, D_MODEL), 1.0)
    g_norm = 1.0 + nrm(ks[2], (DEPTH, D_MODEL), 0.02)
    g_mem_norm = 1.0 + nrm(ks[3], (DEPTH, D_MODEL), 0.02)
    g_final = 1.0 + nrm(ks[4], (D_MODEL,), 0.02)
    w_in = nrm(ks[5], (DEPTH, D_MODEL, N_IN), D_MODEL ** -0.5)
    b_forget = jax.random.uniform(ks[6], (DEPTH, FOX_HEADS), f32, 1.0, 4.0)
    b_merge = nrm(ks[7], (DEPTH, N_BRANCH * D_MODEL), 0.01)
    w_mem_kv = nrm(ks[8], (DEPTH, D_MODEL, 2 * D_MEM), D_MODEL ** -0.5)
    lam_re = -0.5 + nrm(ks[9], (DEPTH, S5_GROUPS, S5_STATE), 0.01)
    lam_im = (math.pi * jnp.arange(S5_STATE, dtype=f32))[None, None, :] + nrm(ks[10], (DEPTH, S5_GROUPS, S5_STATE), 0.01)
    log_step = jax.random.uniform(ks[11], (DEPTH, S5_GROUPS), f32, math.log(1e-3), math.log(1e-1))
    s5_b_re = nrm(ks[12], (DEPTH, S5_GROUPS, S5_STATE, S5_GROUP), (2 * S5_GROUP) ** -0.5)
    s5_b_im = nrm(ks[13], (DEPTH, S5_GROUPS, S5_STATE, S5_GROUP), (2 * S5_GROUP) ** -0.5)
    s5_c_re = nrm(ks[14], (DEPTH, S5_GROUPS, S5_GROUP, S5_STATE), (2 * S5_STATE) ** -0.5)
    s5_c_im = nrm(ks[15], (DEPTH, S5_GROUPS, S5_GROUP, S5_STATE), (2 * S5_STATE) ** -0.5)
    s5_d = nrm(ks[16], (DEPTH, D_S5), 1.0)
    w_glu = nrm(ks[17], (DEPTH, D_S5, D_S5), D_S5 ** -0.5)
    b_glu = nrm(ks[18], (DEPTH, D_S5), 0.01)
    w_proj_fox = nrm(ks[19], (DEPTH, D_FOX, D_MODEL), D_FOX ** -0.5)
    w_proj_s5 = nrm(ks[20], (DEPTH, D_S5, D_MODEL), D_S5 ** -0.5)
    w_proj_mem = nrm(ks[21], (DEPTH, D_MEM, D_MODEL), D_MEM ** -0.5)
    w_out = nrm(ks[22], (DEPTH, D_MODEL, D_MODEL), D_MODEL ** -0.5)
    return {"x": x, "mem": mem, "g_norm": g_norm, "g_mem_norm": g_mem_norm, "g_final": g_final,
            "w_in": w_in, "b_forget": b_forget, "b_merge": b_merge, "w_mem_kv": w_mem_kv,
            "lam_re": lam_re, "lam_im": lam_im, "log_step": log_step,
            "s5_b_re": s5_b_re, "s5_b_im": s5_b_im, "s5_c_re": s5_c_re, "s5_c_im": s5_c_im,
            "s5_d": s5_d, "w_glu": w_glu, "b_glu": b_glu,
            "w_proj_fox": w_proj_fox, "w_proj_s5": w_proj_s5, "w_proj_mem": w_proj_mem, "w_out": w_out}


def reference(x, mem, g_norm, g_mem_norm, g_final, w_in, b_forget, b_merge, w_mem_kv,
              lam_re, lam_im, log_step, s5_b_re, s5_b_im, s5_c_re, s5_c_im, s5_d,
              w_glu, b_glu, w_proj_fox, w_proj_s5, w_proj_mem, w_out):
    Bn, L, _ = x.shape
    M = mem.shape[1]
    for l in range(DEPTH):
        h = rms_norm(x, g_norm[l])
        z = h @ w_in[l]
        q, k, v, fl, gf, u, gs, qm, gm, gl = split_cols(z)

        log_f = jax.nn.log_sigmoid(fl.astype(jnp.float32) + b_forget[l].astype(jnp.float32))
        y_fox = forgetting_attention(q.reshape(Bn, L, FOX_HEADS, FOX_HEAD_DIM),
                                     k.reshape(Bn, L, FOX_HEADS, FOX_HEAD_DIM),
                                     v.reshape(Bn, L, FOX_HEADS, FOX_HEAD_DIM), log_f)
        y_fox = y_fox * jax.nn.silu(gf)

        y_s5 = s5_ssm(u, lam_re[l], lam_im[l], log_step[l], s5_b_re[l], s5_b_im[l],
                      s5_c_re[l], s5_c_im[l], s5_d[l])
        y_s5 = jax.nn.gelu(y_s5)
        y_s5 = y_s5 * jax.nn.sigmoid(y_s5 @ w_glu[l] + b_glu[l])
        y_s5 = y_s5 * jax.nn.silu(gs)

        kv = rms_norm(mem, g_mem_norm[l]) @ w_mem_kv[l]
        mk, mv = jnp.split(kv, 2, axis=-1)
        y_mem = memory_attention(qm.reshape(Bn, L, MEM_HEADS, MEM_HEAD_DIM),
                                 mk.reshape(Bn, M, MEM_HEADS, MEM_HEAD_DIM),
                                 mv.reshape(Bn, M, MEM_HEADS, MEM_HEAD_DIM))
        y_mem = y_mem * jax.nn.silu(gm)

        gates = jax.nn.sigmoid(gl + b_merge[l]).reshape(Bn, L, N_BRANCH, D_MODEL)
        merged = (gates[:, :, 0] * (y_fox @ w_proj_fox[l])
                  + gates[:, :, 1] * (y_s5 @ w_proj_s5[l])
                  + gates[:, :, 2] * (y_mem @ w_proj_mem[l]))
        x = x + merged @ w_out[l]
    return rms_norm(x, g_final)
```

```python
import functools
import math

import jax
import jax.numpy as jnp
from jax import lax
from jax.experimental import pallas as pl
from jax.experimental.pallas import tpu as pltpu

F32 = jnp.float32
BF16 = jnp.bfloat16
HIGHEST = lax.Precision.HIGHEST

D_MODEL = 1024
EPS = 1e-6
NEG = -1e30

FOX_HEAD_DIM = 64
D_FOX = 768
FOX_HEADS = D_FOX // FOX_HEAD_DIM

D_S5 = 768
S5_GROUP = 16
S5_GROUPS = D_S5 // S5_GROUP
S5_STATE = 64

D_MEM = 512
MEM_HEADS = 4
MEM_HEAD_DIM = D_MEM // MEM_HEADS

LANES = 128
S5_CHUNK = 16
S5_LANE_GROUPS = D_S5 // LANES
S5_GROUPS_PER_LANE_GROUP = LANES // S5_GROUP
S5_CW = S5_CHUNK * LANES
S5_HALF = S5_GROUPS_PER_LANE_GROUP * S5_STATE
S5_SW = 2 * S5_HALF

G_GF, G_GS, G_GM, G_QM, G_GL = 0, 768, 1536, 2048, 2560
G_COLS = G_GL + 3 * D_MODEL

VMEM_LIMIT = 56 * 1024 * 1024


def _dot(a, b, **kw):
    return jnp.dot(a, b, preferred_element_type=F32, **kw)


def _dot_nt(a, b, **kw):
    return lax.dot_general(a, b, (((1,), (1,)), ((), ())), preferred_element_type=F32, **kw)


def _rms(xf, g):
    return xf * lax.rsqrt(jnp.mean(xf * xf, axis=-1, keepdims=True) + EPS) * g


def _sigmoid(z):
    return 1.0 / (1.0 + jnp.exp(-z))


def _silu(z):
    return z * _sigmoid(z)


def _gelu_tanh(y):
    return 0.5 * y * (1.0 + jnp.tanh(math.sqrt(2.0 / math.pi) * (y + 0.044715 * (y * y * y))))


def _const_spec(shape):
    nd = len(shape)
    return pl.BlockSpec(shape, lambda *_: (0,) * nd, pipeline_mode=pl.Buffered(1))


def _mem_kv_kernel(mem_ref, g_ref, w_ref, mk_ref, mv_ref):
    h = _rms(mem_ref[...], g_ref[...]).astype(BF16)
    kv = _dot(h, w_ref[...])
    mk_ref[...] = kv[:, :D_MEM].astype(BF16)
    mv_ref[...] = kv[:, D_MEM:].astype(BF16)


def _mem_kv(mem, g, w):
    bn, m, _ = mem.shape
    out = jax.ShapeDtypeStruct((bn, m, D_MEM), BF16)
    return pl.pallas_call(
        _mem_kv_kernel,
        grid=(bn,),
        in_specs=[pl.BlockSpec((None, m, D_MODEL), lambda b: (b, 0, 0)),
                  _const_spec((1, D_MODEL)),
                  _const_spec((D_MODEL, 2 * D_MEM))],
        out_specs=[pl.BlockSpec((None, m, D_MEM), lambda b: (b, 0, 0))] * 2,
        out_shape=[out, out],
        name="mem_kv",
    )(mem, g, w)


def _in_proj_kernel(x_ref, g_ref, w_ref, wfl_ref, bfl_ref, q_ref, k_ref, v_ref, uc_ref, f_ref,
                    us_scr, carry_scr, *, tm):
    @pl.when(pl.program_id(1) == 0)
    def _():
        carry_scr[...] = jnp.zeros_like(carry_scr)

    h = _rms(x_ref[...], g_ref[...]).astype(BF16)
    q_ref[...] = (_dot(h, w_ref[:, 0:D_FOX]) * (FOX_HEAD_DIM ** -0.5)).astype(BF16)
    k_ref[...] = _dot(h, w_ref[:, D_FOX:2 * D_FOX]).astype(BF16)
    v_ref[...] = _dot(h, w_ref[:, 2 * D_FOX:3 * D_FOX]).astype(BF16)

    u = _dot(h, w_ref[:, 3 * D_FOX:3 * D_FOX + D_S5])
    nc = tm // S5_CHUNK
    for lg in range(S5_LANE_GROUPS):
        us_scr[lg] = u[:, lg * LANES:(lg + 1) * LANES]
        for j in range(S5_CHUNK):
            c0 = lg * S5_CW + j * LANES
            uc_ref[:, c0:c0 + LANES] = us_scr[lg, pl.ds(j, nc, stride=S5_CHUNK), :].astype(BF16)

    z = _dot(h, wfl_ref[...]) + bfl_ref[...]
    logf = jnp.minimum(z, 0.0) - jnp.log1p(jnp.exp(-jnp.abs(z)))
    row = lax.broadcasted_iota(jnp.int32, (tm, tm), 0)
    col = lax.broadcasted_iota(jnp.int32, (tm, tm), 1)
    tri = (col <= row).astype(F32)
    f = _dot(tri, logf, precision=HIGHEST) + carry_scr[...]
    f_ref[...] = f
    carry_scr[...] = f[tm - 1:tm, :]


def _in_proj(x, g, w, wfl, bfl, *, tm):
    bn, l, _ = x.shape
    nc = tm // S5_CHUNK
    tok = lambda width: pl.BlockSpec((None, tm, width), lambda b, i: (b, i, 0))
    qkv = jax.ShapeDtypeStruct((bn, l, D_FOX), BF16)
    return pl.pallas_call(
        functools.partial(_in_proj_kernel, tm=tm),
        grid=(bn, l // tm),
        in_specs=[tok(D_MODEL),
                  _const_spec((1, D_MODEL)),
                  _const_spec((D_MODEL, 3 * D_FOX + D_S5)),
                  _const_spec((D_MODEL, LANES)),
                  _const_spec((1, LANES))],
        out_specs=[tok(D_FOX), tok(D_FOX), tok(D_FOX),
                   pl.BlockSpec((None, nc, S5_LANE_GROUPS * S5_CW), lambda b, i: (b, i, 0)),
                   tok(LANES)],
        out_shape=[qkv, qkv, qkv,
                   jax.ShapeDtypeStruct((bn, l // S5_CHUNK, S5_LANE_GROUPS * S5_CW), BF16),
                   jax.ShapeDtypeStruct((bn, l, LANES), F32)],
        scratch_shapes=[pltpu.VMEM((S5_LANE_GROUPS, tm, LANES), F32), pltpu.VMEM((1, LANES), F32)],
        compiler_params=pltpu.CompilerParams(
            dimension_semantics=("arbitrary", "arbitrary"), vmem_limit_bytes=VMEM_LIMIT),
        name="in_proj",
    )(x, g, w, wfl, bfl)


def _fox_kernel(q_ref, k_ref, v_ref, fcol_ref, frow_ref, o_ref, *, t):
    pair = pl.program_id(1)
    qi = pl.program_id(2)
    lane = lax.broadcasted_iota(jnp.int32, (1, LANES), 1)
    q = q_ref[...]
    fcol = fcol_ref[...]
    row = lax.broadcasted_iota(jnp.int32, (t, t), 0)
    col = lax.broadcasted_iota(jnp.int32, (t, t), 1)

    outs = []
    for e in range(2):
        head_lanes = (lane >= e * FOX_HEAD_DIM) & (lane < (e + 1) * FOX_HEAD_DIM)
        qe = jnp.where(head_lanes, q, jnp.zeros_like(q))
        fq = jnp.sum(jnp.where(lane == 2 * pair + e, fcol, 0.0), axis=-1, keepdims=True)

        def tile(j, carry, diagonal, qe=qe, fq=fq, e=e):
            m, l, acc = carry
            k0 = pl.multiple_of(j * t, t)
            kj = k_ref[pl.ds(k0, t), :]
            vj = v_ref[pl.ds(k0, t), :]
            fk = frow_ref[e:e + 1, pl.ds(k0, t)]
            s = _dot_nt(qe, kj) - fk
            if diagonal:
                s = jnp.where(col <= row, s, NEG)
            m_new = jnp.maximum(m, jnp.max(s, axis=-1, keepdims=True) + fq)
            p = jnp.exp(s + (fq - m_new))
            alpha = jnp.exp(m - m_new)
            l = alpha * l + jnp.sum(p, axis=-1, keepdims=True)
            acc = alpha * acc + _dot(p.astype(BF16), vj)
            return m_new, l, acc

        init = (jnp.full((t, 1), NEG, F32), jnp.zeros((t, 1), F32), jnp.zeros((t, LANES), F32))
        carry = lax.fori_loop(0, qi, functools.partial(tile, diagonal=False), init)
        _, l, acc = tile(qi, carry, diagonal=True)
        outs.append(acc * (1.0 / l))
    o_ref[...] = jnp.where(lane < FOX_HEAD_DIM, outs[0], outs[1]).astype(BF16)


def _fox(q, k, v, fcol, frow, *, t):
    bn, l, _ = q.shape
    pairs = FOX_HEADS // 2
    return pl.pallas_call(
        functools.partial(_fox_kernel, t=t),
        grid=(bn, pairs, l // t),
        in_specs=[pl.BlockSpec((None, t, LANES), lambda b, p, i: (b, i, p)),
                  pl.BlockSpec((None, l, LANES), lambda b, p, i: (b, 0, p)),
                  pl.BlockSpec((None, l, LANES), lambda b, p, i: (b, 0, p)),
                  pl.BlockSpec((None, t, LANES), lambda b, p, i: (b, i, 0)),
                  pl.BlockSpec((None, None, 2, l), lambda b, p, i: (b, p, 0, 0))],
        out_specs=pl.BlockSpec((None, t, LANES), lambda b, p, i: (b, i, p)),
        out_shape=jax.ShapeDtypeStruct((bn, l, D_FOX), BF16),
        compiler_params=pltpu.CompilerParams(
            dimension_semantics=("arbitrary", "arbitrary", "arbitrary"), vmem_limit_bytes=VMEM_LIMIT),
        name="fox",
    )(q, k, v, fcol, frow)


def _s5_tables_kernel(lr_ref, li_ref, ls_ref, btr_ref, bti_ref, cr_ref, ci_ref, d_ref,
                      k_ref, qr_ref, qi_ref, rr_ref, ri_ref, p_ref):
    lr, li = lr_ref[...], li_ref[...]
    step = jnp.exp(ls_ref[...])
    mag = jnp.exp(lr * step)
    ab_re, ab_im = mag * jnp.cos(li * step), mag * jnp.sin(li * step)
    den = lr * lr + li * li
    nr, ni = ab_re - 1.0, ab_im
    f_re = (nr * lr + ni * li) / den
    f_im = (ni * lr - nr * li) / den
    btr, bti = btr_ref[...], bti_ref[...]
    bb_re = f_re * btr - f_im * bti
    bb_im = f_re * bti + f_im * btr
    cr, ci = cr_ref[...], ci_ref[...]
    eye = (lax.broadcasted_iota(jnp.int32, (S5_GROUP, S5_GROUP), 0)
           == lax.broadcasted_iota(jnp.int32, (S5_GROUP, S5_GROUP), 1))
    pr = jnp.ones_like(lr)
    pi = jnp.zeros_like(lr)
    for tau in range(S5_CHUNK + 1):
        a_re = cr * pr - ci * pi
        a_im = cr * pi + ci * pr
        if tau < S5_CHUNK:
            rows = slice(tau * S5_GROUP, (tau + 1) * S5_GROUP)
            kt = _dot_nt(a_re, bb_re, precision=HIGHEST) - _dot_nt(a_im, bb_im, precision=HIGHEST)
            if tau == 0:
                kt = kt + jnp.where(eye, d_ref[...], 0.0)
            k_ref[rows, :] = kt
            qr_ref[rows, :] = pr * bb_re - pi * bb_im
            qi_ref[rows, :] = pr * bb_im + pi * bb_re
        if tau >= 1:
            rows = slice((tau - 1) * S5_GROUP, tau * S5_GROUP)
            rr_ref[rows, :] = a_re
            ri_ref[rows, :] = a_im
        if tau == S5_CHUNK:
            p_ref[:, 0:S5_STATE] = pr
            p_ref[:, S5_STATE:] = pi
        pr, pi = pr * ab_re - pi * ab_im, pr * ab_im + pi * ab_re


def _s5_tables(lam_re, lam_im, log_step, b_re, b_im, c_re, c_im, d):
    ng = S5_GROUPS
    vec = lambda a: a.reshape(ng, 1, -1).astype(F32)
    step = jnp.broadcast_to(log_step.reshape(ng, 1, 1).astype(F32), (ng, 1, S5_STATE))
    bt = lambda a: jnp.swapaxes(a.astype(F32), 1, 2)
    rows = S5_CHUNK * S5_GROUP
    spec = lambda r, c: pl.BlockSpec((None, r, c), lambda g: (g, 0, 0))
    tab = lambda c: jax.ShapeDtypeStruct((ng, rows, c), F32)
    return pl.pallas_call(
        _s5_tables_kernel,
        grid=(ng,),
        in_specs=[spec(1, S5_STATE)] * 3 + [spec(S5_GROUP, S5_STATE)] * 4 + [spec(1, S5_GROUP)],
        out_specs=[spec(rows, S5_GROUP)] + [spec(rows, S5_STATE)] * 4 + [spec(1, 2 * S5_STATE)],
        out_shape=[tab(S5_GROUP)] + [tab(S5_STATE)] * 4 + [jax.ShapeDtypeStruct((ng, 1, 2 * S5_STATE), F32)],
        name="s5_tables",
    )(vec(lam_re), vec(lam_im), step, bt(b_re), bt(b_im), c_re.astype(F32), c_im.astype(F32), vec(d))


def _s5_assemble(ktab, qr, qi, rr, ri, p16):
    lg, gp, c, st = S5_LANE_GROUPS, S5_GROUPS_PER_LANE_GROUP, S5_CHUNK, S5_STATE
    eye = jnp.eye(gp, dtype=F32)
    kt = ktab.reshape(lg, gp, c, S5_GROUP, S5_GROUP)
    lag = jnp.arange(c)[None, :] - jnp.arange(c)[:, None]
    tg = jnp.where((lag >= 0)[None, None, :, :, None, None], kt[:, :, jnp.clip(lag, 0, c - 1)], 0.0)
    wtoe = jnp.einsum("Ggabcd,gh->Gagdbhc", tg, eye).reshape(lg, S5_CW, S5_CW)
    q = jnp.stack([qr, qi]).reshape(2, lg, gp, c, S5_GROUP, st)[:, :, :, ::-1]
    win = jnp.einsum("rGgjdp,gh->Gjgdrhp", q, eye).reshape(lg, S5_CW, S5_SW)
    r = jnp.stack([rr, -ri]).reshape(2, lg, gp, c, S5_GROUP, st)
    vout = jnp.einsum("rGgjcp,gh->Grgpjhc", r, eye).reshape(lg, S5_SW, S5_CW)
    lam = p16.reshape(lg, gp, 2, st).transpose(0, 2, 1, 3).reshape(lg, 1, S5_SW)
    return wtoe.astype(BF16), win.astype(BF16), vout.astype(BF16), lam


def _s5_kernel(uc_ref, wtoe_ref, win_ref, vout_ref, lam_ref, yc_ref, e_scr, hs_scr, *, n):
    x = uc_ref[...]
    e_scr[...] = _dot(x, win_ref[...])
    lam = lam_ref[...]
    lr, li = lam[:, :S5_HALF], lam[:, S5_HALF:]

    def step(i, h):
        hr, hi = h
        hs_scr[pl.ds(i, 1), 0:S5_HALF] = hr
        hs_scr[pl.ds(i, 1), S5_HALF:] = hi
        er = e_scr[pl.ds(i, 1), 0:S5_HALF]
        ei = e_scr[pl.ds(i, 1), S5_HALF:]
        return lr * hr - li * hi + er, lr * hi + li * hr + ei

    zero = jnp.zeros((1, S5_HALF), F32)
    lax.fori_loop(0, n, step, (zero, zero))
    hs = hs_scr[...].astype(BF16)
    blk = 2 * LANES
    for jb in range(S5_CW // blk):
        cols = slice(jb * blk, (jb + 1) * blk)
        kk = (jb + 1) * blk
        y = _dot(x[:, :kk], wtoe_ref[0:kk, cols]) + _dot(hs, vout_ref[:, cols])
        yc_ref[:, cols] = y.astype(BF16)


def _s5(uc, wtoe, win, vout, lam):
    bn, n, _ = uc.shape
    per_lg = lambda r, c: pl.BlockSpec((None, r, c), lambda g, b: (g, 0, 0))
    return pl.pallas_call(
        functools.partial(_s5_kernel, n=n),
        grid=(S5_LANE_GROUPS, bn),
        in_specs=[pl.BlockSpec((None, n, S5_CW), lambda g, b: (b, 0, g)),
                  per_lg(S5_CW, S5_CW), per_lg(S5_CW, S5_SW), per_lg(S5_SW, S5_CW), per_lg(1, S5_SW)],
        out_specs=pl.BlockSpec((None, n, S5_CW), lambda g, b: (b, 0, g)),
        out_shape=jax.ShapeDtypeStruct(uc.shape, BF16),
        scratch_shapes=[pltpu.VMEM((n, S5_SW), F32), pltpu.VMEM((n, S5_SW), F32)],
        compiler_params=pltpu.CompilerParams(
            dimension_semantics=("arbitrary", "arbitrary"), vmem_limit_bytes=VMEM_LIMIT),
        name="s5",
    )(uc, wtoe, win, vout, lam)


def _final_kernel(x_ref, yfox_ref, yc_ref, mk_ref, mv_ref, gn_ref, gfin_ref, wg_ref, bm_ref,
                  wglu_ref, bglu_ref, wpf_ref, wps_ref, wpm_ref, wout_ref, o_ref, ys_scr, *, tm):
    x = x_ref[...]
    h = _rms(x, gn_ref[...]).astype(BF16)

    def proj(c0, width):
        return _dot(h, wg_ref[:, c0:c0 + width])

    def gate(i):
        return _sigmoid(proj(G_GL + i * D_MODEL, D_MODEL) + bm_ref[:, i * D_MODEL:(i + 1) * D_MODEL])

    a = (yfox_ref[...].astype(F32) * _silu(proj(G_GF, D_FOX))).astype(BF16)
    merged = gate(0) * _dot(a, wpf_ref[...])

    nc = tm // S5_CHUNK
    for lg in range(S5_LANE_GROUPS):
        for j in range(S5_CHUNK):
            c0 = lg * S5_CW + j * LANES
            ys_scr[lg, pl.ds(j, nc, stride=S5_CHUNK), :] = yc_ref[:, c0:c0 + LANES].astype(F32)
    y = _gelu_tanh(jnp.concatenate([ys_scr[lg] for lg in range(S5_LANE_GROUPS)], axis=-1))
    y = y * _sigmoid(_dot(y.astype(BF16), wglu_ref[...]) + bglu_ref[...])
    y = (y * _silu(proj(G_GS, D_S5))).astype(BF16)
    merged = merged + gate(1) * _dot(y, wps_ref[...])

    qm = (proj(G_QM, D_MEM) * (MEM_HEAD_DIM ** -0.5)).astype(BF16)
    heads = []
    for hd in range(MEM_HEADS):
        cols = slice(hd * MEM_HEAD_DIM, (hd + 1) * MEM_HEAD_DIM)
        s = _dot_nt(qm[:, cols], mk_ref[:, cols])
        p = jnp.exp(s - jnp.max(s, axis=-1, keepdims=True))
        inv = 1.0 / jnp.sum(p, axis=-1, keepdims=True)
        heads.append(_dot(p.astype(BF16), mv_ref[:, cols]) * inv)
    ymem = jnp.concatenate(heads, axis=-1)
    a = (ymem * _silu(proj(G_GM, D_MEM))).astype(BF16)
    merged = merged + gate(2) * _dot(a, wpm_ref[...])

    out = x + _dot(merged.astype(BF16), wout_ref[...])
    o_ref[...] = _rms(out, gfin_ref[...])


def _final(x, yfox, yc, mk, mv, gn, gfin, wg, bm, wglu, bglu, wpf, wps, wpm, wout, *, tm):
    bn, l, _ = x.shape
    m = mk.shape[1]
    nc = tm // S5_CHUNK
    tok = lambda width: pl.BlockSpec((None, tm, width), lambda b, i: (b, i, 0))
    mem = pl.BlockSpec((None, m, D_MEM), lambda b, i: (b, 0, 0))
    return pl.pallas_call(
        functools.partial(_final_kernel, tm=tm),
        grid=(bn, l // tm),
        in_specs=[tok(D_MODEL), tok(D_FOX),
                  pl.BlockSpec((None, nc, S5_LANE_GROUPS * S5_CW), lambda b, i: (b, i, 0)),
                  mem, mem,
                  _const_spec((1, D_MODEL)), _const_spec((1, D_MODEL)),
                  _const_spec((D_MODEL, G_COLS)), _const_spec((1, 3 * D_MODEL)),
                  _const_spec((D_S5, D_S5)), _const_spec((1, D_S5)),
                  _const_spec((D_FOX, D_MODEL)), _const_spec((D_S5, D_MODEL)),
                  _const_spec((D_MEM, D_MODEL)), _const_spec((D_MODEL, D_MODEL))],
        out_specs=tok(D_MODEL),
        out_shape=jax.ShapeDtypeStruct(x.shape, x.dtype),
        scratch_shapes=[pltpu.VMEM((S5_LANE_GROUPS, tm, LANES), F32)],
        compiler_params=pltpu.CompilerParams(
            dimension_semantics=("arbitrary", "arbitrary"), vmem_limit_bytes=VMEM_LIMIT),
        name="final",
    )(x, yfox, yc, mk, mv, gn, gfin, wg, bm, wglu, bglu, wpf, wps, wpm, wout)


def _split_w_in(w):
    sizes = (D_FOX, D_FOX, D_FOX, FOX_HEADS, D_FOX, D_S5, D_S5, D_MEM, D_MEM, 3 * D_MODEL)
    out, c0 = [], 0
    for s in sizes:
        out.append(w[:, c0:c0 + s])
        c0 += s
    return out


def _layer(x, mem, g_norm, g_mem_norm, w_in, b_forget, b_merge, w_mem_kv, lam_re, lam_im, log_step,
           s5_b_re, s5_b_im, s5_c_re, s5_c_im, s5_d, w_glu, b_glu, w_proj_fox, w_proj_s5, w_proj_mem,
           w_out, g_out, *, tm_in, t_attn, tm_out):
    bn, l, _ = x.shape
    row = lambda a: a.reshape(1, -1).astype(F32)
    wq, wk, wv, wfl, wgf, wu, wgs, wqm, wgm, wgl = _split_w_in(w_in)
    w_tok = jnp.concatenate([wq, wk, wv, wu], axis=1).astype(BF16)
    wfl_pad = jnp.pad(wfl, ((0, 0), (0, LANES - FOX_HEADS))).astype(BF16)
    bfl_pad = jnp.pad(row(b_forget), ((0, 0), (0, LANES - FOX_HEADS)))
    w_gate = jnp.concatenate([wgf, wgs, wgm, wqm, wgl], axis=1).astype(BF16)

    mk, mv = _mem_kv(mem, row(g_mem_norm), w_mem_kv.astype(BF16))
    q, k, v, uc, fcol = _in_proj(x, row(g_norm), w_tok, wfl_pad, bfl_pad, tm=tm_in)
    frow = jnp.swapaxes(fcol[:, :, :FOX_HEADS], 1, 2).reshape(bn, FOX_HEADS // 2, 2, l)
    yfox = _fox(q, k, v, fcol, frow, t=t_attn)
    tables = _s5_tables(lam_re, lam_im, log_step, s5_b_re, s5_b_im, s5_c_re, s5_c_im, s5_d)
    yc = _s5(uc, *_s5_assemble(*tables))
    return _final(x, yfox, yc, mk, mv, row(g_norm), row(g_out), w_gate, row(b_merge),
                  w_glu.astype(BF16), row(b_glu), w_proj_fox.astype(BF16), w_proj_s5.astype(BF16),
                  w_proj_mem.astype(BF16), w_out.astype(BF16), tm=tm_out)


def kernel(x, mem, g_norm, g_mem_norm, g_final, w_in, b_forget, b_merge, w_mem_kv, lam_re, lam_im, log_step,
           s5_b_re, s5_b_im, s5_c_re, s5_c_im, s5_d, w_glu, b_glu, w_proj_fox, w_proj_s5, w_proj_mem, w_out):
    depth = w_in.shape[0]
    assert depth == 1, "the fused final kernel applies the closing RMSNorm: one layer only"
    l = x.shape[1]
    tm_in = min(512, l)
    t_attn = min(512, l)
    tm_out = min(256, l)
    return _layer(x, mem, g_norm[0], g_mem_norm[0], w_in[0], b_forget[0], b_merge[0], w_mem_kv[0],
                  lam_re[0], lam_im[0], log_step[0], s5_b_re[0], s5_b_im[0], s5_c_re[0], s5_c_im[0],
                  s5_d[0], w_glu[0], b_glu[0], w_proj_fox[0], w_proj_s5[0], w_proj_mem[0], w_out[0],
                  g_final, tm_in=tm_in, t_attn=t_attn, tm_out=tm_out)
```

```python
import functools
import math

import jax
import jax.numpy as jnp
from jax import lax
from jax.experimental import pallas as pl
from jax.experimental.pallas import tpu as pltpu

F32 = jnp.float32
BF16 = jnp.bfloat16
HIGHEST = lax.Precision.HIGHEST

D_MODEL = 1024
EPS = 1e-6
NEG = -1e30
LOG2E = math.log2(math.e)
SKIP_LOG2 = 152.0
NORM_MARGIN = 1.03

FOX_HEAD_DIM = 64
D_FOX = 768
FOX_HEADS = D_FOX // FOX_HEAD_DIM

D_S5 = 768
S5_GROUP = 16
S5_GROUPS = D_S5 // S5_GROUP
S5_STATE = 64

D_MEM = 512
MEM_HEADS = 4
MEM_HEAD_DIM = D_MEM // MEM_HEADS

LANES = 128
S5_CHUNK = 16
S5_LANE_GROUPS = D_S5 // LANES
S5_GROUPS_PER_LANE_GROUP = LANES // S5_GROUP
S5_CW = S5_CHUNK * LANES
S5_HALF = S5_GROUPS_PER_LANE_GROUP * S5_STATE
S5_SW = 2 * S5_HALF

G_GF, G_GS, G_GM, G_QM, G_GL = 0, 768, 1536, 2048, 2560
G_COLS = G_GL + 3 * D_MODEL

VMEM_LIMIT = 56 * 1024 * 1024


def _dot(a, b, **kw):
    return jnp.dot(a, b, preferred_element_type=F32, **kw)


def _dot_nt(a, b, **kw):
    return lax.dot_general(a, b, (((1,), (1,)), ((), ())), preferred_element_type=F32, **kw)


def _rms(xf, g):
    return xf * lax.rsqrt(jnp.mean(xf * xf, axis=-1, keepdims=True) + EPS) * g


def _sigmoid(z):
    return 1.0 / (1.0 + jnp.exp(-z))


def _silu(z):
    return z * _sigmoid(z)


def _gelu_tanh(y):
    return 0.5 * y * (1.0 + jnp.tanh(math.sqrt(2.0 / math.pi) * (y + 0.044715 * (y * y * y))))


def _const_spec(shape):
    nd = len(shape)
    return pl.BlockSpec(shape, lambda *_: (0,) * nd, pipeline_mode=pl.Buffered(1))


def _mem_kv_kernel(mem_ref, g_ref, w_ref, mk_ref, mv_ref):
    h = _rms(mem_ref[...], g_ref[...]).astype(BF16)
    kv = _dot(h, w_ref[...])
    mk_ref[...] = kv[:, :D_MEM].astype(BF16)
    mv_ref[...] = kv[:, D_MEM:].astype(BF16)


def _mem_kv(mem, g, w):
    bn, m, _ = mem.shape
    out = jax.ShapeDtypeStruct((bn, m, D_MEM), BF16)
    return pl.pallas_call(
        _mem_kv_kernel,
        grid=(bn,),
        in_specs=[pl.BlockSpec((None, m, D_MODEL), lambda b: (b, 0, 0)),
                  _const_spec((1, D_MODEL)),
                  _const_spec((D_MODEL, 2 * D_MEM))],
        out_specs=[pl.BlockSpec((None, m, D_MEM), lambda b: (b, 0, 0))] * 2,
        out_shape=[out, out],
        name="mem_kv",
    )(mem, g, w)


def _in_proj_kernel(x_ref, g_ref, w_ref, wfl_ref, bfl_ref, q_ref, k_ref, v_ref, uc_ref, f_ref,
                    nq_ref, nk_ref, us_scr, carry_scr, *, tm):
    @pl.when(pl.program_id(1) == 0)
    def _():
        carry_scr[...] = jnp.zeros_like(carry_scr)

    h = _rms(x_ref[...], g_ref[...]).astype(BF16)
    qb = (_dot(h, w_ref[:, 0:D_FOX]) * (LOG2E * FOX_HEAD_DIM ** -0.5)).astype(BF16)
    kb = _dot(h, w_ref[:, D_FOX:2 * D_FOX]).astype(BF16)
    q_ref[...] = qb
    k_ref[...] = kb
    v_ref[...] = _dot(h, w_ref[:, 2 * D_FOX:3 * D_FOX]).astype(BF16)

    head_of_col = lax.broadcasted_iota(jnp.int32, (D_FOX, LANES), 0) // FOX_HEAD_DIM
    sel = (head_of_col == lax.broadcasted_iota(jnp.int32, (D_FOX, LANES), 1)).astype(BF16)
    sq = lambda a: (a.astype(F32) * a.astype(F32)).astype(BF16)
    nq_ref[...] = jnp.max(_dot(sq(qb), sel), axis=0, keepdims=True)
    nk_ref[...] = jnp.max(_dot(sq(kb), sel), axis=0, keepdims=True)

    u = _dot(h, w_ref[:, 3 * D_FOX:3 * D_FOX + D_S5])
    nc = tm // S5_CHUNK
    for lg in range(S5_LANE_GROUPS):
        us_scr[lg] = u[:, lg * LANES:(lg + 1) * LANES]
        for j in range(S5_CHUNK):
            c0 = lg * S5_CW + j * LANES
            uc_ref[:, c0:c0 + LANES] = us_scr[lg, pl.ds(j, nc, stride=S5_CHUNK), :].astype(BF16)

    z = _dot(h, wfl_ref[...]) + bfl_ref[...]
    logf = (jnp.minimum(z, 0.0) - jnp.log1p(jnp.exp(-jnp.abs(z)))) * LOG2E
    row = lax.broadcasted_iota(jnp.int32, (tm, tm), 0)
    col = lax.broadcasted_iota(jnp.int32, (tm, tm), 1)
    tri = (col <= row).astype(F32)
    f = _dot(tri, logf, precision=HIGHEST) + carry_scr[...]
    f_ref[...] = f
    carry_scr[...] = f[tm - 1:tm, :]


def _in_proj(x, g, w, wfl, bfl, *, tm):
    bn, l, _ = x.shape
    nc = tm // S5_CHUNK
    tok = lambda width: pl.BlockSpec((None, tm, width), lambda b, i: (b, i, 0))
    per_tile = pl.BlockSpec((None, None, 1, LANES), lambda b, i: (b, i, 0, 0))
    qkv = jax.ShapeDtypeStruct((bn, l, D_FOX), BF16)
    norms = jax.ShapeDtypeStruct((bn, l // tm, 1, LANES), F32)
    return pl.pallas_call(
        functools.partial(_in_proj_kernel, tm=tm),
        grid=(bn, l // tm),
        in_specs=[tok(D_MODEL),
                  _const_spec((1, D_MODEL)),
                  _const_spec((D_MODEL, 3 * D_FOX + D_S5)),
                  _const_spec((D_MODEL, LANES)),
                  _const_spec((1, LANES))],
        out_specs=[tok(D_FOX), tok(D_FOX), tok(D_FOX),
                   pl.BlockSpec((None, nc, S5_LANE_GROUPS * S5_CW), lambda b, i: (b, i, 0)),
                   tok(LANES), per_tile, per_tile],
        out_shape=[qkv, qkv, qkv,
                   jax.ShapeDtypeStruct((bn, l // S5_CHUNK, S5_LANE_GROUPS * S5_CW), BF16),
                   jax.ShapeDtypeStruct((bn, l, LANES), F32), norms, norms],
        scratch_shapes=[pltpu.VMEM((S5_LANE_GROUPS, tm, LANES), F32), pltpu.VMEM((1, LANES), F32)],
        compiler_params=pltpu.CompilerParams(
            dimension_semantics=("arbitrary", "arbitrary"), vmem_limit_bytes=VMEM_LIMIT),
        name="in_proj",
    )(x, g, w, wfl, bfl)


def _fox_kernel(fend_ref, qkb_ref, q_ref, k_ref, v_ref, fcol_ref, frow_ref, o_ref,
                m_scr, l_scr, alpha_scr, acc_scr, s_scr, p_scr, *, t, nt):
    b = pl.program_id(0)
    pair = pl.program_id(1)
    qi = pl.program_id(2)
    lane = lax.broadcasted_iota(jnp.int32, (1, LANES), 1)
    q = q_ref[...]
    fcol = fcol_ref[...]
    row = lax.broadcasted_iota(jnp.int32, (t, t), 0)
    col = lax.broadcasted_iota(jnp.int32, (t, t), 1)

    outs = []
    for e in range(2):
        head = 2 * pair + e
        base = (b * FOX_HEADS + head) * nt
        head_lanes = (lane >= e * FOX_HEAD_DIM) & (lane < (e + 1) * FOX_HEAD_DIM)
        qe = jnp.where(head_lanes, q, jnp.zeros_like(q))
        fq = jnp.sum(jnp.where(lane == head, fcol, 0.0), axis=-1, keepdims=True)

        def scores(j, e=e, qe=qe):
            k0 = pl.multiple_of(jnp.maximum(j, 0) * t, t)
            fk = frow_ref[e:e + 1, pl.ds(k0, t)]
            return _dot_nt(qe, k_ref[pl.ds(k0, t), :]) - fk

        def weighted_values(j):
            k0 = pl.multiple_of(j * t, t)
            acc_scr[...] = alpha_scr[...] * acc_scr[...] + _dot(p_scr[...], v_ref[pl.ds(k0, t), :])

        s = jnp.where(col <= row, scores(qi), NEG)
        m = jnp.max(s, axis=-1, keepdims=True) + fq
        p = jnp.exp2(s + (fq - m))
        m_scr[...] = m
        l_scr[...] = jnp.sum(p, axis=-1, keepdims=True)
        p_scr[...] = p.astype(BF16)
        alpha_scr[...] = jnp.zeros_like(m)
        acc_scr[...] = jnp.zeros_like(acc_scr)

        f_hi = fend_ref[base + jnp.maximum(qi - 1, 0)]
        slack = qkb_ref[base + qi] - jnp.min(m) + SKIP_LOG2

        def wanted(j, base=base, f_hi=f_hi, slack=slack):
            reach = f_hi - fend_ref[base + jnp.maximum(j, 0)] + slack
            return jnp.logical_and(j >= 0, reach >= 0.0).astype(jnp.int32)

        s_scr[...] = scores(qi - 1)

        def tile(carry, fq=fq, scores=scores, wanted=wanted, weighted_values=weighted_values):
            j, _ = carry
            weighted_values(j + 1)
            s = s_scr[...]
            m_old = m_scr[...]
            m_new = jnp.maximum(m_old, jnp.max(s, axis=-1, keepdims=True) + fq)
            p = jnp.exp2(s + (fq - m_new))
            alpha = jnp.exp2(m_old - m_new)
            m_scr[...] = m_new
            l_scr[...] = alpha * l_scr[...] + jnp.sum(p, axis=-1, keepdims=True)
            p_scr[...] = p.astype(BF16)
            alpha_scr[...] = alpha
            s_scr[...] = scores(j - 1)
            return j - 1, wanted(j - 1)

        j_stop, _ = lax.while_loop(lambda c: c[1] > 0, tile, (qi - 1, wanted(qi - 1)))
        weighted_values(j_stop + 1)
        outs.append(acc_scr[...] * (1.0 / l_scr[...]))
    o_ref[...] = jnp.where(lane < FOX_HEAD_DIM, outs[0], outs[1]).astype(BF16)


def _fox(q, k, v, fcol, frow, fend, qkb, *, t):
    bn, l, _ = q.shape
    pairs = FOX_HEADS // 2
    nt = l // t
    grid_spec = pltpu.PrefetchScalarGridSpec(
        num_scalar_prefetch=2,
        grid=(bn, pairs, nt),
        in_specs=[pl.BlockSpec((None, t, LANES), lambda b, p, i, *_: (b, i, p)),
                  pl.BlockSpec((None, l, LANES), lambda b, p, i, *_: (b, 0, p)),
                  pl.BlockSpec((None, l, LANES), lambda b, p, i, *_: (b, 0, p)),
                  pl.BlockSpec((None, t, LANES), lambda b, p, i, *_: (b, i, 0)),
                  pl.BlockSpec((None, None, 2, l), lambda b, p, i, *_: (b, p, 0, 0))],
        out_specs=pl.BlockSpec((None, t, LANES), lambda b, p, i, *_: (b, i, p)),
        scratch_shapes=[pltpu.VMEM((t, 1), F32), pltpu.VMEM((t, 1), F32), pltpu.VMEM((t, 1), F32),
                        pltpu.VMEM((t, LANES), F32), pltpu.VMEM((t, t), F32), pltpu.VMEM((t, t), BF16)])
    return pl.pallas_call(
        functools.partial(_fox_kernel, t=t, nt=nt),
        grid_spec=grid_spec,
        out_shape=jax.ShapeDtypeStruct((bn, l, D_FOX), BF16),
        compiler_params=pltpu.CompilerParams(
            dimension_semantics=("arbitrary", "arbitrary", "arbitrary"), vmem_limit_bytes=VMEM_LIMIT),
        name="fox",
    )(fend, qkb, q, k, v, fcol, frow)


def _fox_skip_tables(fcol, nq, nk, *, t):
    bn = fcol.shape[0]
    fend = jnp.swapaxes(fcol[:, t - 1::t, :FOX_HEADS], 1, 2)
    qmax = jnp.sqrt(jnp.swapaxes(nq[:, :, 0, :FOX_HEADS], 1, 2))
    kmax = jnp.sqrt(jnp.max(nk[:, :, 0, :FOX_HEADS], axis=1))
    qkb = NORM_MARGIN * qmax * kmax[:, :, None]
    return fend.reshape(-1), qkb.reshape(-1)


def _s5_tables_kernel(lr_ref, li_ref, ls_ref, btr_ref, bti_ref, cr_ref, ci_ref, d_ref,
                      wtoe_ref, win_ref, voutt_ref, lam_ref):
    lr, li = lr_ref[...], li_ref[...]
    step = jnp.exp(ls_ref[...])
    mag = jnp.exp(lr * step)
    ab_re, ab_im = mag * jnp.cos(li * step), mag * jnp.sin(li * step)
    den = lr * lr + li * li
    nr, ni = ab_re - 1.0, ab_im
    f_re = (nr * lr + ni * li) / den
    f_im = (ni * lr - nr * li) / den
    btr, bti = btr_ref[...], bti_ref[...]
    bb_re = f_re * btr - f_im * bti
    bb_im = f_re * bti + f_im * btr
    cr, ci = cr_ref[...], ci_ref[...]
    eye = (lax.broadcasted_iota(jnp.int32, (LANES, LANES), 0)
           == lax.broadcasted_iota(jnp.int32, (LANES, LANES), 1))
    tile = lambda j: slice(j * LANES, (j + 1) * LANES)
    zeros = jnp.zeros((LANES, LANES), BF16)
    for jp in range(S5_CHUNK):
        for j in range(jp):
            wtoe_ref[tile(jp), tile(j)] = zeros
    pr = jnp.ones_like(lr)
    pi = jnp.zeros_like(lr)
    for tau in range(S5_CHUNK + 1):
        a_re = cr * pr - ci * pi
        a_im = cr * pi + ci * pr
        if tau < S5_CHUNK:
            kt = _dot_nt(bb_re, a_re, precision=HIGHEST) - _dot_nt(bb_im, a_im, precision=HIGHEST)
            if tau == 0:
                kt = kt + jnp.where(eye, d_ref[...], 0.0)
            kt = kt.astype(BF16)
            for jp in range(S5_CHUNK - tau):
                wtoe_ref[tile(jp), tile(jp + tau)] = kt
            j = S5_CHUNK - 1 - tau
            win_ref[tile(j), 0:S5_HALF] = (pr * bb_re - pi * bb_im).astype(BF16)
            win_ref[tile(j), S5_HALF:] = (pr * bb_im + pi * bb_re).astype(BF16)
        if tau >= 1:
            voutt_ref[tile(tau - 1), 0:S5_HALF] = a_re.astype(BF16)
            voutt_ref[tile(tau - 1), S5_HALF:] = (-a_im).astype(BF16)
        if tau == S5_CHUNK:
            lam_ref[:, 0:S5_HALF] = pr
            lam_ref[:, S5_HALF:] = pi
        pr, pi = pr * ab_re - pi * ab_im, pr * ab_im + pi * ab_re


def _s5_tables(lam_re, lam_im, log_step, b_re, b_im, c_re, c_im, d):
    lg, gp = S5_LANE_GROUPS, S5_GROUPS_PER_LANE_GROUP
    eye = jnp.eye(gp, dtype=F32)

    def block_diag(a):
        a = a.astype(F32).reshape(lg, gp, S5_GROUP, S5_STATE)
        return jnp.einsum("Ggcp,gh->Ggchp", a, eye).reshape(lg, LANES, S5_HALF)

    lanes = lambda a: a.astype(F32).reshape(lg, 1, S5_HALF)
    step = jnp.broadcast_to(log_step.astype(F32)[:, None], (S5_GROUPS, S5_STATE))
    spec = lambda r, c: pl.BlockSpec((None, r, c), lambda g: (g, 0, 0))
    return pl.pallas_call(
        _s5_tables_kernel,
        grid=(lg,),
        in_specs=[spec(1, S5_HALF)] * 3 + [spec(LANES, S5_HALF)] * 4 + [spec(1, LANES)],
        out_specs=[spec(S5_CW, S5_CW), spec(S5_CW, S5_SW), spec(S5_CW, S5_SW), spec(1, S5_SW)],
        out_shape=[jax.ShapeDtypeStruct((lg, S5_CW, S5_CW), BF16),
                   jax.ShapeDtypeStruct((lg, S5_CW, S5_SW), BF16),
                   jax.ShapeDtypeStruct((lg, S5_CW, S5_SW), BF16),
                   jax.ShapeDtypeStruct((lg, 1, S5_SW), F32)],
        compiler_params=pltpu.CompilerParams(
            dimension_semantics=("arbitrary",), vmem_limit_bytes=VMEM_LIMIT),
        name="s5_tables",
    )(lanes(lam_re), lanes(lam_im), lanes(step),
      block_diag(jnp.swapaxes(b_re, 1, 2)), block_diag(jnp.swapaxes(b_im, 1, 2)),
      block_diag(c_re), block_diag(c_im), d.astype(F32).reshape(lg, 1, LANES))


def _s5_kernel(uc_ref, wtoe_ref, win_ref, voutt_ref, lam_ref, yc_ref, e_scr, hs_scr, *, n):
    x = uc_ref[...]
    e_scr[...] = _dot(x, win_ref[...])
    lam = lam_ref[...]
    lr, li = lam[:, :S5_HALF], lam[:, S5_HALF:]

    def step(i, h):
        hr, hi = h
        hs_scr[pl.ds(i, 1), 0:S5_HALF] = hr
        hs_scr[pl.ds(i, 1), S5_HALF:] = hi
        er = e_scr[pl.ds(i, 1), 0:S5_HALF]
        ei = e_scr[pl.ds(i, 1), S5_HALF:]
        return lr * hr - li * hi + er, lr * hi + li * hr + ei

    zero = jnp.zeros((1, S5_HALF), F32)
    lax.fori_loop(0, n, step, (zero, zero))
    hs = hs_scr[...].astype(BF16)
    blk = 2 * LANES
    for jb in range(S5_CW // blk):
        cols = slice(jb * blk, (jb + 1) * blk)
        kk = (jb + 1) * blk
        y = _dot(x[:, :kk], wtoe_ref[0:kk, cols]) + _dot_nt(hs, voutt_ref[cols, :])
        yc_ref[:, cols] = y.astype(BF16)


def _s5(uc, wtoe, win, voutt, lam):
    bn, n, _ = uc.shape
    per_lg = lambda r, c: pl.BlockSpec((None, r, c), lambda g, b: (g, 0, 0))
    return pl.pallas_call(
        functools.partial(_s5_kernel, n=n),
        grid=(S5_LANE_GROUPS, bn),
        in_specs=[pl.BlockSpec((None, n, S5_CW), lambda g, b: (b, 0, g)),
                  per_lg(S5_CW, S5_CW), per_lg(S5_CW, S5_SW), per_lg(S5_CW, S5_SW), per_lg(1, S5_SW)],
        out_specs=pl.BlockSpec((None, n, S5_CW), lambda g, b: (b, 0, g)),
        out_shape=jax.ShapeDtypeStruct(uc.shape, BF16),
        scratch_shapes=[pltpu.VMEM((n, S5_SW), F32), pltpu.VMEM((n, S5_SW), F32)],
        compiler_params=pltpu.CompilerParams(
            dimension_semantics=("arbitrary", "arbitrary"), vmem_limit_bytes=VMEM_LIMIT),
        name="s5",
    )(uc, wtoe, win, voutt, lam)


def _final_kernel(x_ref, yfox_ref, yc_ref, mk_ref, mv_ref, gn_ref, gfin_ref, wg_ref, bm_ref,
                  wglu_ref, bglu_ref, wpf_ref, wps_ref, wpm_ref, wout_ref, o_ref, ys_scr, *, tm):
    x = x_ref[...]
    h = _rms(x, gn_ref[...]).astype(BF16)

    def proj(c0, width):
        return _dot(h, wg_ref[:, c0:c0 + width])

    def gate(i):
        return _sigmoid(proj(G_GL + i * D_MODEL, D_MODEL) + bm_ref[:, i * D_MODEL:(i + 1) * D_MODEL])

    a = (yfox_ref[...].astype(F32) * _silu(proj(G_GF, D_FOX))).astype(BF16)
    merged = gate(0) * _dot(a, wpf_ref[...])

    nc = tm // S5_CHUNK
    for lg in range(S5_LANE_GROUPS):
        for j in range(S5_CHUNK):
            c0 = lg * S5_CW + j * LANES
            ys_scr[lg, pl.ds(j, nc, stride=S5_CHUNK), :] = yc_ref[:, c0:c0 + LANES].astype(F32)
    y = _gelu_tanh(jnp.concatenate([ys_scr[lg] for lg in range(S5_LANE_GROUPS)], axis=-1))
    y = y * _sigmoid(_dot(y.astype(BF16), wglu_ref[...]) + bglu_ref[...])
    y = (y * _silu(proj(G_GS, D_S5))).astype(BF16)
    merged = merged + gate(1) * _dot(y, wps_ref[...])

    qm = (proj(G_QM, D_MEM) * (MEM_HEAD_DIM ** -0.5)).astype(BF16)
    heads = []
    for hd in range(MEM_HEADS):
        cols = slice(hd * MEM_HEAD_DIM, (hd + 1) * MEM_HEAD_DIM)
        s = _dot_nt(qm[:, cols], mk_ref[:, cols])
        p = jnp.exp(s - jnp.max(s, axis=-1, keepdims=True))
        inv = 1.0 / jnp.sum(p, axis=-1, keepdims=True)
        heads.append(_dot(p.astype(BF16), mv_ref[:, cols]) * inv)
    ymem = jnp.concatenate(heads, axis=-1)
    a = (ymem * _silu(proj(G_GM, D_MEM))).astype(BF16)
    merged = merged + gate(2) * _dot(a, wpm_ref[...])

    out = x + _dot(merged.astype(BF16), wout_ref[...])
    o_ref[...] = _rms(out, gfin_ref[...])


def _final(x, yfox, yc, mk, mv, gn, gfin, wg, bm, wglu, bglu, wpf, wps, wpm, wout, *, tm):
    bn, l, _ = x.shape
    m = mk.shape[1]
    nc = tm // S5_CHUNK
    tok = lambda width: pl.BlockSpec((None, tm, width), lambda b, i: (b, i, 0))
    mem = pl.BlockSpec((None, m, D_MEM), lambda b, i: (b, 0, 0))
    return pl.pallas_call(
        functools.partial(_final_kernel, tm=tm),
        grid=(bn, l // tm),
        in_specs=[tok(D_MODEL), tok(D_FOX),
                  pl.BlockSpec((None, nc, S5_LANE_GROUPS * S5_CW), lambda b, i: (b, i, 0)),
                  mem, mem,
                  _const_spec((1, D_MODEL)), _const_spec((1, D_MODEL)),
                  _const_spec((D_MODEL, G_COLS)), _const_spec((1, 3 * D_MODEL)),
                  _const_spec((D_S5, D_S5)), _const_spec((1, D_S5)),
                  _const_spec((D_FOX, D_MODEL)), _const_spec((D_S5, D_MODEL)),
                  _const_spec((D_MEM, D_MODEL)), _const_spec((D_MODEL, D_MODEL))],
        out_specs=tok(D_MODEL),
        out_shape=jax.ShapeDtypeStruct(x.shape, x.dtype),
        scratch_shapes=[pltpu.VMEM((S5_LANE_GROUPS, tm, LANES), F32)],
        compiler_params=pltpu.CompilerParams(
            dimension_semantics=("arbitrary", "arbitrary"), vmem_limit_bytes=VMEM_LIMIT),
        name="final",
    )(x, yfox, yc, mk, mv, gn, gfin, wg, bm, wglu, bglu, wpf, wps, wpm, wout)


def _split_w_in(w):
    sizes = (D_FOX, D_FOX, D_FOX, FOX_HEADS, D_FOX, D_S5, D_S5, D_MEM, D_MEM, 3 * D_MODEL)
    out, c0 = [], 0
    for s in sizes:
        out.append(w[:, c0:c0 + s])
        c0 += s
    return out


def _layer(x, mem, g_norm, g_mem_norm, w_in, b_forget, b_merge, w_mem_kv, lam_re, lam_im, log_step,
           s5_b_re, s5_b_im, s5_c_re, s5_c_im, s5_d, w_glu, b_glu, w_proj_fox, w_proj_s5, w_proj_mem,
           w_out, g_out, *, tm_in, t_attn, tm_out):
    bn, l, _ = x.shape
    assert tm_in == t_attn, "the per-tile q norms from in_proj are indexed by attention query tile"
    row = lambda a: a.reshape(1, -1).astype(F32)
    wq, wk, wv, wfl, wgf, wu, wgs, wqm, wgm, wgl = _split_w_in(w_in)
    w_tok = jnp.concatenate([wq, wk, wv, wu], axis=1).astype(BF16)
    wfl_pad = jnp.pad(wfl, ((0, 0), (0, LANES - FOX_HEADS))).astype(BF16)
    bfl_pad = jnp.pad(row(b_forget), ((0, 0), (0, LANES - FOX_HEADS)))
    w_gate = jnp.concatenate([wgf, wgs, wgm, wqm, wgl], axis=1).astype(BF16)

    mk, mv = _mem_kv(mem, row(g_mem_norm), w_mem_kv.astype(BF16))
    q, k, v, uc, fcol, nq, nk = _in_proj(x, row(g_norm), w_tok, wfl_pad, bfl_pad, tm=tm_in)
    frow = jnp.swapaxes(fcol[:, :, :FOX_HEADS], 1, 2).reshape(bn, FOX_HEADS // 2, 2, l)
    fend, qkb = _fox_skip_tables(fcol, nq, nk, t=t_attn)
    yfox = _fox(q, k, v, fcol, frow, fend, qkb, t=t_attn)
    tables = _s5_tables(lam_re, lam_im, log_step, s5_b_re, s5_b_im, s5_c_re, s5_c_im, s5_d)
    yc = _s5(uc, *tables)
    return _final(x, yfox, yc, mk, mv, row(g_norm), row(g_out), w_gate, row(b_merge),
                  w_glu.astype(BF16), row(b_glu), w_proj_fox.astype(BF16), w_proj_s5.astype(BF16),
                  w_proj_mem.astype(BF16), w_out.astype(BF16), tm=tm_out)


def kernel(x, mem, g_norm, g_mem_norm, g_final, w_in, b_forget, b_merge, w_mem_kv, lam_re, lam_im, log_step,
           s5_b_re, s5_b_im, s5_c_re, s5_c_im, s5_d, w_glu, b_glu, w_proj_fox, w_proj_s5, w_proj_mem, w_out):
    depth = w_in.shape[0]
    assert depth == 1, "the fused final kernel applies the closing RMSNorm: one layer only"
    l = x.shape[1]
    tm_in = min(512, l)
    t_attn = min(512, l)
    tm_out = min(256, l)
    return _layer(x, mem, g_norm[0], g_mem_norm[0], w_in[0], b_forget[0], b_merge[0], w_mem_kv[0],
                  lam_re[0], lam_im[0], log_step[0], s5_b_re[0], s5_b_im[0], s5_c_re[0], s5_c_im[0],
                  s5_d[0], w_glu[0], b_glu[0], w_proj_fox[0], w_proj_s5[0], w_proj_mem[0], w_out[0],
                  g_final, tm_in=tm_in, t_attn=t_attn, tm_out=tm_out)
```

```python
import functools
import math

import jax
import jax.numpy as jnp
import numpy as np
from jax import lax
from jax.experimental import pallas as pl
from jax.experimental.pallas import tpu as pltpu

F32 = jnp.float32
BF16 = jnp.bfloat16
HIGHEST = lax.Precision.HIGHEST

D_MODEL = 1024
EPS = 1e-6
NEG = -1e30
LOG2E = math.log2(math.e)
SKIP_LOG2 = 152.0
NORM_MARGIN = 1.03

FOX_HEAD_DIM = 64
D_FOX = 768
FOX_HEADS = D_FOX // FOX_HEAD_DIM

D_S5 = 768
S5_GROUP = 16
S5_GROUPS = D_S5 // S5_GROUP
S5_STATE = 64

D_MEM = 512
MEM_HEADS = 4
MEM_HEAD_DIM = D_MEM // MEM_HEADS

LANES = 128
S5_CHUNK = 16
S5_LANE_GROUPS = D_S5 // LANES
S5_GROUPS_PER_LANE_GROUP = LANES // S5_GROUP
S5_CW = S5_CHUNK * LANES
S5_HALF = S5_GROUPS_PER_LANE_GROUP * S5_STATE
S5_SW = 2 * S5_HALF

G_GF, G_GS, G_GM, G_QM, G_GL = 0, 768, 1536, 2048, 2560
G_COLS = G_GL + 3 * D_MODEL

VMEM_LIMIT = 56 * 1024 * 1024


def _dot(a, b, **kw):
    return jnp.dot(a, b, preferred_element_type=F32, **kw)


def _dot_nt(a, b, **kw):
    return lax.dot_general(a, b, (((1,), (1,)), ((), ())), preferred_element_type=F32, **kw)


def _rms(xf, g):
    return xf * lax.rsqrt(jnp.mean(xf * xf, axis=-1, keepdims=True) + EPS) * g


def _sigmoid(z):
    return 1.0 / (1.0 + jnp.exp(-z))


def _silu(z):
    return z * _sigmoid(z)


def _gelu_tanh(y):
    return 0.5 * y * (1.0 + jnp.tanh(math.sqrt(2.0 / math.pi) * (y + 0.044715 * (y * y * y))))


def _split3(a):
    hi = a.astype(BF16)
    rest = a - hi.astype(F32)
    mid = rest.astype(BF16)
    lo = (rest - mid.astype(F32)).astype(BF16)
    return jnp.concatenate([hi, mid, lo], axis=-1)


def _const_spec(shape):
    nd = len(shape)
    return pl.BlockSpec(shape, lambda *_: (0,) * nd, pipeline_mode=pl.Buffered(1))


def _mem_kv_kernel(mem_ref, g_ref, w_ref, mk_ref, mv_ref):
    h = _rms(mem_ref[...], g_ref[...]).astype(BF16)
    kv = _dot(h, w_ref[...])
    mk_ref[...] = kv[:, :D_MEM].astype(BF16)
    mv_ref[...] = kv[:, D_MEM:].astype(BF16)


def _mem_kv(mem, g, w):
    bn, m, _ = mem.shape
    out = jax.ShapeDtypeStruct((bn, m, D_MEM), BF16)
    return pl.pallas_call(
        _mem_kv_kernel,
        grid=(bn,),
        in_specs=[pl.BlockSpec((None, m, D_MODEL), lambda b: (b, 0, 0)),
                  _const_spec((1, D_MODEL)),
                  _const_spec((D_MODEL, 2 * D_MEM))],
        out_specs=[pl.BlockSpec((None, m, D_MEM), lambda b: (b, 0, 0))] * 2,
        out_shape=[out, out],
        name="mem_kv",
    )(mem, g, w)


def _in_proj_kernel(x_ref, g_ref, w_ref, wvt_ref, wfl_ref, bfl_ref, spread_ref, ones_ref,
                    q_ref, k_ref, vt_ref, uc_ref, fqa_ref, fka_ref, fend_ref, nq_ref, nk_ref,
                    us_scr, carry_scr, *, tm):
    @pl.when(pl.program_id(1) == 0)
    def _():
        carry_scr[...] = jnp.zeros_like(carry_scr)

    h = _rms(x_ref[...], g_ref[...]).astype(BF16)
    qb = (_dot(h, w_ref[:, 0:D_FOX]) * (LOG2E * FOX_HEAD_DIM ** -0.5)).astype(BF16)
    kb = _dot(h, w_ref[:, D_FOX:2 * D_FOX]).astype(BF16)
    q_ref[...] = qb
    k_ref[...] = kb
    for pair in range(FOX_HEADS // 2):
        vt_ref[pair] = _dot_nt(wvt_ref[pair * LANES:(pair + 1) * LANES, :], h).astype(BF16)

    head_of_col = lax.broadcasted_iota(jnp.int32, (D_FOX, LANES), 0) // FOX_HEAD_DIM
    sel = (head_of_col == lax.broadcasted_iota(jnp.int32, (D_FOX, LANES), 1)).astype(BF16)
    sq = lambda a: (a.astype(F32) * a.astype(F32)).astype(BF16)
    nq_ref[...] = jnp.max(_dot(sq(qb), sel), axis=0, keepdims=True)
    nk_ref[...] = jnp.max(_dot(sq(kb), sel), axis=0, keepdims=True)

    u = _dot(h, w_ref[:, 2 * D_FOX:2 * D_FOX + D_S5])
    nc = tm // S5_CHUNK
    for lg in range(S5_LANE_GROUPS):
        us_scr[lg] = u[:, lg * LANES:(lg + 1) * LANES]
        for j in range(S5_CHUNK):
            c0 = lg * S5_CW + j * LANES
            uc_ref[:, c0:c0 + LANES] = us_scr[lg, pl.ds(j, nc, stride=S5_CHUNK), :].astype(BF16)

    z = _dot(h, wfl_ref[...]) + bfl_ref[...]
    logf = (jnp.minimum(z, 0.0) - jnp.log1p(jnp.exp(-jnp.abs(z)))) * LOG2E
    row = lax.broadcasted_iota(jnp.int32, (tm, tm), 0)
    col = lax.broadcasted_iota(jnp.int32, (tm, tm), 1)
    tri = (col <= row).astype(BF16)
    parts = _dot(tri, _split3(logf))
    f = (parts[:, 0:LANES] + parts[:, LANES:2 * LANES]) + parts[:, 2 * LANES:] + carry_scr[...]
    carry_scr[...] = f[tm - 1:tm, :]
    fend_ref[...] = f[tm - 1:tm, :]
    aug = _dot(_split3(f), spread_ref[...]) + ones_ref[...]
    fka_ref[...] = aug[:, 0:LANES].astype(BF16)
    fqa_ref[...] = aug[:, LANES:].astype(BF16)


def _fox_aug_constants():
    spread = np.zeros((3 * LANES, 2 * LANES), np.float32)
    ones = np.zeros((1, 2 * LANES), np.float32)
    half = LANES // 2
    for h in range(FOX_HEADS):
        for i in range(3):
            spread[i * LANES + h, 3 * h + i] = 1.0
            spread[i * LANES + h, LANES + half + 3 * h + i] = 1.0
            ones[0, half + 3 * h + i] = 1.0
            ones[0, LANES + 3 * h + i] = -1.0
    return jnp.asarray(spread, BF16), jnp.asarray(ones, F32)


def _in_proj(x, g, w, wvt, wfl, bfl, *, tm):
    bn, l, _ = x.shape
    nc = tm // S5_CHUNK
    nt = l // tm
    pairs = FOX_HEADS // 2
    spread, ones = _fox_aug_constants()
    tok = lambda width: pl.BlockSpec((None, tm, width), lambda b, i: (b, i, 0))
    per_tile = pl.BlockSpec((None, None, 1, LANES), lambda b, i: (b, i, 0, 0))
    qk = jax.ShapeDtypeStruct((bn, l, D_FOX), BF16)
    aug = jax.ShapeDtypeStruct((bn, l, LANES), BF16)
    stat = jax.ShapeDtypeStruct((bn, nt, 1, LANES), F32)
    return pl.pallas_call(
        functools.partial(_in_proj_kernel, tm=tm),
        grid=(bn, nt),
        in_specs=[tok(D_MODEL),
                  _const_spec((1, D_MODEL)),
                  _const_spec((D_MODEL, 2 * D_FOX + D_S5)),
                  _const_spec((D_FOX, D_MODEL)),
                  _const_spec((D_MODEL, LANES)),
                  _const_spec((1, LANES)),
                  _const_spec((3 * LANES, 2 * LANES)),
                  _const_spec((1, 2 * LANES))],
        out_specs=[tok(D_FOX), tok(D_FOX),
                   pl.BlockSpec((None, pairs, None, LANES, tm), lambda b, i: (b, 0, i, 0, 0)),
                   pl.BlockSpec((None, nc, S5_LANE_GROUPS * S5_CW), lambda b, i: (b, i, 0)),
                   tok(LANES), tok(LANES), per_tile, per_tile, per_tile],
        out_shape=[qk, qk,
                   jax.ShapeDtypeStruct((bn, pairs, nt, LANES, tm), BF16),
                   jax.ShapeDtypeStruct((bn, l // S5_CHUNK, S5_LANE_GROUPS * S5_CW), BF16),
                   aug, aug, stat, stat, stat],
        scratch_shapes=[pltpu.VMEM((S5_LANE_GROUPS, tm, LANES), F32), pltpu.VMEM((1, LANES), F32)],
        compiler_params=pltpu.CompilerParams(
            dimension_semantics=("arbitrary", "arbitrary"), vmem_limit_bytes=VMEM_LIMIT),
        name="in_proj",
    )(x, g, w, wvt, wfl, bfl, spread, ones)


def _fox_kernel(fend_ref, qkb_ref, q_ref, fqa_ref, k_ref, fka_ref, vt_ref, o_ref,
                m_scr, l_scr, alpha_scr, acc_scr, s_scr, p_scr, *, t, nt):
    b = pl.program_id(0)
    pair = pl.program_id(1)
    qi = pl.program_id(2)
    lane = lax.broadcasted_iota(jnp.int32, (1, LANES), 1)
    q = q_ref[...]
    fqa = fqa_ref[...]
    key_pos = lax.broadcasted_iota(jnp.int32, (t, t), 0)
    query_pos = lax.broadcasted_iota(jnp.int32, (t, t), 1)

    heads = (0, 1)
    q_aug, base = [], []
    for e in heads:
        head = 2 * pair + e
        head_lanes = (lane >= e * FOX_HEAD_DIM) & (lane < (e + 1) * FOX_HEAD_DIM)
        f_lanes = lax.rem(lane, LANES // 2) // 3 == head
        q_aug.append(jnp.concatenate([jnp.where(head_lanes, q, jnp.zeros_like(q)),
                                      jnp.where(f_lanes, fqa, jnp.zeros_like(fqa))], axis=-1))
        base.append((b * FOX_HEADS + head) * nt)

    def scores(e, j):
        k0 = pl.multiple_of(jnp.maximum(j, 0) * t, t)
        k_aug = jnp.concatenate([k_ref[pl.ds(k0, t), :], fka_ref[pl.ds(k0, t), :]], axis=-1)
        return _dot_nt(k_aug, q_aug[e])

    def weighted_values(e, j):
        acc_scr[e] = alpha_scr[e] * acc_scr[e] + _dot(vt_ref[j], p_scr[e])

    slack = []
    for e in heads:
        s = jnp.where(key_pos <= query_pos, scores(e, qi), NEG)
        m = jnp.max(s, axis=0, keepdims=True)
        p = jnp.exp2(s - m)
        m_scr[e] = m
        l_scr[e] = jnp.sum(p, axis=0, keepdims=True)
        p_scr[e] = p.astype(BF16)
        alpha_scr[e] = jnp.zeros_like(m)
        acc_scr[e] = jnp.zeros((LANES, t), F32)
        slack.append(qkb_ref[base[e] + qi] - jnp.min(m) + SKIP_LOG2)
        s_scr[e] = scores(e, qi - 1)

    def wanted(e, j):
        f_hi = fend_ref[base[e] + jnp.maximum(qi - 1, 0)]
        reach = f_hi - fend_ref[base[e] + jnp.maximum(j, 0)] + slack[e]
        return jnp.logical_and(j >= 0, reach >= 0.0).astype(jnp.int32)

    def trip(j, which):
        for e in which:
            weighted_values(e, j + 1)
        for e in which:
            s = s_scr[e]
            m_old = m_scr[e]
            m_new = jnp.maximum(m_old, jnp.max(s, axis=0, keepdims=True))
            p = jnp.exp2(s - m_new)
            alpha = jnp.exp2(m_old - m_new)
            m_scr[e] = m_new
            l_scr[e] = alpha * l_scr[e] + jnp.sum(p, axis=0, keepdims=True)
            p_scr[e] = p.astype(BF16)
            alpha_scr[e] = alpha
        for e in which:
            s_scr[e] = scores(e, j - 1)

    def walk(j, which):
        def go(jj):
            flag = wanted(which[0], jj)
            for e in which[1:]:
                flag = flag * wanted(e, jj)
            return flag

        def body(carry):
            trip(carry[0], which)
            return carry[0] - 1, go(carry[0] - 1)

        return lax.while_loop(lambda c: c[1] > 0, body, (j, go(j)))[0]

    j_both = walk(qi - 1, heads)
    outs = []
    for e in heads:
        j_stop = walk(j_both, (e,))
        weighted_values(e, j_stop + 1)
        outs.append(acc_scr[e] * (1.0 / l_scr[e]))
    dim = lax.broadcasted_iota(jnp.int32, (LANES, 1), 0)
    o_ref[...] = jnp.where(dim < FOX_HEAD_DIM, outs[0], outs[1]).T.astype(BF16)


def _fox(q, fqa, k, fka, vt, fend, qkb, *, t):
    bn, l, _ = q.shape
    pairs = FOX_HEADS // 2
    nt = l // t
    query_tile = pl.BlockSpec((None, t, LANES), lambda b, p, i, *_: (b, i, p))
    grid_spec = pltpu.PrefetchScalarGridSpec(
        num_scalar_prefetch=2,
        grid=(bn, pairs, nt),
        in_specs=[query_tile,
                  pl.BlockSpec((None, t, LANES), lambda b, p, i, *_: (b, i, 0)),
                  pl.BlockSpec((None, l, LANES), lambda b, p, i, *_: (b, 0, p)),
                  pl.BlockSpec((None, l, LANES), lambda b, p, i, *_: (b, 0, 0)),
                  pl.BlockSpec((None, None, nt, LANES, t), lambda b, p, i, *_: (b, p, 0, 0, 0))],
        out_specs=query_tile,
        scratch_shapes=[pltpu.VMEM((2, 1, t), F32), pltpu.VMEM((2, 1, t), F32), pltpu.VMEM((2, 1, t), F32),
                        pltpu.VMEM((2, LANES, t), F32), pltpu.VMEM((2, t, t), F32),
                        pltpu.VMEM((2, t, t), BF16)])
    return pl.pallas_call(
        functools.partial(_fox_kernel, t=t, nt=nt),
        grid_spec=grid_spec,
        out_shape=jax.ShapeDtypeStruct((bn, l, D_FOX), BF16),
        compiler_params=pltpu.CompilerParams(
            dimension_semantics=("arbitrary", "arbitrary", "arbitrary"), vmem_limit_bytes=VMEM_LIMIT),
        name="fox",
    )(fend, qkb, q, fqa, k, fka, vt)


def _fox_skip_tables(fend, nq, nk):
    per_head = lambda a: jnp.swapaxes(a[:, :, 0, :FOX_HEADS], 1, 2)
    qmax = jnp.sqrt(per_head(nq))
    kmax = jnp.sqrt(jnp.max(per_head(nk), axis=2, keepdims=True))
    return per_head(fend).reshape(-1), (NORM_MARGIN * qmax * kmax).reshape(-1)


def _s5_tables_kernel(lr_ref, li_ref, ls_ref, btr_ref, bti_ref, cr_ref, ci_ref, d_ref,
                      wtoe_ref, win_ref, voutt_ref, lam_ref):
    lr, li = lr_ref[...], li_ref[...]
    step = jnp.exp(ls_ref[...])
    mag = jnp.exp(lr * step)
    ab_re, ab_im = mag * jnp.cos(li * step), mag * jnp.sin(li * step)
    den = lr * lr + li * li
    nr, ni = ab_re - 1.0, ab_im
    f_re = (nr * lr + ni * li) / den
    f_im = (ni * lr - nr * li) / den
    btr, bti = btr_ref[...], bti_ref[...]
    bb_re = f_re * btr - f_im * bti
    bb_im = f_re * bti + f_im * btr
    cr, ci = cr_ref[...], ci_ref[...]
    eye = (lax.broadcasted_iota(jnp.int32, (LANES, LANES), 0)
           == lax.broadcasted_iota(jnp.int32, (LANES, LANES), 1))
    tile = lambda j: slice(j * LANES, (j + 1) * LANES)
    zeros = jnp.zeros((LANES, LANES), BF16)
    for jp in range(S5_CHUNK):
        for j in range(jp):
            wtoe_ref[tile(jp), tile(j)] = zeros
    pr = jnp.ones_like(lr)
    pi = jnp.zeros_like(lr)
    for tau in range(S5_CHUNK + 1):
        a_re = cr * pr - ci * pi
        a_im = cr * pi + ci * pr
        if tau < S5_CHUNK:
            kt = _dot_nt(bb_re, a_re, precision=HIGHEST) - _dot_nt(bb_im, a_im, precision=HIGHEST)
            if tau == 0:
                kt = kt + jnp.where(eye, d_ref[...], 0.0)
            kt = kt.astype(BF16)
            for jp in range(S5_CHUNK - tau):
                wtoe_ref[tile(jp), tile(jp + tau)] = kt
            j = S5_CHUNK - 1 - tau
            win_ref[tile(j), 0:S5_HALF] = (pr * bb_re - pi * bb_im).astype(BF16)
            win_ref[tile(j), S5_HALF:] = (pr * bb_im + pi * bb_re).astype(BF16)
        if tau >= 1:
            voutt_ref[tile(tau - 1), 0:S5_HALF] = a_re.astype(BF16)
            voutt_ref[tile(tau - 1), S5_HALF:] = (-a_im).astype(BF16)
        if tau == S5_CHUNK:
            lam_ref[:, 0:S5_HALF] = pr
            lam_ref[:, S5_HALF:] = pi
        pr, pi = pr * ab_re - pi * ab_im, pr * ab_im + pi * ab_re


def _s5_tables(lam_re, lam_im, log_step, b_re, b_im, c_re, c_im, d):
    lg, gp = S5_LANE_GROUPS, S5_GROUPS_PER_LANE_GROUP
    eye = jnp.eye(gp, dtype=F32)

    def block_diag(a):
        a = a.astype(F32).reshape(lg, gp, S5_GROUP, S5_STATE)
        return jnp.einsum("Ggcp,gh->Ggchp", a, eye).reshape(lg, LANES, S5_HALF)

    lanes = lambda a: a.astype(F32).reshape(lg, 1, S5_HALF)
    step = jnp.broadcast_to(log_step.astype(F32)[:, None], (S5_GROUPS, S5_STATE))
    spec = lambda r, c: pl.BlockSpec((None, r, c), lambda g: (g, 0, 0))
    return pl.pallas_call(
        _s5_tables_kernel,
        grid=(lg,),
        in_specs=[spec(1, S5_HALF)] * 3 + [spec(LANES, S5_HALF)] * 4 + [spec(1, LANES)],
        out_specs=[spec(S5_CW, S5_CW), spec(S5_CW, S5_SW), spec(S5_CW, S5_SW), spec(1, S5_SW)],
        out_shape=[jax.ShapeDtypeStruct((lg, S5_CW, S5_CW), BF16),
                   jax.ShapeDtypeStruct((lg, S5_CW, S5_SW), BF16),
                   jax.ShapeDtypeStruct((lg, S5_CW, S5_SW), BF16),
                   jax.ShapeDtypeStruct((lg, 1, S5_SW), F32)],
        compiler_params=pltpu.CompilerParams(
            dimension_semantics=("arbitrary",), vmem_limit_bytes=VMEM_LIMIT),
        name="s5_tables",
    )(lanes(lam_re), lanes(lam_im), lanes(step),
      block_diag(jnp.swapaxes(b_re, 1, 2)), block_diag(jnp.swapaxes(b_im, 1, 2)),
      block_diag(c_re), block_diag(c_im), d.astype(F32).reshape(lg, 1, LANES))


def _s5_kernel(uc_ref, wtoe_ref, win_ref, voutt_ref, lam_ref, yc_ref, e_scr, hs_scr, *, n):
    x = uc_ref[...]
    e_scr[...] = _dot(x, win_ref[...])
    lam = lam_ref[...]
    lr, li = lam[:, :S5_HALF], lam[:, S5_HALF:]

    def step(i, h):
        hr, hi = h
        hs_scr[pl.ds(i, 1), 0:S5_HALF] = hr
        hs_scr[pl.ds(i, 1), S5_HALF:] = hi
        er = e_scr[pl.ds(i, 1), 0:S5_HALF]
        ei = e_scr[pl.ds(i, 1), S5_HALF:]
        return lr * hr - li * hi + er, lr * hi + li * hr + ei

    zero = jnp.zeros((1, S5_HALF), F32)
    lax.fori_loop(0, n, step, (zero, zero))
    hs = hs_scr[...].astype(BF16)
    blk = 2 * LANES
    for jb in range(S5_CW // blk):
        cols = slice(jb * blk, (jb + 1) * blk)
        kk = (jb + 1) * blk
        y = _dot(x[:, :kk], wtoe_ref[0:kk, cols]) + _dot_nt(hs, voutt_ref[cols, :])
        yc_ref[:, cols] = y.astype(BF16)


def _s5(uc, wtoe, win, voutt, lam):
    bn, n, _ = uc.shape
    per_lg = lambda r, c: pl.BlockSpec((None, r, c), lambda g, b: (g, 0, 0))
    return pl.pallas_call(
        functools.partial(_s5_kernel, n=n),
        grid=(S5_LANE_GROUPS, bn),
        in_specs=[pl.BlockSpec((None, n, S5_CW), lambda g, b: (b, 0, g)),
                  per_lg(S5_CW, S5_CW), per_lg(S5_CW, S5_SW), per_lg(S5_CW, S5_SW), per_lg(1, S5_SW)],
        out_specs=pl.BlockSpec((None, n, S5_CW), lambda g, b: (b, 0, g)),
        out_shape=jax.ShapeDtypeStruct(uc.shape, BF16),
        scratch_shapes=[pltpu.VMEM((n, S5_SW), F32), pltpu.VMEM((n, S5_SW), F32)],
        compiler_params=pltpu.CompilerParams(
            dimension_semantics=("arbitrary", "arbitrary"), vmem_limit_bytes=VMEM_LIMIT),
        name="s5",
    )(uc, wtoe, win, voutt, lam)


def _final_kernel(x_ref, yfox_ref, yc_ref, mk_ref, mv_ref, gn_ref, gfin_ref, wg_ref, bm_ref,
                  wglu_ref, bglu_ref, wpf_ref, wps_ref, wpm_ref, wout_ref, o_ref, ys_scr, *, tm):
    x = x_ref[...]
    h = _rms(x, gn_ref[...]).astype(BF16)

    def proj(c0, width):
        return _dot(h, wg_ref[:, c0:c0 + width])

    def gate(i):
        return _sigmoid(proj(G_GL + i * D_MODEL, D_MODEL) + bm_ref[:, i * D_MODEL:(i + 1) * D_MODEL])

    a = (yfox_ref[...].astype(F32) * _silu(proj(G_GF, D_FOX))).astype(BF16)
    merged = gate(0) * _dot(a, wpf_ref[...])

    nc = tm // S5_CHUNK
    for lg in range(S5_LANE_GROUPS):
        for j in range(S5_CHUNK):
            c0 = lg * S5_CW + j * LANES
            ys_scr[lg, pl.ds(j, nc, stride=S5_CHUNK), :] = yc_ref[:, c0:c0 + LANES].astype(F32)
    y = _gelu_tanh(jnp.concatenate([ys_scr[lg] for lg in range(S5_LANE_GROUPS)], axis=-1))
    y = y * _sigmoid(_dot(y.astype(BF16), wglu_ref[...]) + bglu_ref[...])
    y = (y * _silu(proj(G_GS, D_S5))).astype(BF16)
    merged = merged + gate(1) * _dot(y, wps_ref[...])

    qm = (proj(G_QM, D_MEM) * (MEM_HEAD_DIM ** -0.5)).astype(BF16)
    heads = []
    for hd in range(MEM_HEADS):
        cols = slice(hd * MEM_HEAD_DIM, (hd + 1) * MEM_HEAD_DIM)
        s = _dot_nt(qm[:, cols], mk_ref[:, cols])
        p = jnp.exp(s - jnp.max(s, axis=-1, keepdims=True))
        inv = 1.0 / jnp.sum(p, axis=-1, keepdims=True)
        heads.append(_dot(p.astype(BF16), mv_ref[:, cols]) * inv)
    ymem = jnp.concatenate(heads, axis=-1)
    a = (ymem * _silu(proj(G_GM, D_MEM))).astype(BF16)
    merged = merged + gate(2) * _dot(a, wpm_ref[...])

    out = x + _dot(merged.astype(BF16), wout_ref[...])
    o_ref[...] = _rms(out, gfin_ref[...])


def _final(x, yfox, yc, mk, mv, gn, gfin, wg, bm, wglu, bglu, wpf, wps, wpm, wout, *, tm):
    bn, l, _ = x.shape
    m = mk.shape[1]
    nc = tm // S5_CHUNK
    tok = lambda width: pl.BlockSpec((None, tm, width), lambda b, i: (b, i, 0))
    mem = pl.BlockSpec((None, m, D_MEM), lambda b, i: (b, 0, 0))
    return pl.pallas_call(
        functools.partial(_final_kernel, tm=tm),
        grid=(bn, l // tm),
        in_specs=[tok(D_MODEL), tok(D_FOX),
                  pl.BlockSpec((None, nc, S5_LANE_GROUPS * S5_CW), lambda b, i: (b, i, 0)),
                  mem, mem,
                  _const_spec((1, D_MODEL)), _const_spec((1, D_MODEL)),
                  _const_spec((D_MODEL, G_COLS)), _const_spec((1, 3 * D_MODEL)),
                  _const_spec((D_S5, D_S5)), _const_spec((1, D_S5)),
                  _const_spec((D_FOX, D_MODEL)), _const_spec((D_S5, D_MODEL)),
                  _const_spec((D_MEM, D_MODEL)), _const_spec((D_MODEL, D_MODEL))],
        out_specs=tok(D_MODEL),
        out_shape=jax.ShapeDtypeStruct(x.shape, x.dtype),
        scratch_shapes=[pltpu.VMEM((S5_LANE_GROUPS, tm, LANES), F32)],
        compiler_params=pltpu.CompilerParams(
            dimension_semantics=("arbitrary", "arbitrary"), vmem_limit_bytes=VMEM_LIMIT),
        name="final",
    )(x, yfox, yc, mk, mv, gn, gfin, wg, bm, wglu, bglu, wpf, wps, wpm, wout)


def _split_w_in(w):
    sizes = (D_FOX, D_FOX, D_FOX, FOX_HEADS, D_FOX, D_S5, D_S5, D_MEM, D_MEM, 3 * D_MODEL)
    out, c0 = [], 0
    for s in sizes:
        out.append(w[:, c0:c0 + s])
        c0 += s
    return out


def _layer(x, mem, g_norm, g_mem_norm, w_in, b_forget, b_merge, w_mem_kv, lam_re, lam_im, log_step,
           s5_b_re, s5_b_im, s5_c_re, s5_c_im, s5_d, w_glu, b_glu, w_proj_fox, w_proj_s5, w_proj_mem,
           w_out, g_out, *, tm_in, t_attn, tm_out):
    bn, l, _ = x.shape
    assert tm_in == t_attn, "the per-tile q norms from in_proj are indexed by attention query tile"
    row = lambda a: a.reshape(1, -1).astype(F32)
    wq, wk, wv, wfl, wgf, wu, wgs, wqm, wgm, wgl = _split_w_in(w_in)
    w_tok = jnp.concatenate([wq, wk, wu], axis=1).astype(BF16)
    wvt = wv.T.astype(BF16)
    wfl_pad = jnp.pad(wfl, ((0, 0), (0, LANES - FOX_HEADS))).astype(BF16)
    bfl_pad = jnp.pad(row(b_forget), ((0, 0), (0, LANES - FOX_HEADS)))
    w_gate = jnp.concatenate([wgf, wgs, wgm, wqm, wgl], axis=1).astype(BF16)

    mk, mv = _mem_kv(mem, row(g_mem_norm), w_mem_kv.astype(BF16))
    q, k, vt, uc, fqa, fka, fend, nq, nk = _in_proj(x, row(g_norm), w_tok, wvt, wfl_pad, bfl_pad, tm=tm_in)
    yfox = _fox(q, fqa, k, fka, vt, *_fox_skip_tables(fend, nq, nk), t=t_attn)
    tables = _s5_tables(lam_re, lam_im, log_step, s5_b_re, s5_b_im, s5_c_re, s5_c_im, s5_d)
    yc = _s5(uc, *tables)
    return _final(x, yfox, yc, mk, mv, row(g_norm), row(g_out), w_gate, row(b_merge),
                  w_glu.astype(BF16), row(b_glu), w_proj_fox.astype(BF16), w_proj_s5.astype(BF16),
                  w_proj_mem.astype(BF16), w_out.astype(BF16), tm=tm_out)


def kernel(x, mem, g_norm, g_mem_norm, g_final, w_in, b_forget, b_merge, w_mem_kv, lam_re, lam_im, log_step,
           s5_b_re, s5_b_im, s5_c_re, s5_c_im, s5_d, w_glu, b_glu, w_proj_fox, w_proj_s5, w_proj_mem, w_out):
    depth = w_in.shape[0]
    assert depth == 1, "the fused final kernel applies the closing RMSNorm: one layer only"
    l = x.shape[1]
    tm_in = min(512, l)
    t_attn = min(512, l)
    tm_out = min(512, l)
    return _layer(x, mem, g_norm[0], g_mem_norm[0], w_in[0], b_forget[0], b_merge[0], w_mem_kv[0],
                  lam_re[0], lam_im[0], log_step[0], s5_b_re[0], s5_b_im[0], s5_c_re[0], s5_c_im[0],
                  s5_d[0], w_glu[0], b_glu[0], w_proj_fox[0], w_proj_s5[0], w_proj_mem[0], w_out[0],
                  g_final, tm_in=tm_in, t_attn=t_attn, tm_out=tm_out)
```

```python
import functools
import math

import jax
import jax.numpy as jnp
import numpy as np
from jax import lax
from jax.experimental import pallas as pl
from jax.experimental.pallas import tpu as pltpu

F32 = jnp.float32
BF16 = jnp.bfloat16
HIGHEST = lax.Precision.HIGHEST

D_MODEL = 1024
EPS = 1e-6
NEG = -1e30
LOG2E = math.log2(math.e)
SKIP_LOG2 = 127.0
NORM_MARGIN = 1.03

FOX_HEAD_DIM = 64
D_FOX = 768
FOX_HEADS = D_FOX // FOX_HEAD_DIM

D_S5 = 768
S5_GROUP = 16
S5_GROUPS = D_S5 // S5_GROUP
S5_STATE = 64

D_MEM = 512
MEM_HEADS = 4
MEM_HEAD_DIM = D_MEM // MEM_HEADS

LANES = 128
S5_CHUNK = 16
S5_LANE_GROUPS = D_S5 // LANES
S5_GROUPS_PER_LANE_GROUP = LANES // S5_GROUP
S5_CW = S5_CHUNK * LANES
S5_HALF = S5_GROUPS_PER_LANE_GROUP * S5_STATE
S5_SW = 2 * S5_HALF
S5_SEGMENTS = 8

G_GF, G_GS, G_GM, G_QM, G_GL = 0, 768, 1536, 2048, 2560
G_COLS = G_GL + 3 * D_MODEL

VMEM_LIMIT = 56 * 1024 * 1024


def _dot(a, b, **kw):
    return jnp.dot(a, b, preferred_element_type=F32, **kw)


def _dot_nt(a, b, **kw):
    return lax.dot_general(a, b, (((1,), (1,)), ((), ())), preferred_element_type=F32, **kw)


def _rms(xf, g):
    return xf * lax.rsqrt(jnp.mean(xf * xf, axis=-1, keepdims=True) + EPS) * g


def _sigmoid(z):
    return 1.0 / (1.0 + jnp.exp(-z))


def _silu(z):
    return z * _sigmoid(z)


def _gelu_tanh(y):
    return 0.5 * y * (1.0 + jnp.tanh(math.sqrt(2.0 / math.pi) * (y + 0.044715 * (y * y * y))))


def _split3(a):
    hi = a.astype(BF16)
    rest = a - hi.astype(F32)
    mid = rest.astype(BF16)
    lo = (rest - mid.astype(F32)).astype(BF16)
    return jnp.concatenate([hi, mid, lo], axis=-1)


def _const_spec(shape):
    nd = len(shape)
    return pl.BlockSpec(shape, lambda *_: (0,) * nd, pipeline_mode=pl.Buffered(1))


def _mem_kv_kernel(mem_ref, g_ref, w_ref, mk_ref, mv_ref):
    h = _rms(mem_ref[...], g_ref[...]).astype(BF16)
    kv = _dot(h, w_ref[...])
    mk_ref[...] = kv[:, :D_MEM].astype(BF16)
    mv_ref[...] = kv[:, D_MEM:].astype(BF16)


def _mem_kv(mem, g, w):
    bn, m, _ = mem.shape
    out = jax.ShapeDtypeStruct((bn, m, D_MEM), BF16)
    return pl.pallas_call(
        _mem_kv_kernel,
        grid=(bn,),
        in_specs=[pl.BlockSpec((None, m, D_MODEL), lambda b: (b, 0, 0)),
                  _const_spec((1, D_MODEL)),
                  _const_spec((D_MODEL, 2 * D_MEM))],
        out_specs=[pl.BlockSpec((None, m, D_MEM), lambda b: (b, 0, 0))] * 2,
        out_shape=[out, out],
        name="mem_kv",
    )(mem, g, w)


def _in_proj_kernel(x_ref, g_ref, w_ref, wvt_ref, wfl_ref, bfl_ref, spread_ref, ones_ref,
                    q_ref, k_ref, vt_ref, uc_ref, fqa_ref, fka_ref, fend_ref, nq_ref, nk_ref,
                    us_scr, carry_scr, *, tm):
    @pl.when(pl.program_id(1) == 0)
    def _():
        carry_scr[...] = jnp.zeros_like(carry_scr)

    h = _rms(x_ref[...], g_ref[...]).astype(BF16)
    qb = (_dot(h, w_ref[:, 0:D_FOX]) * (LOG2E * FOX_HEAD_DIM ** -0.5)).astype(BF16)
    kb = _dot(h, w_ref[:, D_FOX:2 * D_FOX]).astype(BF16)
    q_ref[...] = qb
    k_ref[...] = kb
    for pair in range(FOX_HEADS // 2):
        vt_ref[pair] = _dot_nt(wvt_ref[pair * LANES:(pair + 1) * LANES, :], h).astype(BF16)

    head_of_col = lax.broadcasted_iota(jnp.int32, (D_FOX, LANES), 0) // FOX_HEAD_DIM
    sel = (head_of_col == lax.broadcasted_iota(jnp.int32, (D_FOX, LANES), 1)).astype(BF16)
    sq = lambda a: (a.astype(F32) * a.astype(F32)).astype(BF16)
    nq_ref[...] = jnp.max(_dot(sq(qb), sel), axis=0, keepdims=True)
    nk_ref[...] = jnp.max(_dot(sq(kb), sel), axis=0, keepdims=True)

    u = _dot(h, w_ref[:, 2 * D_FOX:2 * D_FOX + D_S5])
    nc = tm // S5_CHUNK
    for lg in range(S5_LANE_GROUPS):
        us_scr[lg] = u[:, lg * LANES:(lg + 1) * LANES]
        for j in range(S5_CHUNK):
            c0 = lg * S5_CW + j * LANES
            uc_ref[:, c0:c0 + LANES] = us_scr[lg, pl.ds(j, nc, stride=S5_CHUNK), :].astype(BF16)

    z = _dot(h, wfl_ref[...]) + bfl_ref[...]
    logf = (jnp.minimum(z, 0.0) - jnp.log1p(jnp.exp(-jnp.abs(z)))) * LOG2E
    row = lax.broadcasted_iota(jnp.int32, (tm, tm), 0)
    col = lax.broadcasted_iota(jnp.int32, (tm, tm), 1)
    tri = (col <= row).astype(BF16)
    parts = _dot(tri, _split3(logf))
    f = (parts[:, 0:LANES] + parts[:, LANES:2 * LANES]) + parts[:, 2 * LANES:] + carry_scr[...]
    carry_scr[...] = f[tm - 1:tm, :]
    fend_ref[...] = f[tm - 1:tm, :]
    aug = _dot(_split3(f), spread_ref[...]) + ones_ref[...]
    fka_ref[...] = aug[:, 0:LANES].astype(BF16)
    fqa_ref[...] = aug[:, LANES:].astype(BF16)


def _fox_aug_constants():
    spread = np.zeros((3 * LANES, 2 * LANES), np.float32)
    ones = np.zeros((1, 2 * LANES), np.float32)
    half = LANES // 2
    for h in range(FOX_HEADS):
        for i in range(3):
            spread[i * LANES + h, 3 * h + i] = 1.0
            spread[i * LANES + h, LANES + half + 3 * h + i] = 1.0
            ones[0, half + 3 * h + i] = 1.0
            ones[0, LANES + 3 * h + i] = -1.0
    return jnp.asarray(spread, BF16), jnp.asarray(ones, F32)


def _in_proj(x, g, w, wvt, wfl, bfl, *, tm):
    bn, l, _ = x.shape
    nc = tm // S5_CHUNK
    nt = l // tm
    pairs = FOX_HEADS // 2
    spread, ones = _fox_aug_constants()
    tok = lambda width: pl.BlockSpec((None, tm, width), lambda b, i: (b, i, 0))
    per_tile = pl.BlockSpec((None, None, 1, LANES), lambda b, i: (b, i, 0, 0))
    qk = jax.ShapeDtypeStruct((bn, l, D_FOX), BF16)
    aug = jax.ShapeDtypeStruct((bn, l, LANES), BF16)
    stat = jax.ShapeDtypeStruct((bn, nt, 1, LANES), F32)
    return pl.pallas_call(
        functools.partial(_in_proj_kernel, tm=tm),
        grid=(bn, nt),
        in_specs=[tok(D_MODEL),
                  _const_spec((1, D_MODEL)),
                  _const_spec((D_MODEL, 2 * D_FOX + D_S5)),
                  _const_spec((D_FOX, D_MODEL)),
                  _const_spec((D_MODEL, LANES)),
                  _const_spec((1, LANES)),
                  _const_spec((3 * LANES, 2 * LANES)),
                  _const_spec((1, 2 * LANES))],
        out_specs=[tok(D_FOX), tok(D_FOX),
                   pl.BlockSpec((None, pairs, None, LANES, tm), lambda b, i: (b, 0, i, 0, 0)),
                   pl.BlockSpec((None, nc, S5_LANE_GROUPS * S5_CW), lambda b, i: (b, i, 0)),
                   tok(LANES), tok(LANES), per_tile, per_tile, per_tile],
        out_shape=[qk, qk,
                   jax.ShapeDtypeStruct((bn, pairs, nt, LANES, tm), BF16),
                   jax.ShapeDtypeStruct((bn, l // S5_CHUNK, S5_LANE_GROUPS * S5_CW), BF16),
                   aug, aug, stat, stat, stat],
        scratch_shapes=[pltpu.VMEM((S5_LANE_GROUPS, tm, LANES), F32), pltpu.VMEM((1, LANES), F32)],
        compiler_params=pltpu.CompilerParams(
            dimension_semantics=("arbitrary", "arbitrary"), vmem_limit_bytes=VMEM_LIMIT),
        name="in_proj",
    )(x, g, w, wvt, wfl, bfl, spread, ones)


def _fox_kernel(fend_ref, qkb_ref, q_ref, fqa_ref, k_ref, fka_ref, vt_ref, o_ref,
                m_scr, l_scr, alpha_scr, acc_scr, s_scr, p_scr, *, t, nt):
    b = pl.program_id(0)
    pair = pl.program_id(1)
    qi = pl.program_id(2)
    lane = lax.broadcasted_iota(jnp.int32, (1, LANES), 1)
    q = q_ref[...]
    fqa = fqa_ref[...]
    key_pos = lax.broadcasted_iota(jnp.int32, (t, t), 0)
    query_pos = lax.broadcasted_iota(jnp.int32, (t, t), 1)

    heads = (0, 1)
    q_aug, base = [], []
    for e in heads:
        head = 2 * pair + e
        head_lanes = (lane >= e * FOX_HEAD_DIM) & (lane < (e + 1) * FOX_HEAD_DIM)
        f_lanes = lax.rem(lane, LANES // 2) // 3 == head
        q_aug.append(jnp.concatenate([jnp.where(head_lanes, q, jnp.zeros_like(q)),
                                      jnp.where(f_lanes, fqa, jnp.zeros_like(fqa))], axis=-1))
        base.append((b * FOX_HEADS + head) * nt)

    def scores(e, j):
        k0 = pl.multiple_of(jnp.maximum(j, 0) * t, t)
        k_aug = jnp.concatenate([k_ref[pl.ds(k0, t), :], fka_ref[pl.ds(k0, t), :]], axis=-1)
        return _dot_nt(k_aug, q_aug[e])

    def weighted_values(e, j):
        acc_scr[e] = alpha_scr[e] * acc_scr[e] + _dot(vt_ref[j], p_scr[e])

    slack = []
    for e in heads:
        s = jnp.where(key_pos <= query_pos, scores(e, qi), NEG)
        m = jnp.max(s, axis=0, keepdims=True)
        p = jnp.exp2(s - m)
        m_scr[e] = m
        l_scr[e] = jnp.sum(p, axis=0, keepdims=True)
        p_scr[e] = p.astype(BF16)
        alpha_scr[e] = jnp.zeros_like(m)
        acc_scr[e] = jnp.zeros((LANES, t), F32)
        slack.append(qkb_ref[base[e] + qi] - jnp.min(m) + SKIP_LOG2)
        s_scr[e] = scores(e, qi - 1)

    def wanted(e, j):
        f_hi = fend_ref[base[e] + jnp.maximum(qi - 1, 0)]
        reach = f_hi - fend_ref[base[e] + jnp.maximum(j, 0)] + slack[e]
        return jnp.logical_and(j >= 0, reach >= 0.0).astype(jnp.int32)

    def trip(j, which):
        for e in which:
            weighted_values(e, j + 1)
        for e in which:
            s = s_scr[e]
            m_old = m_scr[e]
            m_new = jnp.maximum(m_old, jnp.max(s, axis=0, keepdims=True))
            p = jnp.exp2(s - m_new)
            alpha = jnp.exp2(m_old - m_new)
            m_scr[e] = m_new
            l_scr[e] = alpha * l_scr[e] + jnp.sum(p, axis=0, keepdims=True)
            p_scr[e] = p.astype(BF16)
            alpha_scr[e] = alpha
        for e in which:
            s_scr[e] = scores(e, j - 1)

    def walk(j, which):
        def go(jj):
            flag = wanted(which[0], jj)
            for e in which[1:]:
                flag = flag * wanted(e, jj)
            return flag

        def body(carry):
            trip(carry[0], which)
            return carry[0] - 1, go(carry[0] - 1)

        return lax.while_loop(lambda c: c[1] > 0, body, (j, go(j)))[0]

    j_both = walk(qi - 1, heads)
    outs = []
    for e in heads:
        j_stop = walk(j_both, (e,))
        weighted_values(e, j_stop + 1)
        outs.append(acc_scr[e] * (1.0 / l_scr[e]))
    dim = lax.broadcasted_iota(jnp.int32, (LANES, 1), 0)
    o_ref[...] = jnp.where(dim < FOX_HEAD_DIM, outs[0], outs[1]).T.astype(BF16)


def _fox(q, fqa, k, fka, vt, fend, qkb, *, t):
    bn, l, _ = q.shape
    pairs = FOX_HEADS // 2
    nt = l // t
    query_tile = pl.BlockSpec((None, t, LANES), lambda b, p, i, *_: (b, i, p))
    grid_spec = pltpu.PrefetchScalarGridSpec(
        num_scalar_prefetch=2,
        grid=(bn, pairs, nt),
        in_specs=[query_tile,
                  pl.BlockSpec((None, t, LANES), lambda b, p, i, *_: (b, i, 0)),
                  pl.BlockSpec((None, l, LANES), lambda b, p, i, *_: (b, 0, p)),
                  pl.BlockSpec((None, l, LANES), lambda b, p, i, *_: (b, 0, 0)),
                  pl.BlockSpec((None, None, nt, LANES, t), lambda b, p, i, *_: (b, p, 0, 0, 0))],
        out_specs=query_tile,
        scratch_shapes=[pltpu.VMEM((2, 1, t), F32), pltpu.VMEM((2, 1, t), F32), pltpu.VMEM((2, 1, t), F32),
                        pltpu.VMEM((2, LANES, t), F32), pltpu.VMEM((2, t, t), F32),
                        pltpu.VMEM((2, t, t), BF16)])
    return pl.pallas_call(
        functools.partial(_fox_kernel, t=t, nt=nt),
        grid_spec=grid_spec,
        out_shape=jax.ShapeDtypeStruct((bn, l, D_FOX), BF16),
        compiler_params=pltpu.CompilerParams(
            dimension_semantics=("arbitrary", "arbitrary", "arbitrary"), vmem_limit_bytes=VMEM_LIMIT),
        name="fox",
    )(fend, qkb, q, fqa, k, fka, vt)


def _fox_skip_tables(fend, nq, nk):
    per_head = lambda a: jnp.swapaxes(a[:, :, 0, :FOX_HEADS], 1, 2)
    qmax = jnp.sqrt(per_head(nq))
    kmax = jnp.sqrt(jnp.max(per_head(nk), axis=2, keepdims=True))
    return per_head(fend).reshape(-1), (NORM_MARGIN * qmax * kmax).reshape(-1)


def _s5_tables_kernel(lr_ref, li_ref, ls_ref, btr_ref, bti_ref, cr_ref, ci_ref, d_ref,
                      wtoe_ref, win_ref, voutt_ref, lam_ref):
    lr, li = lr_ref[...], li_ref[...]
    step = jnp.exp(ls_ref[...])
    mag = jnp.exp(lr * step)
    ab_re, ab_im = mag * jnp.cos(li * step), mag * jnp.sin(li * step)
    den = lr * lr + li * li
    nr, ni = ab_re - 1.0, ab_im
    f_re = (nr * lr + ni * li) / den
    f_im = (ni * lr - nr * li) / den
    btr, bti = btr_ref[...], bti_ref[...]
    bb_re = f_re * btr - f_im * bti
    bb_im = f_re * bti + f_im * btr
    cr, ci = cr_ref[...], ci_ref[...]
    eye = (lax.broadcasted_iota(jnp.int32, (LANES, LANES), 0)
           == lax.broadcasted_iota(jnp.int32, (LANES, LANES), 1))
    tile = lambda j: slice(j * LANES, (j + 1) * LANES)
    zeros = jnp.zeros((LANES, LANES), BF16)
    for jp in range(S5_CHUNK):
        for j in range(jp):
            wtoe_ref[tile(jp), tile(j)] = zeros
    pr = jnp.ones_like(lr)
    pi = jnp.zeros_like(lr)
    for tau in range(S5_CHUNK + 1):
        a_re = cr * pr - ci * pi
        a_im = cr * pi + ci * pr
        if tau < S5_CHUNK:
            kt = _dot_nt(bb_re, a_re, precision=HIGHEST) - _dot_nt(bb_im, a_im, precision=HIGHEST)
            if tau == 0:
                kt = kt + jnp.where(eye, d_ref[...], 0.0)
            kt = kt.astype(BF16)
            for jp in range(S5_CHUNK - tau):
                wtoe_ref[tile(jp), tile(jp + tau)] = kt
            j = S5_CHUNK - 1 - tau
            win_ref[tile(j), 0:S5_HALF] = (pr * bb_re - pi * bb_im).astype(BF16)
            win_ref[tile(j), S5_HALF:] = (pr * bb_im + pi * bb_re).astype(BF16)
        if tau >= 1:
            voutt_ref[tile(tau - 1), 0:S5_HALF] = a_re.astype(BF16)
            voutt_ref[tile(tau - 1), S5_HALF:] = (-a_im).astype(BF16)
        if tau == S5_CHUNK:
            lam_ref[:, 0:S5_HALF] = pr
            lam_ref[:, S5_HALF:] = pi
        pr, pi = pr * ab_re - pi * ab_im, pr * ab_im + pi * ab_re


def _s5_tables(lam_re, lam_im, log_step, b_re, b_im, c_re, c_im, d):
    lg, gp = S5_LANE_GROUPS, S5_GROUPS_PER_LANE_GROUP
    eye = jnp.eye(gp, dtype=F32)

    def block_diag(a):
        a = a.astype(F32).reshape(lg, gp, S5_GROUP, S5_STATE)
        return jnp.einsum("Ggcp,gh->Ggchp", a, eye).reshape(lg, LANES, S5_HALF)

    lanes = lambda a: a.astype(F32).reshape(lg, 1, S5_HALF)
    step = jnp.broadcast_to(log_step.astype(F32)[:, None], (S5_GROUPS, S5_STATE))
    spec = lambda r, c: pl.BlockSpec((None, r, c), lambda g: (g, 0, 0))
    return pl.pallas_call(
        _s5_tables_kernel,
        grid=(lg,),
        in_specs=[spec(1, S5_HALF)] * 3 + [spec(LANES, S5_HALF)] * 4 + [spec(1, LANES)],
        out_specs=[spec(S5_CW, S5_CW), spec(S5_CW, S5_SW), spec(S5_CW, S5_SW), spec(1, S5_SW)],
        out_shape=[jax.ShapeDtypeStruct((lg, S5_CW, S5_CW), BF16),
                   jax.ShapeDtypeStruct((lg, S5_CW, S5_SW), BF16),
                   jax.ShapeDtypeStruct((lg, S5_CW, S5_SW), BF16),
                   jax.ShapeDtypeStruct((lg, 1, S5_SW), F32)],
        compiler_params=pltpu.CompilerParams(
            dimension_semantics=("arbitrary",), vmem_limit_bytes=VMEM_LIMIT),
        name="s5_tables",
    )(lanes(lam_re), lanes(lam_im), lanes(step),
      block_diag(jnp.swapaxes(b_re, 1, 2)), block_diag(jnp.swapaxes(b_im, 1, 2)),
      block_diag(c_re), block_diag(c_im), d.astype(F32).reshape(lg, 1, LANES))


def _s5_kernel(uc_ref, wtoe_ref, win_ref, voutt_ref, lam_ref, yc_ref, e_scr, hs_scr, *, n):
    x = uc_ref[...]
    e = _dot(x, win_ref[...])
    tiles = S5_SW // LANES
    half = tiles // 2
    lane_tile = lambda c: slice(c * LANES, (c + 1) * LANES)
    for c in range(tiles):
        e_scr[c, 0:n, :] = e[:, lane_tile(c)]
        e_scr[c, n:, :] = jnp.zeros((S5_SEGMENTS, LANES), F32)
    lam = jnp.broadcast_to(lam_ref[...], (S5_SEGMENTS, S5_SW))
    lam_r = [lam[:, lane_tile(c)] for c in range(half)]
    lam_i = [lam[:, lane_tile(half + c)] for c in range(half)]
    cmul = lambda ar, ai, br, bi: (ar * br - ai * bi, ar * bi + ai * br)

    seg = n // S5_SEGMENTS + 1
    run_rows = lambda r: pl.ds(r, S5_SEGMENTS, stride=seg)
    zero = jnp.zeros((S5_SEGMENTS, LANES), F32)
    one = jnp.ones((S5_SEGMENTS, LANES), F32)
    zero_row = jnp.zeros((1, LANES), F32)

    def scan_runs(r, carry):
        h, pw = carry
        h_next, pw_next = [None] * tiles, [None] * tiles
        for c in range(half):
            hs_scr[c, run_rows(r), :] = h[c]
            hs_scr[half + c, run_rows(r), :] = h[half + c]
            nr, ni = cmul(lam_r[c], lam_i[c], h[c], h[half + c])
            h_next[c] = nr + e_scr[c, run_rows(r), :]
            h_next[half + c] = ni + e_scr[half + c, run_rows(r), :]
            pw_next[c], pw_next[half + c] = cmul(lam_r[c], lam_i[c], pw[c], pw[half + c])
        return tuple(h_next), tuple(pw_next)

    unit = (one,) * half + (zero,) * half
    run_end, run_mult = lax.fori_loop(0, seg, scan_runs, ((zero,) * tiles, unit))

    start = [None] * tiles
    for c in range(half):
        sr, si, rows_r, rows_i = zero_row, zero_row, [], []
        for s in range(S5_SEGMENTS):
            rows_r.append(sr)
            rows_i.append(si)
            nr, ni = cmul(run_mult[c][0:1], run_mult[half + c][0:1], sr, si)
            sr, si = nr + run_end[c][s:s + 1], ni + run_end[half + c][s:s + 1]
        start[c], start[half + c] = jnp.concatenate(rows_r, axis=0), jnp.concatenate(rows_i, axis=0)

    def add_run_starts(r, pw):
        pw_next = [None] * tiles
        for c in range(half):
            ar, ai = cmul(pw[c], pw[half + c], start[c], start[half + c])
            hs_scr[c, run_rows(r), :] = hs_scr[c, run_rows(r), :] + ar
            hs_scr[half + c, run_rows(r), :] = hs_scr[half + c, run_rows(r), :] + ai
            pw_next[c], pw_next[half + c] = cmul(lam_r[c], lam_i[c], pw[c], pw[half + c])
        return tuple(pw_next)

    lax.fori_loop(0, seg, add_run_starts, unit)
    hs = jnp.concatenate([hs_scr[c, 0:n, :] for c in range(tiles)], axis=-1).astype(BF16)
    blk = 2 * LANES
    for jb in range(S5_CW // blk):
        cols = slice(jb * blk, (jb + 1) * blk)
        kk = (jb + 1) * blk
        y = _dot(x[:, :kk], wtoe_ref[0:kk, cols]) + _dot_nt(hs, voutt_ref[cols, :])
        yc_ref[:, cols] = y.astype(BF16)


def _s5(uc, wtoe, win, voutt, lam):
    bn, n, _ = uc.shape
    per_lg = lambda r, c: pl.BlockSpec((None, r, c), lambda g, b: (g, 0, 0))
    return pl.pallas_call(
        functools.partial(_s5_kernel, n=n),
        grid=(S5_LANE_GROUPS, bn),
        in_specs=[pl.BlockSpec((None, n, S5_CW), lambda g, b: (b, 0, g)),
                  per_lg(S5_CW, S5_CW), per_lg(S5_CW, S5_SW), per_lg(S5_CW, S5_SW), per_lg(1, S5_SW)],
        out_specs=pl.BlockSpec((None, n, S5_CW), lambda g, b: (b, 0, g)),
        out_shape=jax.ShapeDtypeStruct(uc.shape, BF16),
        scratch_shapes=[pltpu.VMEM((S5_SW // LANES, n + S5_SEGMENTS, LANES), F32)] * 2,
        compiler_params=pltpu.CompilerParams(
            dimension_semantics=("arbitrary", "arbitrary"), vmem_limit_bytes=VMEM_LIMIT),
        name="s5",
    )(uc, wtoe, win, voutt, lam)


def _final_kernel(x_ref, yfox_ref, yc_ref, mk_ref, mv_ref, gn_ref, gfin_ref, wg_ref, bm_ref,
                  wglu_ref, bglu_ref, wpf_ref, wps_ref, wpm_ref, wout_ref, o_ref, ys_scr, *, tm):
    x = x_ref[...]
    h = _rms(x, gn_ref[...]).astype(BF16)

    def proj(c0, width):
        return _dot(h, wg_ref[:, c0:c0 + width])

    def gate(i):
        return _sigmoid(proj(G_GL + i * D_MODEL, D_MODEL) + bm_ref[:, i * D_MODEL:(i + 1) * D_MODEL])

    a = (yfox_ref[...].astype(F32) * _silu(proj(G_GF, D_FOX))).astype(BF16)
    merged = gate(0) * _dot(a, wpf_ref[...])

    nc = tm // S5_CHUNK
    for lg in range(S5_LANE_GROUPS):
        for j in range(S5_CHUNK):
            c0 = lg * S5_CW + j * LANES
            ys_scr[lg, pl.ds(j, nc, stride=S5_CHUNK), :] = yc_ref[:, c0:c0 + LANES].astype(F32)
    y = _gelu_tanh(jnp.concatenate([ys_scr[lg] for lg in range(S5_LANE_GROUPS)], axis=-1))
    y = y * _sigmoid(_dot(y.astype(BF16), wglu_ref[...]) + bglu_ref[...])
    y = (y * _silu(proj(G_GS, D_S5))).astype(BF16)
    merged = merged + gate(1) * _dot(y, wps_ref[...])

    qm = (proj(G_QM, D_MEM) * (MEM_HEAD_DIM ** -0.5)).astype(BF16)
    heads = []
    for hd in range(MEM_HEADS):
        cols = slice(hd * MEM_HEAD_DIM, (hd + 1) * MEM_HEAD_DIM)
        s = _dot_nt(qm[:, cols], mk_ref[:, cols])
        p = jnp.exp(s - jnp.max(s, axis=-1, keepdims=True))
        inv = 1.0 / jnp.sum(p, axis=-1, keepdims=True)
        heads.append(_dot(p.astype(BF16), mv_ref[:, cols]) * inv)
    ymem = jnp.concatenate(heads, axis=-1)
    a = (ymem * _silu(proj(G_GM, D_MEM))).astype(BF16)
    merged = merged + gate(2) * _dot(a, wpm_ref[...])

    out = x + _dot(merged.astype(BF16), wout_ref[...])
    o_ref[...] = _rms(out, gfin_ref[...])


def _final(x, yfox, yc, mk, mv, gn, gfin, wg, bm, wglu, bglu, wpf, wps, wpm, wout, *, tm):
    bn, l, _ = x.shape
    m = mk.shape[1]
    nc = tm // S5_CHUNK
    tok = lambda width: pl.BlockSpec((None, tm, width), lambda b, i: (b, i, 0))
    mem = pl.BlockSpec((None, m, D_MEM), lambda b, i: (b, 0, 0))
    return pl.pallas_call(
        functools.partial(_final_kernel, tm=tm),
        grid=(bn, l // tm),
        in_specs=[tok(D_MODEL), tok(D_FOX),
                  pl.BlockSpec((None, nc, S5_LANE_GROUPS * S5_CW), lambda b, i: (b, i, 0)),
                  mem, mem,
                  _const_spec((1, D_MODEL)), _const_spec((1, D_MODEL)),
                  _const_spec((D_MODEL, G_COLS)), _const_spec((1, 3 * D_MODEL)),
                  _const_spec((D_S5, D_S5)), _const_spec((1, D_S5)),
                  _const_spec((D_FOX, D_MODEL)), _const_spec((D_S5, D_MODEL)),
                  _const_spec((D_MEM, D_MODEL)), _const_spec((D_MODEL, D_MODEL))],
        out_specs=tok(D_MODEL),
        out_shape=jax.ShapeDtypeStruct(x.shape, x.dtype),
        scratch_shapes=[pltpu.VMEM((S5_LANE_GROUPS, tm, LANES), F32)],
        compiler_params=pltpu.CompilerParams(
            dimension_semantics=("arbitrary", "arbitrary"), vmem_limit_bytes=VMEM_LIMIT),
        name="final",
    )(x, yfox, yc, mk, mv, gn, gfin, wg, bm, wglu, bglu, wpf, wps, wpm, wout)


def _split_w_in(w):
    sizes = (D_FOX, D_FOX, D_FOX, FOX_HEADS, D_FOX, D_S5, D_S5, D_MEM, D_MEM, 3 * D_MODEL)
    out, c0 = [], 0
    for s in sizes:
        out.append(w[:, c0:c0 + s])
        c0 += s
    return out


def _layer(x, mem, g_norm, g_mem_norm, w_in, b_forget, b_merge, w_mem_kv, lam_re, lam_im, log_step,
           s5_b_re, s5_b_im, s5_c_re, s5_c_im, s5_d, w_glu, b_glu, w_proj_fox, w_proj_s5, w_proj_mem,
           w_out, g_out, *, tm_in, t_attn, tm_out):
    bn, l, _ = x.shape
    assert tm_in == t_attn, "the per-tile q norms from in_proj are indexed by attention query tile"
    row = lambda a: a.reshape(1, -1).astype(F32)
    wq, wk, wv, wfl, wgf, wu, wgs, wqm, wgm, wgl = _split_w_in(w_in)
    w_tok = jnp.concatenate([wq, wk, wu], axis=1).astype(BF16)
    wvt = wv.T.astype(BF16)
    wfl_pad = jnp.pad(wfl, ((0, 0), (0, LANES - FOX_HEADS))).astype(BF16)
    bfl_pad = jnp.pad(row(b_forget), ((0, 0), (0, LANES - FOX_HEADS)))
    w_gate = jnp.concatenate([wgf, wgs, wgm, wqm, wgl], axis=1).astype(BF16)

    mk, mv = _mem_kv(mem, row(g_mem_norm), w_mem_kv.astype(BF16))
    q, k, vt, uc, fqa, fka, fend, nq, nk = _in_proj(x, row(g_norm), w_tok, wvt, wfl_pad, bfl_pad, tm=tm_in)
    yfox = _fox(q, fqa, k, fka, vt, *_fox_skip_tables(fend, nq, nk), t=t_attn)
    tables = _s5_tables(lam_re, lam_im, log_step, s5_b_re, s5_b_im, s5_c_re, s5_c_im, s5_d)
    yc = _s5(uc, *tables)
    return _final(x, yfox, yc, mk, mv, row(g_norm), row(g_out), w_gate, row(b_merge),
                  w_glu.astype(BF16), row(b_glu), w_proj_fox.astype(BF16), w_proj_s5.astype(BF16),
                  w_proj_mem.astype(BF16), w_out.astype(BF16), tm=tm_out)


def kernel(x, mem, g_norm, g_mem_norm, g_final, w_in, b_forget, b_merge, w_mem_kv, lam_re, lam_im, log_step,
           s5_b_re, s5_b_im, s5_c_re, s5_c_im, s5_d, w_glu, b_glu, w_proj_fox, w_proj_s5, w_proj_mem, w_out):
    depth = w_in.shape[0]
    assert depth == 1, "the fused final kernel applies the closing RMSNorm: one layer only"
    l = x.shape[1]
    tm_in = min(512, l)
    t_attn = min(512, l)
    tm_out = min(512, l)
    return _layer(x, mem, g_norm[0], g_mem_norm[0], w_in[0], b_forget[0], b_merge[0], w_mem_kv[0],
                  lam_re[0], lam_im[0], log_step[0], s5_b_re[0], s5_b_im[0], s5_c_re[0], s5_c_im[0],
                  s5_d[0], w_glu[0], b_glu[0], w_proj_fox[0], w_proj_s5[0], w_proj_mem[0], w_out[0],
                  g_final, tm_in=tm_in, t_attn=t_attn, tm_out=tm_out)
```

```python
import functools
import math

import jax
import jax.numpy as jnp
import numpy as np
from jax import lax
from jax.experimental import pallas as pl
from jax.experimental.pallas import tpu as pltpu

F32 = jnp.float32
BF16 = jnp.bfloat16
HIGHEST = lax.Precision.HIGHEST

D_MODEL = 1024
EPS = 1e-6
NEG = -1e30
LOG2E = math.log2(math.e)
SKIP_LOG2 = 127.0
NORM_MARGIN = 1.03

FOX_HEAD_DIM = 64
D_FOX = 768
FOX_HEADS = D_FOX // FOX_HEAD_DIM

D_S5 = 768
S5_GROUP = 16
S5_GROUPS = D_S5 // S5_GROUP
S5_STATE = 64

D_MEM = 512
MEM_HEADS = 4
MEM_HEAD_DIM = D_MEM // MEM_HEADS

LANES = 128
S5_CHUNK = 16
S5_LANE_GROUPS = D_S5 // LANES
S5_GROUPS_PER_LANE_GROUP = LANES // S5_GROUP
S5_CW = S5_CHUNK * LANES
S5_HALF = S5_GROUPS_PER_LANE_GROUP * S5_STATE
S5_SW = 2 * S5_HALF
S5_SEGMENTS = 8

W_FL = 3 * D_FOX
W_REST = W_FL + FOX_HEADS
R_GF, R_U, R_GS, R_QM, R_GM, R_GL = 0, 768, 1536, 2304, 2816, 3328
R_COLS = R_GL + 3 * D_MODEL

VMEM_LIMIT = 56 * 1024 * 1024


def _dot(a, b, **kw):
    return jnp.dot(a, b, preferred_element_type=F32, **kw)


def _dot_nt(a, b, **kw):
    return lax.dot_general(a, b, (((1,), (1,)), ((), ())), preferred_element_type=F32, **kw)


def _rms(xf, g):
    return xf * lax.rsqrt(jnp.mean(xf * xf, axis=-1, keepdims=True) + EPS) * g


def _sigmoid(z):
    return 1.0 / (1.0 + jnp.exp(-z))


def _silu(z):
    return z * _sigmoid(z)


def _gelu_tanh(y):
    return 0.5 * y * (1.0 + jnp.tanh(math.sqrt(2.0 / math.pi) * (y + 0.044715 * (y * y * y))))


def _split3(a):
    hi = a.astype(BF16)
    rest = a - hi.astype(F32)
    mid = rest.astype(BF16)
    lo = (rest - mid.astype(F32)).astype(BF16)
    return jnp.concatenate([hi, mid, lo], axis=-1)


def _const_spec(shape):
    nd = len(shape)
    return pl.BlockSpec(shape, lambda *_: (0,) * nd, pipeline_mode=pl.Buffered(1))


def _weight_prep_kernel(w_ref, wqk_ref, wvt_ref, wfl_ref, wrest_ref):
    wqk_ref[...] = w_ref[:, 0:2 * D_FOX].astype(BF16)
    wvt_ref[...] = w_ref[:, 2 * D_FOX:3 * D_FOX].T.astype(BF16)
    lane = lax.broadcasted_iota(jnp.int32, (1, LANES), 1)
    wfl_ref[...] = jnp.where(lane < FOX_HEADS, w_ref[:, W_FL:W_FL + LANES], 0.0).astype(BF16)
    wrest_ref[...] = w_ref[:, W_REST:W_REST + R_COLS].astype(BF16)


def _weight_prep(w_in, *, rows=128):
    _, d, n_in = w_in.shape
    assert n_in == W_REST + R_COLS
    return pl.pallas_call(
        _weight_prep_kernel,
        grid=(d // rows,),
        in_specs=[pl.BlockSpec((None, rows, n_in), lambda i: (0, i, 0))],
        out_specs=[pl.BlockSpec((rows, 2 * D_FOX), lambda i: (i, 0)),
                   pl.BlockSpec((D_FOX, rows), lambda i: (0, i)),
                   pl.BlockSpec((rows, LANES), lambda i: (i, 0)),
                   pl.BlockSpec((rows, R_COLS), lambda i: (i, 0))],
        out_shape=[jax.ShapeDtypeStruct((d, 2 * D_FOX), BF16),
                   jax.ShapeDtypeStruct((D_FOX, d), BF16),
                   jax.ShapeDtypeStruct((d, LANES), BF16),
                   jax.ShapeDtypeStruct((d, R_COLS), BF16)],
        compiler_params=pltpu.CompilerParams(dimension_semantics=("arbitrary",), vmem_limit_bytes=VMEM_LIMIT),
        name="weight_prep",
    )(w_in)


def _mem_kv_kernel(mem_ref, g_ref, w_ref, mk_ref, mv_ref):
    h = _rms(mem_ref[...], g_ref[...]).astype(BF16)
    kv = _dot(h, w_ref[...])
    mk_ref[...] = kv[:, :D_MEM].astype(BF16)
    mv_ref[...] = kv[:, D_MEM:].astype(BF16)


def _mem_kv(mem, g, w):
    bn, m, _ = mem.shape
    out = jax.ShapeDtypeStruct((bn, m, D_MEM), BF16)
    return pl.pallas_call(
        _mem_kv_kernel,
        grid=(bn,),
        in_specs=[pl.BlockSpec((None, m, D_MODEL), lambda b: (b, 0, 0)),
                  _const_spec((1, D_MODEL)),
                  _const_spec((D_MODEL, 2 * D_MEM))],
        out_specs=[pl.BlockSpec((None, m, D_MEM), lambda b: (b, 0, 0))] * 2,
        out_shape=[out, out],
        name="mem_kv",
    )(mem, g, w)


def _in_proj_kernel(x_ref, g_ref, w_ref, wu_ref, wvt_ref, wfl_ref, bfl_ref, spread_ref, ones_ref,
                    q_ref, k_ref, vt_ref, uc_ref, fqa_ref, fka_ref, fend_ref, nq_ref, nk_ref,
                    us_scr, carry_scr, *, tm):
    @pl.when(pl.program_id(1) == 0)
    def _():
        carry_scr[...] = jnp.zeros_like(carry_scr)

    h = _rms(x_ref[...], g_ref[...]).astype(BF16)
    qb = (_dot(h, w_ref[:, 0:D_FOX]) * (LOG2E * FOX_HEAD_DIM ** -0.5)).astype(BF16)
    kb = _dot(h, w_ref[:, D_FOX:2 * D_FOX]).astype(BF16)
    q_ref[...] = qb
    k_ref[...] = kb
    for pair in range(FOX_HEADS // 2):
        vt_ref[pair] = _dot_nt(wvt_ref[pair * LANES:(pair + 1) * LANES, :], h).astype(BF16)

    head_of_col = lax.broadcasted_iota(jnp.int32, (D_FOX, LANES), 0) // FOX_HEAD_DIM
    sel = (head_of_col == lax.broadcasted_iota(jnp.int32, (D_FOX, LANES), 1)).astype(BF16)
    sq = lambda a: (a.astype(F32) * a.astype(F32)).astype(BF16)
    nq_ref[...] = jnp.max(_dot(sq(qb), sel), axis=0, keepdims=True)
    nk_ref[...] = jnp.max(_dot(sq(kb), sel), axis=0, keepdims=True)

    u = _dot(h, wu_ref[...])
    nc = tm // S5_CHUNK
    for lg in range(S5_LANE_GROUPS):
        us_scr[lg] = u[:, lg * LANES:(lg + 1) * LANES]
        for j in range(S5_CHUNK):
            c0 = lg * S5_CW + j * LANES
            uc_ref[:, c0:c0 + LANES] = us_scr[lg, pl.ds(j, nc, stride=S5_CHUNK), :].astype(BF16)

    z = _dot(h, wfl_ref[...]) + bfl_ref[...]
    logf = (jnp.minimum(z, 0.0) - jnp.log1p(jnp.exp(-jnp.abs(z)))) * LOG2E
    row = lax.broadcasted_iota(jnp.int32, (tm, tm), 0)
    col = lax.broadcasted_iota(jnp.int32, (tm, tm), 1)
    tri = (col <= row).astype(BF16)
    parts = _dot(tri, _split3(logf))
    f = (parts[:, 0:LANES] + parts[:, LANES:2 * LANES]) + parts[:, 2 * LANES:] + carry_scr[...]
    carry_scr[...] = f[tm - 1:tm, :]
    fend_ref[...] = f[tm - 1:tm, :]
    aug = _dot(_split3(f), spread_ref[...]) + ones_ref[...]
    fka_ref[...] = aug[:, 0:LANES].astype(BF16)
    fqa_ref[...] = aug[:, LANES:].astype(BF16)


def _fox_aug_constants():
    spread = np.zeros((3 * LANES, 2 * LANES), np.float32)
    ones = np.zeros((1, 2 * LANES), np.float32)
    half = LANES // 2
    for h in range(FOX_HEADS):
        for i in range(3):
            spread[i * LANES + h, 3 * h + i] = 1.0
            spread[i * LANES + h, LANES + half + 3 * h + i] = 1.0
            ones[0, half + 3 * h + i] = 1.0
            ones[0, LANES + 3 * h + i] = -1.0
    return jnp.asarray(spread, BF16), jnp.asarray(ones, F32)


def _in_proj(x, g, wqk, wrest, wvt, wfl, bfl, *, tm):
    bn, l, _ = x.shape
    nc = tm // S5_CHUNK
    nt = l // tm
    pairs = FOX_HEADS // 2
    spread, ones = _fox_aug_constants()
    tok = lambda width: pl.BlockSpec((None, tm, width), lambda b, i: (b, i, 0))
    per_tile = pl.BlockSpec((None, None, 1, LANES), lambda b, i: (b, i, 0, 0))
    qk = jax.ShapeDtypeStruct((bn, l, D_FOX), BF16)
    aug = jax.ShapeDtypeStruct((bn, l, LANES), BF16)
    stat = jax.ShapeDtypeStruct((bn, nt, 1, LANES), F32)
    return pl.pallas_call(
        functools.partial(_in_proj_kernel, tm=tm),
        grid=(bn, nt),
        in_specs=[tok(D_MODEL),
                  _const_spec((1, D_MODEL)),
                  _const_spec((D_MODEL, 2 * D_FOX)),
                  pl.BlockSpec((D_MODEL, D_S5), lambda *_: (0, R_U // D_S5), pipeline_mode=pl.Buffered(1)),
                  _const_spec((D_FOX, D_MODEL)),
                  _const_spec((D_MODEL, LANES)),
                  _const_spec((1, LANES)),
                  _const_spec((3 * LANES, 2 * LANES)),
                  _const_spec((1, 2 * LANES))],
        out_specs=[tok(D_FOX), tok(D_FOX),
                   pl.BlockSpec((None, pairs, None, LANES, tm), lambda b, i: (b, 0, i, 0, 0)),
                   pl.BlockSpec((None, nc, S5_LANE_GROUPS * S5_CW), lambda b, i: (b, i, 0)),
                   tok(LANES), tok(LANES), per_tile, per_tile, per_tile],
        out_shape=[qk, qk,
                   jax.ShapeDtypeStruct((bn, pairs, nt, LANES, tm), BF16),
                   jax.ShapeDtypeStruct((bn, l // S5_CHUNK, S5_LANE_GROUPS * S5_CW), BF16),
                   aug, aug, stat, stat, stat],
        scratch_shapes=[pltpu.VMEM((S5_LANE_GROUPS, tm, LANES), F32), pltpu.VMEM((1, LANES), F32)],
        compiler_params=pltpu.CompilerParams(
            dimension_semantics=("arbitrary", "arbitrary"), vmem_limit_bytes=VMEM_LIMIT),
        name="in_proj",
    )(x, g, wqk, wrest, wvt, wfl, bfl, spread, ones)


def _fox_kernel(fend_ref, qkb_ref, q_ref, fqa_ref, k_ref, fka_ref, vt_ref, o_ref,
                m_scr, l_scr, alpha_scr, acc_scr, s_scr, p_scr, *, t, nt):
    b = pl.program_id(0)
    pair = pl.program_id(1)
    qi = pl.program_id(2)
    lane = lax.broadcasted_iota(jnp.int32, (1, LANES), 1)
    q = q_ref[...]
    fqa = fqa_ref[...]
    key_pos = lax.broadcasted_iota(jnp.int32, (t, t), 0)
    query_pos = lax.broadcasted_iota(jnp.int32, (t, t), 1)

    heads = (0, 1)
    q_aug, base = [], []
    for e in heads:
        head = 2 * pair + e
        head_lanes = (lane >= e * FOX_HEAD_DIM) & (lane < (e + 1) * FOX_HEAD_DIM)
        f_lanes = lax.rem(lane, LANES // 2) // 3 == head
        q_aug.append(jnp.concatenate([jnp.where(head_lanes, q, jnp.zeros_like(q)),
                                      jnp.where(f_lanes, fqa, jnp.zeros_like(fqa))], axis=-1))
        base.append((b * FOX_HEADS + head) * nt)

    def scores(e, j):
        k0 = pl.multiple_of(jnp.maximum(j, 0) * t, t)
        k_aug = jnp.concatenate([k_ref[pl.ds(k0, t), :], fka_ref[pl.ds(k0, t), :]], axis=-1)
        return _dot_nt(k_aug, q_aug[e])

    def weighted_values(e, j):
        acc_scr[e] = alpha_scr[e] * acc_scr[e] + _dot(vt_ref[j], p_scr[e])

    slack = []
    for e in heads:
        s = jnp.where(key_pos <= query_pos, scores(e, qi), NEG)
        m = jnp.max(s, axis=0, keepdims=True)
        p = jnp.exp2(s - m)
        m_scr[e] = m
        l_scr[e] = jnp.sum(p, axis=0, keepdims=True)
        p_scr[e] = p.astype(BF16)
        alpha_scr[e] = jnp.zeros_like(m)
        acc_scr[e] = jnp.zeros((LANES, t), F32)
        slack.append(qkb_ref[base[e] + qi] - jnp.min(m) + SKIP_LOG2)
        s_scr[e] = scores(e, qi - 1)

    def wanted(e, j):
        f_hi = fend_ref[base[e] + jnp.maximum(qi - 1, 0)]
        reach = f_hi - fend_ref[base[e] + jnp.maximum(j, 0)] + slack[e]
        return jnp.logical_and(j >= 0, reach >= 0.0).astype(jnp.int32)

    def trip(j, which):
        for e in which:
            weighted_values(e, j + 1)
        for e in which:
            s = s_scr[e]
            m_old = m_scr[e]
            m_new = jnp.maximum(m_old, jnp.max(s, axis=0, keepdims=True))
            p = jnp.exp2(s - m_new)
            alpha = jnp.exp2(m_old - m_new)
            m_scr[e] = m_new
            l_scr[e] = alpha * l_scr[e] + jnp.sum(p, axis=0, keepdims=True)
            p_scr[e] = p.astype(BF16)
            alpha_scr[e] = alpha
        for e in which:
            s_scr[e] = scores(e, j - 1)

    def walk(j, which):
        def go(jj):
            flag = wanted(which[0], jj)
            for e in which[1:]:
                flag = flag * wanted(e, jj)
            return flag

        def body(carry):
            trip(carry[0], which)
            return carry[0] - 1, go(carry[0] - 1)

        return lax.while_loop(lambda c: c[1] > 0, body, (j, go(j)))[0]

    j_both = walk(qi - 1, heads)
    outs = []
    for e in heads:
        j_stop = walk(j_both, (e,))
        weighted_values(e, j_stop + 1)
        outs.append(acc_scr[e] * (1.0 / l_scr[e]))
    dim = lax.broadcasted_iota(jnp.int32, (LANES, 1), 0)
    o_ref[...] = jnp.where(dim < FOX_HEAD_DIM, outs[0], outs[1]).T.astype(BF16)


def _fox(q, fqa, k, fka, vt, fend, qkb, *, t):
    bn, l, _ = q.shape
    pairs = FOX_HEADS // 2
    nt = l // t
    query_tile = pl.BlockSpec((None, t, LANES), lambda b, p, i, *_: (b, i, p))
    grid_spec = pltpu.PrefetchScalarGridSpec(
        num_scalar_prefetch=2,
        grid=(bn, pairs, nt),
        in_specs=[query_tile,
                  pl.BlockSpec((None, t, LANES), lambda b, p, i, *_: (b, i, 0)),
                  pl.BlockSpec((None, l, LANES), lambda b, p, i, *_: (b, 0, p)),
                  pl.BlockSpec((None, l, LANES), lambda b, p, i, *_: (b, 0, 0)),
                  pl.BlockSpec((None, None, nt, LANES, t), lambda b, p, i, *_: (b, p, 0, 0, 0))],
        out_specs=query_tile,
        scratch_shapes=[pltpu.VMEM((2, 1, t), F32), pltpu.VMEM((2, 1, t), F32), pltpu.VMEM((2, 1, t), F32),
                        pltpu.VMEM((2, LANES, t), F32), pltpu.VMEM((2, t, t), F32),
                        pltpu.VMEM((2, t, t), BF16)])
    return pl.pallas_call(
        functools.partial(_fox_kernel, t=t, nt=nt),
        grid_spec=grid_spec,
        out_shape=jax.ShapeDtypeStruct((bn, l, D_FOX), BF16),
        compiler_params=pltpu.CompilerParams(
            dimension_semantics=("arbitrary", "arbitrary", "arbitrary"), vmem_limit_bytes=VMEM_LIMIT),
        name="fox",
    )(fend, qkb, q, fqa, k, fka, vt)


def _fox_skip_tables(fend, nq, nk):
    per_head = lambda a: jnp.swapaxes(a[:, :, 0, :FOX_HEADS], 1, 2)
    qmax = jnp.sqrt(per_head(nq))
    kmax = jnp.sqrt(jnp.max(per_head(nk), axis=2, keepdims=True))
    return per_head(fend).reshape(-1), (NORM_MARGIN * qmax * kmax).reshape(-1)


def _s5_tables_kernel(lr_ref, li_ref, ls_ref, btr_ref, bti_ref, cr_ref, ci_ref, d_ref,
                      wtoe_ref, win_ref, voutt_ref, lam_ref):
    lr, li = lr_ref[...], li_ref[...]
    step = jnp.exp(ls_ref[...])
    mag = jnp.exp(lr * step)
    ab_re, ab_im = mag * jnp.cos(li * step), mag * jnp.sin(li * step)
    den = lr * lr + li * li
    nr, ni = ab_re - 1.0, ab_im
    f_re = (nr * lr + ni * li) / den
    f_im = (ni * lr - nr * li) / den
    btr, bti = btr_ref[...], bti_ref[...]
    bb_re = f_re * btr - f_im * bti
    bb_im = f_re * bti + f_im * btr
    cr, ci = cr_ref[...], ci_ref[...]
    eye = (lax.broadcasted_iota(jnp.int32, (LANES, LANES), 0)
           == lax.broadcasted_iota(jnp.int32, (LANES, LANES), 1))
    tile = lambda j: slice(j * LANES, (j + 1) * LANES)
    zeros = jnp.zeros((LANES, LANES), BF16)
    for jp in range(S5_CHUNK):
        for j in range(jp):
            wtoe_ref[tile(jp), tile(j)] = zeros
    pr = jnp.ones_like(lr)
    pi = jnp.zeros_like(lr)
    for tau in range(S5_CHUNK + 1):
        a_re = cr * pr - ci * pi
        a_im = cr * pi + ci * pr
        if tau < S5_CHUNK:
            kt = _dot_nt(bb_re, a_re, precision=HIGHEST) - _dot_nt(bb_im, a_im, precision=HIGHEST)
            if tau == 0:
                kt = kt + jnp.where(eye, d_ref[...], 0.0)
            kt = kt.astype(BF16)
            for jp in range(S5_CHUNK - tau):
                wtoe_ref[tile(jp), tile(jp + tau)] = kt
            j = S5_CHUNK - 1 - tau
            win_ref[tile(j), 0:S5_HALF] = (pr * bb_re - pi * bb_im).astype(BF16)
            win_ref[tile(j), S5_HALF:] = (pr * bb_im + pi * bb_re).astype(BF16)
        if tau >= 1:
            voutt_ref[tile(tau - 1), 0:S5_HALF] = a_re.astype(BF16)
            voutt_ref[tile(tau - 1), S5_HALF:] = (-a_im).astype(BF16)
        if tau == S5_CHUNK:
            lam_ref[:, 0:S5_HALF] = pr
            lam_ref[:, S5_HALF:] = pi
        pr, pi = pr * ab_re - pi * ab_im, pr * ab_im + pi * ab_re


def _s5_tables(lam_re, lam_im, log_step, b_re, b_im, c_re, c_im, d):
    lg, gp = S5_LANE_GROUPS, S5_GROUPS_PER_LANE_GROUP
    eye = jnp.eye(gp, dtype=F32)

    def block_diag(a):
        a = a.astype(F32).reshape(lg, gp, S5_GROUP, S5_STATE)
        return jnp.einsum("Ggcp,gh->Ggchp", a, eye).reshape(lg, LANES, S5_HALF)

    lanes = lambda a: a.astype(F32).reshape(lg, 1, S5_HALF)
    step = jnp.broadcast_to(log_step.astype(F32)[:, None], (S5_GROUPS, S5_STATE))
    spec = lambda r, c: pl.BlockSpec((None, r, c), lambda g: (g, 0, 0))
    return pl.pallas_call(
        _s5_tables_kernel,
        grid=(lg,),
        in_specs=[spec(1, S5_HALF)] * 3 + [spec(LANES, S5_HALF)] * 4 + [spec(1, LANES)],
        out_specs=[spec(S5_CW, S5_CW), spec(S5_CW, S5_SW), spec(S5_CW, S5_SW), spec(1, S5_SW)],
        out_shape=[jax.ShapeDtypeStruct((lg, S5_CW, S5_CW), BF16),
                   jax.ShapeDtypeStruct((lg, S5_CW, S5_SW), BF16),
                   jax.ShapeDtypeStruct((lg, S5_CW, S5_SW), BF16),
                   jax.ShapeDtypeStruct((lg, 1, S5_SW), F32)],
        compiler_params=pltpu.CompilerParams(
            dimension_semantics=("arbitrary",), vmem_limit_bytes=VMEM_LIMIT),
        name="s5_tables",
    )(lanes(lam_re), lanes(lam_im), lanes(step),
      block_diag(jnp.swapaxes(b_re, 1, 2)), block_diag(jnp.swapaxes(b_im, 1, 2)),
      block_diag(c_re), block_diag(c_im), d.astype(F32).reshape(lg, 1, LANES))


def _s5_kernel(uc_ref, wtoe_ref, win_ref, voutt_ref, lam_ref, yc_ref, e_scr, hs_scr, *, n):
    x = uc_ref[...]
    e = _dot(x, win_ref[...])
    tiles = S5_SW // LANES
    half = tiles // 2
    lane_tile = lambda c: slice(c * LANES, (c + 1) * LANES)
    for c in range(tiles):
        e_scr[c, 0:n, :] = e[:, lane_tile(c)]
        e_scr[c, n:, :] = jnp.zeros((S5_SEGMENTS, LANES), F32)
    lam = jnp.broadcast_to(lam_ref[...], (S5_SEGMENTS, S5_SW))
    lam_r = [lam[:, lane_tile(c)] for c in range(half)]
    lam_i = [lam[:, lane_tile(half + c)] for c in range(half)]
    cmul = lambda ar, ai, br, bi: (ar * br - ai * bi, ar * bi + ai * br)

    seg = n // S5_SEGMENTS + 1
    run_rows = lambda r: pl.ds(r, S5_SEGMENTS, stride=seg)
    zero = jnp.zeros((S5_SEGMENTS, LANES), F32)
    one = jnp.ones((S5_SEGMENTS, LANES), F32)
    zero_row = jnp.zeros((1, LANES), F32)

    def scan_runs(r, carry):
        h, pw = carry
        h_next, pw_next = [None] * tiles, [None] * tiles
        for c in range(half):
            hs_scr[c, run_rows(r), :] = h[c]
            hs_scr[half + c, run_rows(r), :] = h[half + c]
            nr, ni = cmul(lam_r[c], lam_i[c], h[c], h[half + c])
            h_next[c] = nr + e_scr[c, run_rows(r), :]
            h_next[half + c] = ni + e_scr[half + c, run_rows(r), :]
            pw_next[c], pw_next[half + c] = cmul(lam_r[c], lam_i[c], pw[c], pw[half + c])
        return tuple(h_next), tuple(pw_next)

    unit = (one,) * half + (zero,) * half
    run_end, run_mult = lax.fori_loop(0, seg, scan_runs, ((zero,) * tiles, unit))

    start = [None] * tiles
    for c in range(half):
        sr, si, rows_r, rows_i = zero_row, zero_row, [], []
        for s in range(S5_SEGMENTS):
            rows_r.append(sr)
            rows_i.append(si)
            nr, ni = cmul(run_mult[c][0:1], run_mult[half + c][0:1], sr, si)
            sr, si = nr + run_end[c][s:s + 1], ni + run_end[half + c][s:s + 1]
        start[c], start[half + c] = jnp.concatenate(rows_r, axis=0), jnp.concatenate(rows_i, axis=0)

    def add_run_starts(r, pw):
        pw_next = [None] * tiles
        for c in range(half):
            ar, ai = cmul(pw[c], pw[half + c], start[c], start[half + c])
            hs_scr[c, run_rows(r), :] = hs_scr[c, run_rows(r), :] + ar
            hs_scr[half + c, run_rows(r), :] = hs_scr[half + c, run_rows(r), :] + ai
            pw_next[c], pw_next[half + c] = cmul(lam_r[c], lam_i[c], pw[c], pw[half + c])
        return tuple(pw_next)

    lax.fori_loop(0, seg, add_run_starts, unit)
    hs = jnp.concatenate([hs_scr[c, 0:n, :] for c in range(tiles)], axis=-1).astype(BF16)
    blk = 2 * LANES
    for jb in range(S5_CW // blk):
        cols = slice(jb * blk, (jb + 1) * blk)
        kk = (jb + 1) * blk
        y = _dot(x[:, :kk], wtoe_ref[0:kk, cols]) + _dot_nt(hs, voutt_ref[cols, :])
        yc_ref[:, cols] = y.astype(BF16)


def _s5(uc, wtoe, win, voutt, lam):
    bn, n, _ = uc.shape
    per_lg = lambda r, c: pl.BlockSpec((None, r, c), lambda g, b: (g, 0, 0))
    return pl.pallas_call(
        functools.partial(_s5_kernel, n=n),
        grid=(S5_LANE_GROUPS, bn),
        in_specs=[pl.BlockSpec((None, n, S5_CW), lambda g, b: (b, 0, g)),
                  per_lg(S5_CW, S5_CW), per_lg(S5_CW, S5_SW), per_lg(S5_CW, S5_SW), per_lg(1, S5_SW)],
        out_specs=pl.BlockSpec((None, n, S5_CW), lambda g, b: (b, 0, g)),
        out_shape=jax.ShapeDtypeStruct(uc.shape, BF16),
        scratch_shapes=[pltpu.VMEM((S5_SW // LANES, n + S5_SEGMENTS, LANES), F32)] * 2,
        compiler_params=pltpu.CompilerParams(
            dimension_semantics=("arbitrary", "arbitrary"), vmem_limit_bytes=VMEM_LIMIT),
        name="s5",
    )(uc, wtoe, win, voutt, lam)


def _final_kernel(x_ref, yfox_ref, yc_ref, mk_ref, mv_ref, gn_ref, gfin_ref, wg_ref, bm_ref,
                  wglu_ref, bglu_ref, wpf_ref, wps_ref, wpm_ref, wout_ref, o_ref, ys_scr, *, tm):
    x = x_ref[...]
    h = _rms(x, gn_ref[...]).astype(BF16)

    def proj(c0, width):
        return _dot(h, wg_ref[:, c0:c0 + width])

    def gate(i):
        return _sigmoid(proj(R_GL + i * D_MODEL, D_MODEL) + bm_ref[:, i * D_MODEL:(i + 1) * D_MODEL])

    a = (yfox_ref[...].astype(F32) * _silu(proj(R_GF, D_FOX))).astype(BF16)
    merged = gate(0) * _dot(a, wpf_ref[...])

    nc = tm // S5_CHUNK
    for lg in range(S5_LANE_GROUPS):
        for j in range(S5_CHUNK):
            c0 = lg * S5_CW + j * LANES
            ys_scr[lg, pl.ds(j, nc, stride=S5_CHUNK), :] = yc_ref[:, c0:c0 + LANES].astype(F32)
    y = _gelu_tanh(jnp.concatenate([ys_scr[lg] for lg in range(S5_LANE_GROUPS)], axis=-1))
    y = y * _sigmoid(_dot(y.astype(BF16), wglu_ref[...]) + bglu_ref[...])
    y = (y * _silu(proj(R_GS, D_S5))).astype(BF16)
    merged = merged + gate(1) * _dot(y, wps_ref[...])

    qm = (proj(R_QM, D_MEM) * (MEM_HEAD_DIM ** -0.5)).astype(BF16)
    heads = []
    for hd in range(MEM_HEADS):
        cols = slice(hd * MEM_HEAD_DIM, (hd + 1) * MEM_HEAD_DIM)
        s = _dot_nt(qm[:, cols], mk_ref[:, cols])
        p = jnp.exp(s - jnp.max(s, axis=-1, keepdims=True))
        inv = 1.0 / jnp.sum(p, axis=-1, keepdims=True)
        heads.append(_dot(p.astype(BF16), mv_ref[:, cols]) * inv)
    ymem = jnp.concatenate(heads, axis=-1)
    a = (ymem * _silu(proj(R_GM, D_MEM))).astype(BF16)
    merged = merged + gate(2) * _dot(a, wpm_ref[...])

    out = x + _dot(merged.astype(BF16), wout_ref[...])
    o_ref[...] = _rms(out, gfin_ref[...])


def _final(x, yfox, yc, mk, mv, gn, gfin, wg, bm, wglu, bglu, wpf, wps, wpm, wout, *, tm):
    bn, l, _ = x.shape
    m = mk.shape[1]
    nc = tm // S5_CHUNK
    tok = lambda width: pl.BlockSpec((None, tm, width), lambda b, i: (b, i, 0))
    mem = pl.BlockSpec((None, m, D_MEM), lambda b, i: (b, 0, 0))
    return pl.pallas_call(
        functools.partial(_final_kernel, tm=tm),
        grid=(bn, l // tm),
        in_specs=[tok(D_MODEL), tok(D_FOX),
                  pl.BlockSpec((None, nc, S5_LANE_GROUPS * S5_CW), lambda b, i: (b, i, 0)),
                  mem, mem,
                  _const_spec((1, D_MODEL)), _const_spec((1, D_MODEL)),
                  _const_spec((D_MODEL, R_COLS)), _const_spec((1, 3 * D_MODEL)),
                  _const_spec((D_S5, D_S5)), _const_spec((1, D_S5)),
                  _const_spec((D_FOX, D_MODEL)), _const_spec((D_S5, D_MODEL)),
                  _const_spec((D_MEM, D_MODEL)), _const_spec((D_MODEL, D_MODEL))],
        out_specs=tok(D_MODEL),
        out_shape=jax.ShapeDtypeStruct(x.shape, x.dtype),
        scratch_shapes=[pltpu.VMEM((S5_LANE_GROUPS, tm, LANES), F32)],
        compiler_params=pltpu.CompilerParams(
            dimension_semantics=("arbitrary", "arbitrary"), vmem_limit_bytes=VMEM_LIMIT),
        name="final",
    )(x, yfox, yc, mk, mv, gn, gfin, wg, bm, wglu, bglu, wpf, wps, wpm, wout)


def _layer(x, mem, g_norm, g_mem_norm, w_in, b_forget, b_merge, w_mem_kv, lam_re, lam_im, log_step,
           s5_b_re, s5_b_im, s5_c_re, s5_c_im, s5_d, w_glu, b_glu, w_proj_fox, w_proj_s5, w_proj_mem,
           w_out, g_out, *, tm_in, t_attn, tm_out):
    bn, l, _ = x.shape
    assert tm_in == t_attn, "the per-tile q norms from in_proj are indexed by attention query tile"
    row = lambda a: a.reshape(1, -1).astype(F32)
    wqk, wvt, wfl_pad, wrest = _weight_prep(w_in)
    bfl_pad = jnp.pad(row(b_forget), ((0, 0), (0, LANES - FOX_HEADS)))

    mk, mv = _mem_kv(mem, row(g_mem_norm), w_mem_kv.astype(BF16))
    q, k, vt, uc, fqa, fka, fend, nq, nk = _in_proj(x, row(g_norm), wqk, wrest, wvt, wfl_pad, bfl_pad, tm=tm_in)
    yfox = _fox(q, fqa, k, fka, vt, *_fox_skip_tables(fend, nq, nk), t=t_attn)
    tables = _s5_tables(lam_re, lam_im, log_step, s5_b_re, s5_b_im, s5_c_re, s5_c_im, s5_d)
    yc = _s5(uc, *tables)
    return _final(x, yfox, yc, mk, mv, row(g_norm), row(g_out), wrest, row(b_merge),
                  w_glu.astype(BF16), row(b_glu), w_proj_fox.astype(BF16), w_proj_s5.astype(BF16),
                  w_proj_mem.astype(BF16), w_out.astype(BF16), tm=tm_out)


def kernel(x, mem, g_norm, g_mem_norm, g_final, w_in, b_forget, b_merge, w_mem_kv, lam_re, lam_im, log_step,
           s5_b_re, s5_b_im, s5_c_re, s5_c_im, s5_d, w_glu, b_glu, w_proj_fox, w_proj_s5, w_proj_mem, w_out):
    depth = w_in.shape[0]
    assert depth == 1, "the fused final kernel applies the closing RMSNorm: one layer only"
    l = x.shape[1]
    tm_in = min(512, l)
    t_attn = min(512, l)
    tm_out = min(512, l)
    return _layer(x, mem, g_norm[0], g_mem_norm[0], w_in, b_forget[0], b_merge[0], w_mem_kv[0],
                  lam_re[0], lam_im[0], log_step[0], s5_b_re[0], s5_b_im[0], s5_c_re[0], s5_c_im[0],
                  s5_d[0], w_glu[0], b_glu[0], w_proj_fox[0], w_proj_s5[0], w_proj_mem[0], w_out[0],
                  g_final, tm_in=tm_in, t_attn=t_attn, tm_out=tm_out)
```

```python
import functools
import math

import jax
import jax.numpy as jnp
import numpy as np
from jax import lax
from jax.experimental import pallas as pl
from jax.experimental.pallas import tpu as pltpu

F32 = jnp.float32
BF16 = jnp.bfloat16

D_MODEL = 1024
EPS = 1e-6
NEG = -1e30
LOG2E = math.log2(math.e)
SKIP_LOG2 = 127.0
NORM_MARGIN = 1.03

FOX_HEAD_DIM = 64
D_FOX = 768
FOX_HEADS = D_FOX // FOX_HEAD_DIM

D_S5 = 768
S5_GROUP = 16
S5_GROUPS = D_S5 // S5_GROUP
S5_STATE = 64

D_MEM = 512
MEM_HEADS = 4
MEM_HEAD_DIM = D_MEM // MEM_HEADS

LANES = 128
S5_CHUNK = 16
S5_LANE_GROUPS = D_S5 // LANES
S5_GROUPS_PER_LANE_GROUP = LANES // S5_GROUP
S5_CW = S5_CHUNK * LANES
S5_HALF = S5_GROUPS_PER_LANE_GROUP * S5_STATE
S5_SW = 2 * S5_HALF
S5_SEGMENTS = 8

W_FL = 3 * D_FOX
W_REST = W_FL + FOX_HEADS
R_GF, R_U, R_GS, R_QM, R_GM, R_GL = 0, 768, 1536, 2304, 2816, 3328
R_COLS = R_GL + 3 * D_MODEL

VMEM_LIMIT = 56 * 1024 * 1024


def _dot(a, b, **kw):
    return jnp.dot(a, b, preferred_element_type=F32, **kw)


def _dot_nt(a, b, **kw):
    return lax.dot_general(a, b, (((1,), (1,)), ((), ())), preferred_element_type=F32, **kw)


def _dot_nt_split(a, b):
    a_hi, b_hi = a.astype(BF16), b.astype(BF16)
    a_lo = (a - a_hi.astype(F32)).astype(BF16)
    b_lo = (b - b_hi.astype(F32)).astype(BF16)
    return _dot_nt(a_hi, b_hi) + (_dot_nt(a_hi, b_lo) + _dot_nt(a_lo, b_hi))


def _rms(xf, g):
    return xf * lax.rsqrt(jnp.mean(xf * xf, axis=-1, keepdims=True) + EPS) * g


def _sigmoid(z):
    return 1.0 / (1.0 + jnp.exp(-z))


def _silu(z):
    return z * _sigmoid(z)


def _gelu_tanh(y):
    return 0.5 * y * (1.0 + jnp.tanh(math.sqrt(2.0 / math.pi) * (y + 0.044715 * (y * y * y))))


def _split3(a):
    hi = a.astype(BF16)
    rest = a - hi.astype(F32)
    mid = rest.astype(BF16)
    lo = (rest - mid.astype(F32)).astype(BF16)
    return jnp.concatenate([hi, mid, lo], axis=-1)


def _const_spec(shape):
    nd = len(shape)
    return pl.BlockSpec(shape, lambda *_: (0,) * nd, pipeline_mode=pl.Buffered(1))


def _weight_prep_kernel(w_ref, wqk_ref, wvt_ref, wfl_ref, wrest_ref):
    wqk_ref[...] = w_ref[:, 0:2 * D_FOX].astype(BF16)
    wvt_ref[...] = w_ref[:, 2 * D_FOX:3 * D_FOX].T.astype(BF16)
    lane = lax.broadcasted_iota(jnp.int32, (1, LANES), 1)
    wfl_ref[...] = jnp.where(lane < FOX_HEADS, w_ref[:, W_FL:W_FL + LANES], 0.0).astype(BF16)
    wrest_ref[...] = w_ref[:, W_REST:W_REST + R_COLS].astype(BF16)


def _weight_prep(w_in, *, rows=128):
    _, d, n_in = w_in.shape
    assert n_in == W_REST + R_COLS
    return pl.pallas_call(
        _weight_prep_kernel,
        grid=(d // rows,),
        in_specs=[pl.BlockSpec((None, rows, n_in), lambda i: (0, i, 0))],
        out_specs=[pl.BlockSpec((rows, 2 * D_FOX), lambda i: (i, 0)),
                   pl.BlockSpec((D_FOX, rows), lambda i: (0, i)),
                   pl.BlockSpec((rows, LANES), lambda i: (i, 0)),
                   pl.BlockSpec((rows, R_COLS), lambda i: (i, 0))],
        out_shape=[jax.ShapeDtypeStruct((d, 2 * D_FOX), BF16),
                   jax.ShapeDtypeStruct((D_FOX, d), BF16),
                   jax.ShapeDtypeStruct((d, LANES), BF16),
                   jax.ShapeDtypeStruct((d, R_COLS), BF16)],
        compiler_params=pltpu.CompilerParams(dimension_semantics=("arbitrary",), vmem_limit_bytes=VMEM_LIMIT),
        name="weight_prep",
    )(w_in)


def _mem_kv_kernel(mem_ref, g_ref, w_ref, mk_ref, mv_ref):
    h = _rms(mem_ref[...], g_ref[...]).astype(BF16)
    kv = _dot(h, w_ref[...])
    mk_ref[...] = kv[:, :D_MEM].astype(BF16)
    mv_ref[...] = kv[:, D_MEM:].astype(BF16)


def _mem_kv(mem, g, w):
    bn, m, _ = mem.shape
    out = jax.ShapeDtypeStruct((bn, m, D_MEM), BF16)
    return pl.pallas_call(
        _mem_kv_kernel,
        grid=(bn,),
        in_specs=[pl.BlockSpec((None, m, D_MODEL), lambda b: (b, 0, 0)),
                  _const_spec((1, D_MODEL)),
                  _const_spec((D_MODEL, 2 * D_MEM))],
        out_specs=[pl.BlockSpec((None, m, D_MEM), lambda b: (b, 0, 0))] * 2,
        out_shape=[out, out],
        name="mem_kv",
    )(mem, g, w)


def _in_proj_kernel(x_ref, g_ref, w_ref, wu_ref, wvt_ref, wfl_ref, bfl_ref, spread_ref, ones_ref,
                    q_ref, k_ref, vt_ref, uc_ref, fqa_ref, fka_ref, fend_ref, nq_ref, nk_ref,
                    us_scr, carry_scr, *, tm):
    @pl.when(pl.program_id(1) == 0)
    def _():
        carry_scr[...] = jnp.zeros_like(carry_scr)

    h = _rms(x_ref[...], g_ref[...]).astype(BF16)
    qb = (_dot(h, w_ref[:, 0:D_FOX]) * (LOG2E * FOX_HEAD_DIM ** -0.5)).astype(BF16)
    kb = _dot(h, w_ref[:, D_FOX:2 * D_FOX]).astype(BF16)
    q_ref[...] = qb
    k_ref[...] = kb
    for pair in range(FOX_HEADS // 2):
        vt_ref[pair] = _dot_nt(wvt_ref[pair * LANES:(pair + 1) * LANES, :], h).astype(BF16)

    head_of_col = lax.broadcasted_iota(jnp.int32, (D_FOX, LANES), 0) // FOX_HEAD_DIM
    sel = (head_of_col == lax.broadcasted_iota(jnp.int32, (D_FOX, LANES), 1)).astype(BF16)
    sq = lambda a: (a.astype(F32) * a.astype(F32)).astype(BF16)
    nq_ref[...] = jnp.max(_dot(sq(qb), sel), axis=0, keepdims=True)
    nk_ref[...] = jnp.max(_dot(sq(kb), sel), axis=0, keepdims=True)

    u = _dot(h, wu_ref[...])
    nc = tm // S5_CHUNK
    for lg in range(S5_LANE_GROUPS):
        us_scr[lg] = u[:, lg * LANES:(lg + 1) * LANES]
        for j in range(S5_CHUNK):
            c0 = lg * S5_CW + j * LANES
            uc_ref[:, c0:c0 + LANES] = us_scr[lg, pl.ds(j, nc, stride=S5_CHUNK), :].astype(BF16)

    z = _dot(h, wfl_ref[...]) + bfl_ref[...]
    logf = (jnp.minimum(z, 0.0) - jnp.log1p(jnp.exp(-jnp.abs(z)))) * LOG2E
    row = lax.broadcasted_iota(jnp.int32, (tm, tm), 0)
    col = lax.broadcasted_iota(jnp.int32, (tm, tm), 1)
    tri = (col <= row).astype(BF16)
    parts = _dot(tri, _split3(logf))
    f = (parts[:, 0:LANES] + parts[:, LANES:2 * LANES]) + parts[:, 2 * LANES:] + carry_scr[...]
    carry_scr[...] = f[tm - 1:tm, :]
    fend_ref[...] = f[tm - 1:tm, :]
    aug = _dot(_split3(f), spread_ref[...]) + ones_ref[...]
    fka_ref[...] = aug[:, 0:LANES].astype(BF16)
    fqa_ref[...] = aug[:, LANES:].astype(BF16)


def _fox_aug_constants():
    spread = np.zeros((3 * LANES, 2 * LANES), np.float32)
    ones = np.zeros((1, 2 * LANES), np.float32)
    half = LANES // 2
    for h in range(FOX_HEADS):
        for i in range(3):
            spread[i * LANES + h, 3 * h + i] = 1.0
            spread[i * LANES + h, LANES + half + 3 * h + i] = 1.0
            ones[0, half + 3 * h + i] = 1.0
            ones[0, LANES + 3 * h + i] = -1.0
    return jnp.asarray(spread, BF16), jnp.asarray(ones, F32)


def _in_proj(x, g, wqk, wrest, wvt, wfl, bfl, *, tm):
    bn, l, _ = x.shape
    nc = tm // S5_CHUNK
    nt = l // tm
    pairs = FOX_HEADS // 2
    spread, ones = _fox_aug_constants()
    tok = lambda width: pl.BlockSpec((None, tm, width), lambda b, i: (b, i, 0))
    per_tile = pl.BlockSpec((None, None, 1, LANES), lambda b, i: (b, i, 0, 0))
    qk = jax.ShapeDtypeStruct((bn, l, D_FOX), BF16)
    aug = jax.ShapeDtypeStruct((bn, l, LANES), BF16)
    stat = jax.ShapeDtypeStruct((bn, nt, 1, LANES), F32)
    return pl.pallas_call(
        functools.partial(_in_proj_kernel, tm=tm),
        grid=(bn, nt),
        in_specs=[tok(D_MODEL),
                  _const_spec((1, D_MODEL)),
                  _const_spec((D_MODEL, 2 * D_FOX)),
                  pl.BlockSpec((D_MODEL, D_S5), lambda *_: (0, R_U // D_S5), pipeline_mode=pl.Buffered(1)),
                  _const_spec((D_FOX, D_MODEL)),
                  _const_spec((D_MODEL, LANES)),
                  _const_spec((1, LANES)),
                  _const_spec((3 * LANES, 2 * LANES)),
                  _const_spec((1, 2 * LANES))],
        out_specs=[tok(D_FOX), tok(D_FOX),
                   pl.BlockSpec((None, pairs, None, LANES, tm), lambda b, i: (b, 0, i, 0, 0)),
                   pl.BlockSpec((None, nc, S5_LANE_GROUPS * S5_CW), lambda b, i: (b, i, 0)),
                   tok(LANES), tok(LANES), per_tile, per_tile, per_tile],
        out_shape=[qk, qk,
                   jax.ShapeDtypeStruct((bn, pairs, nt, LANES, tm), BF16),
                   jax.ShapeDtypeStruct((bn, l // S5_CHUNK, S5_LANE_GROUPS * S5_CW), BF16),
                   aug, aug, stat, stat, stat],
        scratch_shapes=[pltpu.VMEM((S5_LANE_GROUPS, tm, LANES), F32), pltpu.VMEM((1, LANES), F32)],
        compiler_params=pltpu.CompilerParams(
            dimension_semantics=("arbitrary", "arbitrary"), vmem_limit_bytes=VMEM_LIMIT),
        name="in_proj",
    )(x, g, wqk, wrest, wvt, wfl, bfl, spread, ones)


def _fox_kernel(fend_ref, qkb_ref, q_ref, fqa_ref, k_ref, fka_ref, vt_ref, o_ref,
                m_scr, l_scr, alpha_scr, acc_scr, s_scr, p_scr, *, t, nt):
    b = pl.program_id(0)
    pair = pl.program_id(1)
    qi = pl.program_id(2)
    lane = lax.broadcasted_iota(jnp.int32, (1, LANES), 1)
    q = q_ref[...]
    fqa = fqa_ref[...]
    key_pos = lax.broadcasted_iota(jnp.int32, (t, t), 0)
    query_pos = lax.broadcasted_iota(jnp.int32, (t, t), 1)

    heads = (0, 1)
    q_aug, base = [], []
    for e in heads:
        head = 2 * pair + e
        head_lanes = (lane >= e * FOX_HEAD_DIM) & (lane < (e + 1) * FOX_HEAD_DIM)
        f_lanes = lax.rem(lane, LANES // 2) // 3 == head
        q_aug.append(jnp.concatenate([jnp.where(head_lanes, q, jnp.zeros_like(q)),
                                      jnp.where(f_lanes, fqa, jnp.zeros_like(fqa))], axis=-1))
        base.append((b * FOX_HEADS + head) * nt)

    def scores(e, j):
        k0 = pl.multiple_of(jnp.maximum(j, 0) * t, t)
        k_aug = jnp.concatenate([k_ref[pl.ds(k0, t), :], fka_ref[pl.ds(k0, t), :]], axis=-1)
        return _dot_nt(k_aug, q_aug[e])

    def weighted_values(e, j, slot):
        acc_scr[e] = alpha_scr[e, slot] * acc_scr[e] + _dot(vt_ref[j], p_scr[e, slot])

    def softmax(e, s_slot, p_slot):
        s = s_scr[e, s_slot]
        m_old = m_scr[e]
        m_new = jnp.maximum(m_old, jnp.max(s, axis=0, keepdims=True))
        p = jnp.exp2(s - m_new)
        alpha = jnp.exp2(m_old - m_new)
        m_scr[e] = m_new
        l_scr[e] = alpha * l_scr[e] + jnp.sum(p, axis=0, keepdims=True)
        p_scr[e, p_slot] = p.astype(BF16)
        alpha_scr[e, p_slot] = alpha

    slack = []
    for e in heads:
        s = jnp.where(key_pos <= query_pos, scores(e, qi), NEG)
        m = jnp.max(s, axis=0, keepdims=True)
        p = jnp.exp2(s - m)
        m_scr[e] = m
        l_scr[e] = jnp.sum(p, axis=0, keepdims=True)
        p_scr[e, 0] = p.astype(BF16)
        alpha_scr[e, 0] = jnp.zeros_like(m)
        acc_scr[e] = jnp.zeros((LANES, t), F32)
        slack.append(qkb_ref[base[e] + qi] - jnp.min(m) + SKIP_LOG2)
        s_scr[e, 0] = scores(e, qi - 1)

    def wanted(e, j):
        f_hi = fend_ref[base[e] + jnp.maximum(qi - 1, 0)]
        reach = f_hi - fend_ref[base[e] + jnp.maximum(j, 0)] + slack[e]
        return jnp.logical_and(j >= 0, reach >= 0.0).astype(jnp.int32)

    def trip(j, which, tiles):
        assert tiles == 1 or tiles % 2 == 0
        for i in range(tiles):
            here, there = i % 2, (i + 1) % 2 if tiles > 1 else 0
            for e in which:
                weighted_values(e, j + 1 - i, here)
            for e in which:
                softmax(e, here, there)
            for e in which:
                s_scr[e, there] = scores(e, j - 1 - i)

    def walk(j, which, tiles):
        def go(jj):
            flag = wanted(which[0], jj - (tiles - 1))
            for e in which[1:]:
                flag = flag * wanted(e, jj - (tiles - 1))
            return flag

        def body(carry):
            trip(carry[0], which, tiles)
            return carry[0] - tiles, go(carry[0] - tiles)

        return lax.while_loop(lambda c: c[1] > 0, body, (j, go(j)))[0]

    j_both = walk(qi - 1, heads, 1)
    outs = []
    for e in heads:
        j_stop = walk(walk(j_both, (e,), 2), (e,), 1)
        weighted_values(e, j_stop + 1, 0)
        outs.append(acc_scr[e] * (1.0 / l_scr[e]))
    dim = lax.broadcasted_iota(jnp.int32, (LANES, 1), 0)
    o_ref[...] = jnp.where(dim < FOX_HEAD_DIM, outs[0], outs[1]).T.astype(BF16)


def _fox(q, fqa, k, fka, vt, fend, qkb, *, t):
    bn, l, _ = q.shape
    pairs = FOX_HEADS // 2
    nt = l // t
    query_tile = pl.BlockSpec((None, t, LANES), lambda b, p, i, *_: (b, i, p))
    grid_spec = pltpu.PrefetchScalarGridSpec(
        num_scalar_prefetch=2,
        grid=(bn, pairs, nt),
        in_specs=[query_tile,
                  pl.BlockSpec((None, t, LANES), lambda b, p, i, *_: (b, i, 0)),
                  pl.BlockSpec((None, l, LANES), lambda b, p, i, *_: (b, 0, p)),
                  pl.BlockSpec((None, l, LANES), lambda b, p, i, *_: (b, 0, 0)),
                  pl.BlockSpec((None, None, nt, LANES, t), lambda b, p, i, *_: (b, p, 0, 0, 0))],
        out_specs=query_tile,
        scratch_shapes=[pltpu.VMEM((2, 1, t), F32), pltpu.VMEM((2, 1, t), F32), pltpu.VMEM((2, 2, 1, t), F32),
                        pltpu.VMEM((2, LANES, t), F32), pltpu.VMEM((2, 2, t, t), F32),
                        pltpu.VMEM((2, 2, t, t), BF16)])
    return pl.pallas_call(
        functools.partial(_fox_kernel, t=t, nt=nt),
        grid_spec=grid_spec,
        out_shape=jax.ShapeDtypeStruct((bn, l, D_FOX), BF16),
        compiler_params=pltpu.CompilerParams(
            dimension_semantics=("arbitrary", "arbitrary", "arbitrary"), vmem_limit_bytes=VMEM_LIMIT),
        name="fox",
    )(fend, qkb, q, fqa, k, fka, vt)


def _fox_skip_tables(fend, nq, nk):
    per_head = lambda a: jnp.swapaxes(a[:, :, 0, :FOX_HEADS], 1, 2)
    qmax = jnp.sqrt(per_head(nq))
    kmax = jnp.sqrt(jnp.max(per_head(nk), axis=2, keepdims=True))
    return per_head(fend).reshape(-1), (NORM_MARGIN * qmax * kmax).reshape(-1)


def _s5_tables_kernel(lr_ref, li_ref, ls_ref, btr_ref, bti_ref, cr_ref, ci_ref, d_ref,
                      wtoe_ref, win_ref, voutt_ref, lam_ref):
    lr, li = lr_ref[...], li_ref[...]
    step = jnp.exp(ls_ref[...])
    mag = jnp.exp(lr * step)
    ab_re, ab_im = mag * jnp.cos(li * step), mag * jnp.sin(li * step)
    den = lr * lr + li * li
    nr, ni = ab_re - 1.0, ab_im
    f_re = (nr * lr + ni * li) / den
    f_im = (ni * lr - nr * li) / den
    btr, bti = btr_ref[...], bti_ref[...]
    bb_re = f_re * btr - f_im * bti
    bb_im = f_re * bti + f_im * btr
    cr, ci = cr_ref[...], ci_ref[...]
    eye = (lax.broadcasted_iota(jnp.int32, (LANES, LANES), 0)
           == lax.broadcasted_iota(jnp.int32, (LANES, LANES), 1))
    tile = lambda j: slice(j * LANES, (j + 1) * LANES)
    zeros = jnp.zeros((LANES, LANES), BF16)
    for jp in range(S5_CHUNK):
        for j in range(jp):
            wtoe_ref[tile(jp), tile(j)] = zeros
    pr = jnp.ones_like(lr)
    pi = jnp.zeros_like(lr)
    for tau in range(S5_CHUNK + 1):
        a_re = cr * pr - ci * pi
        a_im = cr * pi + ci * pr
        if tau < S5_CHUNK:
            kt = _dot_nt_split(bb_re, a_re) - _dot_nt_split(bb_im, a_im)
            if tau == 0:
                kt = kt + jnp.where(eye, d_ref[...], 0.0)
            kt = kt.astype(BF16)
            for jp in range(S5_CHUNK - tau):
                wtoe_ref[tile(jp), tile(jp + tau)] = kt
            j = S5_CHUNK - 1 - tau
            win_ref[tile(j), 0:S5_HALF] = (pr * bb_re - pi * bb_im).astype(BF16)
            win_ref[tile(j), S5_HALF:] = (pr * bb_im + pi * bb_re).astype(BF16)
        if tau >= 1:
            voutt_ref[tile(tau - 1), 0:S5_HALF] = a_re.astype(BF16)
            voutt_ref[tile(tau - 1), S5_HALF:] = (-a_im).astype(BF16)
        if tau == S5_CHUNK:
            lam_ref[:, 0:S5_HALF] = pr
            lam_ref[:, S5_HALF:] = pi
        pr, pi = pr * ab_re - pi * ab_im, pr * ab_im + pi * ab_re


def _s5_tables(lam_re, lam_im, log_step, b_re, b_im, c_re, c_im, d):
    lg, gp = S5_LANE_GROUPS, S5_GROUPS_PER_LANE_GROUP
    eye = jnp.eye(gp, dtype=F32)

    def block_diag(a):
        a = a.astype(F32).reshape(lg, gp, S5_GROUP, S5_STATE)
        return jnp.einsum("Ggcp,gh->Ggchp", a, eye).reshape(lg, LANES, S5_HALF)

    lanes = lambda a: a.astype(F32).reshape(lg, 1, S5_HALF)
    step = jnp.broadcast_to(log_step.astype(F32)[:, None], (S5_GROUPS, S5_STATE))
    spec = lambda r, c: pl.BlockSpec((None, r, c), lambda g: (g, 0, 0))
    return pl.pallas_call(
        _s5_tables_kernel,
        grid=(lg,),
        in_specs=[spec(1, S5_HALF)] * 3 + [spec(LANES, S5_HALF)] * 4 + [spec(1, LANES)],
        out_specs=[spec(S5_CW, S5_CW), spec(S5_CW, S5_SW), spec(S5_CW, S5_SW), spec(1, S5_SW)],
        out_shape=[jax.ShapeDtypeStruct((lg, S5_CW, S5_CW), BF16),
                   jax.ShapeDtypeStruct((lg, S5_CW, S5_SW), BF16),
                   jax.ShapeDtypeStruct((lg, S5_CW, S5_SW), BF16),
                   jax.ShapeDtypeStruct((lg, 1, S5_SW), F32)],
        compiler_params=pltpu.CompilerParams(
            dimension_semantics=("arbitrary",), vmem_limit_bytes=VMEM_LIMIT),
        name="s5_tables",
    )(lanes(lam_re), lanes(lam_im), lanes(step),
      block_diag(jnp.swapaxes(b_re, 1, 2)), block_diag(jnp.swapaxes(b_im, 1, 2)),
      block_diag(c_re), block_diag(c_im), d.astype(F32).reshape(lg, 1, LANES))


def _s5_kernel(uc_ref, wtoe_ref, win_ref, voutt_ref, lam_ref, yc_ref, e_scr, hs_scr, *, n):
    x = uc_ref[...]
    e = _dot(x, win_ref[...])
    tiles = S5_SW // LANES
    half = tiles // 2
    lane_tile = lambda c: slice(c * LANES, (c + 1) * LANES)
    for c in range(tiles):
        e_scr[c, 0:n, :] = e[:, lane_tile(c)]
        e_scr[c, n:, :] = jnp.zeros((S5_SEGMENTS, LANES), F32)
    lam = jnp.broadcast_to(lam_ref[...], (S5_SEGMENTS, S5_SW))
    lam_r = [lam[:, lane_tile(c)] for c in range(half)]
    lam_i = [lam[:, lane_tile(half + c)] for c in range(half)]
    cmul = lambda ar, ai, br, bi: (ar * br - ai * bi, ar * bi + ai * br)

    seg = n // S5_SEGMENTS + 1
    run_rows = lambda r: pl.ds(r, S5_SEGMENTS, stride=seg)
    zero = jnp.zeros((S5_SEGMENTS, LANES), F32)
    one = jnp.ones((S5_SEGMENTS, LANES), F32)
    zero_row = jnp.zeros((1, LANES), F32)

    def scan_runs(r, carry):
        h, pw = carry
        h_next, pw_next = [None] * tiles, [None] * tiles
        for c in range(half):
            hs_scr[c, run_rows(r), :] = h[c]
            hs_scr[half + c, run_rows(r), :] = h[half + c]
            nr, ni = cmul(lam_r[c], lam_i[c], h[c], h[half + c])
            h_next[c] = nr + e_scr[c, run_rows(r), :]
            h_next[half + c] = ni + e_scr[half + c, run_rows(r), :]
            pw_next[c], pw_next[half + c] = cmul(lam_r[c], lam_i[c], pw[c], pw[half + c])
        return tuple(h_next), tuple(pw_next)

    unit = (one,) * half + (zero,) * half
    run_end, run_mult = lax.fori_loop(0, seg, scan_runs, ((zero,) * tiles, unit))

    start = [None] * tiles
    for c in range(half):
        sr, si, rows_r, rows_i = zero_row, zero_row, [], []
        for s in range(S5_SEGMENTS):
            rows_r.append(sr)
            rows_i.append(si)
            nr, ni = cmul(run_mult[c][0:1], run_mult[half + c][0:1], sr, si)
            sr, si = nr + run_end[c][s:s + 1], ni + run_end[half + c][s:s + 1]
        start[c], start[half + c] = jnp.concatenate(rows_r, axis=0), jnp.concatenate(rows_i, axis=0)

    def add_run_starts(r, pw):
        pw_next = [None] * tiles
        for c in range(half):
            ar, ai = cmul(pw[c], pw[half + c], start[c], start[half + c])
            hs_scr[c, run_rows(r), :] = hs_scr[c, run_rows(r), :] + ar
            hs_scr[half + c, run_rows(r), :] = hs_scr[half + c, run_rows(r), :] + ai
            pw_next[c], pw_next[half + c] = cmul(lam_r[c], lam_i[c], pw[c], pw[half + c])
        return tuple(pw_next)

    lax.fori_loop(0, seg, add_run_starts, unit)
    hs = jnp.concatenate([hs_scr[c, 0:n, :] for c in range(tiles)], axis=-1).astype(BF16)
    blk = 2 * LANES
    for jb in range(S5_CW // blk):
        cols = slice(jb * blk, (jb + 1) * blk)
        kk = (jb + 1) * blk
        y = _dot(x[:, :kk], wtoe_ref[0:kk, cols]) + _dot_nt(hs, voutt_ref[cols, :])
        yc_ref[:, cols] = y.astype(BF16)


def _s5(uc, wtoe, win, voutt, lam):
    bn, n, _ = uc.shape
    per_lg = lambda r, c: pl.BlockSpec((None, r, c), lambda g, b: (g, 0, 0))
    return pl.pallas_call(
        functools.partial(_s5_kernel, n=n),
        grid=(S5_LANE_GROUPS, bn),
        in_specs=[pl.BlockSpec((None, n, S5_CW), lambda g, b: (b, 0, g)),
                  per_lg(S5_CW, S5_CW), per_lg(S5_CW, S5_SW), per_lg(S5_CW, S5_SW), per_lg(1, S5_SW)],
        out_specs=pl.BlockSpec((None, n, S5_CW), lambda g, b: (b, 0, g)),
        out_shape=jax.ShapeDtypeStruct(uc.shape, BF16),
        scratch_shapes=[pltpu.VMEM((S5_SW // LANES, n + S5_SEGMENTS, LANES), F32)] * 2,
        compiler_params=pltpu.CompilerParams(
            dimension_semantics=("arbitrary", "arbitrary"), vmem_limit_bytes=VMEM_LIMIT),
        name="s5",
    )(uc, wtoe, win, voutt, lam)


def _final_kernel(x_ref, yfox_ref, yc_ref, mk_ref, mv_ref, gn_ref, gfin_ref, wg_ref, bm_ref,
                  wglu_ref, bglu_ref, wpf_ref, wps_ref, wpm_ref, wout_ref, o_ref, ys_scr, *, tm):
    x = x_ref[...]
    h = _rms(x, gn_ref[...]).astype(BF16)

    def proj(c0, width):
        return _dot(h, wg_ref[:, c0:c0 + width])

    def gate(i):
        return _sigmoid(proj(R_GL + i * D_MODEL, D_MODEL) + bm_ref[:, i * D_MODEL:(i + 1) * D_MODEL])

    a = (yfox_ref[...].astype(F32) * _silu(proj(R_GF, D_FOX))).astype(BF16)
    merged = gate(0) * _dot(a, wpf_ref[...])

    nc = tm // S5_CHUNK
    for lg in range(S5_LANE_GROUPS):
        for j in range(S5_CHUNK):
            c0 = lg * S5_CW + j * LANES
            ys_scr[lg, pl.ds(j, nc, stride=S5_CHUNK), :] = yc_ref[:, c0:c0 + LANES].astype(F32)
    y = _gelu_tanh(jnp.concatenate([ys_scr[lg] for lg in range(S5_LANE_GROUPS)], axis=-1))
    y = y * _sigmoid(_dot(y.astype(BF16), wglu_ref[...]) + bglu_ref[...])
    y = (y * _silu(proj(R_GS, D_S5))).astype(BF16)
    merged = merged + gate(1) * _dot(y, wps_ref[...])

    qm = (proj(R_QM, D_MEM) * (MEM_HEAD_DIM ** -0.5)).astype(BF16)
    heads = []
    for hd in range(MEM_HEADS):
        cols = slice(hd * MEM_HEAD_DIM, (hd + 1) * MEM_HEAD_DIM)
        s = _dot_nt(qm[:, cols], mk_ref[:, cols])
        p = jnp.exp(s - jnp.max(s, axis=-1, keepdims=True))
        inv = 1.0 / jnp.sum(p, axis=-1, keepdims=True)
        heads.append(_dot(p.astype(BF16), mv_ref[:, cols]) * inv)
    ymem = jnp.concatenate(heads, axis=-1)
    a = (ymem * _silu(proj(R_GM, D_MEM))).astype(BF16)
    merged = merged + gate(2) * _dot(a, wpm_ref[...])

    out = x + _dot(merged.astype(BF16), wout_ref[...])
    o_ref[...] = _rms(out, gfin_ref[...])


def _final(x, yfox, yc, mk, mv, gn, gfin, wg, bm, wglu, bglu, wpf, wps, wpm, wout, *, tm):
    bn, l, _ = x.shape
    m = mk.shape[1]
    nc = tm // S5_CHUNK
    tok = lambda width: pl.BlockSpec((None, tm, width), lambda b, i: (b, i, 0))
    mem = pl.BlockSpec((None, m, D_MEM), lambda b, i: (b, 0, 0))
    return pl.pallas_call(
        functools.partial(_final_kernel, tm=tm),
        grid=(bn, l // tm),
        in_specs=[tok(D_MODEL), tok(D_FOX),
                  pl.BlockSpec((None, nc, S5_LANE_GROUPS * S5_CW), lambda b, i: (b, i, 0)),
                  mem, mem,
                  _const_spec((1, D_MODEL)), _const_spec((1, D_MODEL)),
                  _const_spec((D_MODEL, R_COLS)), _const_spec((1, 3 * D_MODEL)),
                  _const_spec((D_S5, D_S5)), _const_spec((1, D_S5)),
                  _const_spec((D_FOX, D_MODEL)), _const_spec((D_S5, D_MODEL)),
                  _const_spec((D_MEM, D_MODEL)), _const_spec((D_MODEL, D_MODEL))],
        out_specs=tok(D_MODEL),
        out_shape=jax.ShapeDtypeStruct(x.shape, x.dtype),
        scratch_shapes=[pltpu.VMEM((S5_LANE_GROUPS, tm, LANES), F32)],
        compiler_params=pltpu.CompilerParams(
            dimension_semantics=("arbitrary", "arbitrary"), vmem_limit_bytes=VMEM_LIMIT),
        name="final",
    )(x, yfox, yc, mk, mv, gn, gfin, wg, bm, wglu, bglu, wpf, wps, wpm, wout)


def _layer(x, mem, g_norm, g_mem_norm, w_in, b_forget, b_merge, w_mem_kv, lam_re, lam_im, log_step,
           s5_b_re, s5_b_im, s5_c_re, s5_c_im, s5_d, w_glu, b_glu, w_proj_fox, w_proj_s5, w_proj_mem,
           w_out, g_out, *, tm_in, t_attn, tm_out):
    bn, l, _ = x.shape
    assert tm_in == t_attn, "the per-tile q norms from in_proj are indexed by attention query tile"
    row = lambda a: a.reshape(1, -1).astype(F32)
    wqk, wvt, wfl_pad, wrest = _weight_prep(w_in)
    bfl_pad = jnp.pad(row(b_forget), ((0, 0), (0, LANES - FOX_HEADS)))

    mk, mv = _mem_kv(mem, row(g_mem_norm), w_mem_kv.astype(BF16))
    q, k, vt, uc, fqa, fka, fend, nq, nk = _in_proj(x, row(g_norm), wqk, wrest, wvt, wfl_pad, bfl_pad, tm=tm_in)
    yfox = _fox(q, fqa, k, fka, vt, *_fox_skip_tables(fend, nq, nk), t=t_attn)
    tables = _s5_tables(lam_re, lam_im, log_step, s5_b_re, s5_b_im, s5_c_re, s5_c_im, s5_d)
    yc = _s5(uc, *tables)
    return _final(x, yfox, yc, mk, mv, row(g_norm), row(g_out), wrest, row(b_merge),
                  w_glu.astype(BF16), row(b_glu), w_proj_fox.astype(BF16), w_proj_s5.astype(BF16),
                  w_proj_mem.astype(BF16), w_out.astype(BF16), tm=tm_out)


def kernel(x, mem, g_norm, g_mem_norm, g_final, w_in, b_forget, b_merge, w_mem_kv, lam_re, lam_im, log_step,
           s5_b_re, s5_b_im, s5_c_re, s5_c_im, s5_d, w_glu, b_glu, w_proj_fox, w_proj_s5, w_proj_mem, w_out):
    depth = w_in.shape[0]
    assert depth == 1, "the fused final kernel applies the closing RMSNorm: one layer only"
    l = x.shape[1]
    tm_in = min(512, l)
    t_attn = min(512, l)
    tm_out = min(512, l)
    return _layer(x, mem, g_norm[0], g_mem_norm[0], w_in, b_forget[0], b_merge[0], w_mem_kv[0],
                  lam_re[0], lam_im[0], log_step[0], s5_b_re[0], s5_b_im[0], s5_c_re[0], s5_c_im[0],
                  s5_d[0], w_glu[0], b_glu[0], w_proj_fox[0], w_proj_s5[0], w_proj_mem[0], w_out[0],
                  g_final, tm_in=tm_in, t_attn=t_attn, tm_out=tm_out)
```

```python
import functools
import math

import jax
import jax.numpy as jnp
import numpy as np
from jax import lax
from jax.experimental import pallas as pl
from jax.experimental.pallas import tpu as pltpu

F32 = jnp.float32
BF16 = jnp.bfloat16

D_MODEL = 1024
EPS = 1e-6
NEG = -1e30
LOG2E = math.log2(math.e)
SKIP_LOG2 = 127.0
NORM_MARGIN = 1.03

FOX_HEAD_DIM = 64
D_FOX = 768
FOX_HEADS = D_FOX // FOX_HEAD_DIM

D_S5 = 768
S5_GROUP = 16
S5_GROUPS = D_S5 // S5_GROUP
S5_STATE = 64

D_MEM = 512
MEM_HEADS = 4
MEM_HEAD_DIM = D_MEM // MEM_HEADS

LANES = 128
S5_CHUNK = 16
S5_LANE_GROUPS = D_S5 // LANES
S5_GROUPS_PER_LANE_GROUP = LANES // S5_GROUP
S5_CW = S5_CHUNK * LANES
S5_HALF = S5_GROUPS_PER_LANE_GROUP * S5_STATE
S5_SW = 2 * S5_HALF
S5_SEGMENTS = 8

W_FL = 3 * D_FOX
W_REST = W_FL + FOX_HEADS
R_GF, R_U, R_GS, R_QM, R_GM, R_GL = 0, 768, 1536, 2304, 2816, 3328
R_COLS = R_GL + 3 * D_MODEL

VMEM_LIMIT = 56 * 1024 * 1024


def _dot(a, b, **kw):
    return jnp.dot(a, b, preferred_element_type=F32, **kw)


def _dot_nt(a, b, **kw):
    return lax.dot_general(a, b, (((1,), (1,)), ((), ())), preferred_element_type=F32, **kw)


def _dot_nt_split(a, b):
    a_hi, b_hi = a.astype(BF16), b.astype(BF16)
    a_lo = (a - a_hi.astype(F32)).astype(BF16)
    b_lo = (b - b_hi.astype(F32)).astype(BF16)
    return _dot_nt(a_hi, b_hi) + (_dot_nt(a_hi, b_lo) + _dot_nt(a_lo, b_hi))


def _rms(xf, g):
    return xf * lax.rsqrt(jnp.mean(xf * xf, axis=-1, keepdims=True) + EPS) * g


def _sigmoid(z):
    return 1.0 / (1.0 + jnp.exp(-z))


def _silu(z):
    return z * _sigmoid(z)


def _gelu_tanh(y):
    return 0.5 * y * (1.0 + jnp.tanh(math.sqrt(2.0 / math.pi) * (y + 0.044715 * (y * y * y))))


def _split3(a):
    hi = a.astype(BF16)
    rest = a - hi.astype(F32)
    mid = rest.astype(BF16)
    lo = (rest - mid.astype(F32)).astype(BF16)
    return jnp.concatenate([hi, mid, lo], axis=-1)


def _const_spec(shape):
    nd = len(shape)
    return pl.BlockSpec(shape, lambda *_: (0,) * nd, pipeline_mode=pl.Buffered(1))


def _columns_kernel(wt_ref, o_ref, *, valid):
    cols = wt_ref[:, 0, :].T
    if valid < LANES:
        cols = jnp.where(lax.broadcasted_iota(jnp.int32, (1, LANES), 1) < valid, cols, 0.0)
    o_ref[...] = cols.astype(BF16)


def _rows_kernel(wt_ref, o_ref):
    o_ref[...] = wt_ref[:, 0, :].astype(BF16)


def _weight_prep(w_in):
    depth, d, n_in = w_in.shape
    assert depth == 1 and n_in == W_REST + R_COLS
    wt = jnp.transpose(w_in, (2, 0, 1))

    def columns(first, count, valid=LANES, name="weight_cols"):
        blocks = pl.cdiv(count, LANES)
        return pl.pallas_call(
            functools.partial(_columns_kernel, valid=valid),
            grid=(blocks,),
            in_specs=[pl.BlockSpec((pl.Element(LANES), pl.Element(1), pl.Element(d)),
                                   lambda c: (first + LANES * c, 0, 0))],
            out_specs=pl.BlockSpec((d, LANES), lambda c: (0, c)),
            out_shape=jax.ShapeDtypeStruct((d, blocks * LANES), BF16),
            name=name,
        )(wt)

    wqk = columns(0, 2 * D_FOX, name="weight_qk")
    wfl = columns(W_FL, LANES, valid=FOX_HEADS, name="weight_forget")
    wrest = columns(W_REST, R_COLS, name="weight_rest")
    wvt = pl.pallas_call(
        _rows_kernel,
        grid=(D_FOX // LANES,),
        in_specs=[pl.BlockSpec((LANES, 1, d), lambda r: (2 * D_FOX // LANES + r, 0, 0))],
        out_specs=pl.BlockSpec((LANES, d), lambda r: (r, 0)),
        out_shape=jax.ShapeDtypeStruct((D_FOX, d), BF16),
        name="weight_vt",
    )(wt)
    return wqk, wvt, wfl, wrest


def _mem_kv_kernel(mem_ref, g_ref, w_ref, mk_ref, mv_ref):
    h = _rms(mem_ref[...], g_ref[...]).astype(BF16)
    kv = _dot(h, w_ref[...])
    mk_ref[...] = kv[:, :D_MEM].astype(BF16)
    mv_ref[...] = kv[:, D_MEM:].astype(BF16)


def _mem_kv(mem, g, w):
    bn, m, _ = mem.shape
    out = jax.ShapeDtypeStruct((bn, m, D_MEM), BF16)
    return pl.pallas_call(
        _mem_kv_kernel,
        grid=(bn,),
        in_specs=[pl.BlockSpec((None, m, D_MODEL), lambda b: (b, 0, 0)),
                  _const_spec((1, D_MODEL)),
                  _const_spec((D_MODEL, 2 * D_MEM))],
        out_specs=[pl.BlockSpec((None, m, D_MEM), lambda b: (b, 0, 0))] * 2,
        out_shape=[out, out],
        name="mem_kv",
    )(mem, g, w)


def _in_proj_kernel(x_ref, g_ref, w_ref, wu_ref, wvt_ref, wfl_ref, bfl_ref, spread_ref, ones_ref,
                    q_ref, k_ref, vt_ref, uc_ref, fqa_ref, fka_ref, fend_ref, nq_ref, nk_ref,
                    us_scr, carry_scr, *, tm):
    @pl.when(pl.program_id(1) == 0)
    def _():
        carry_scr[...] = jnp.zeros_like(carry_scr)

    h = _rms(x_ref[...], g_ref[...]).astype(BF16)
    qb = (_dot(h, w_ref[:, 0:D_FOX]) * (LOG2E * FOX_HEAD_DIM ** -0.5)).astype(BF16)
    kb = _dot(h, w_ref[:, D_FOX:2 * D_FOX]).astype(BF16)
    q_ref[...] = qb
    k_ref[...] = kb
    for pair in range(FOX_HEADS // 2):
        vt_ref[pair] = _dot_nt(wvt_ref[pair * LANES:(pair + 1) * LANES, :], h).astype(BF16)

    head_of_col = lax.broadcasted_iota(jnp.int32, (D_FOX, LANES), 0) // FOX_HEAD_DIM
    sel = (head_of_col == lax.broadcasted_iota(jnp.int32, (D_FOX, LANES), 1)).astype(BF16)
    sq = lambda a: (a.astype(F32) * a.astype(F32)).astype(BF16)
    nq_ref[...] = jnp.max(_dot(sq(qb), sel), axis=0, keepdims=True)
    nk_ref[...] = jnp.max(_dot(sq(kb), sel), axis=0, keepdims=True)

    u = _dot(h, wu_ref[...])
    nc = tm // S5_CHUNK
    for lg in range(S5_LANE_GROUPS):
        us_scr[lg] = u[:, lg * LANES:(lg + 1) * LANES]
        for j in range(S5_CHUNK):
            c0 = lg * S5_CW + j * LANES
            uc_ref[:, c0:c0 + LANES] = us_scr[lg, pl.ds(j, nc, stride=S5_CHUNK), :].astype(BF16)

    z = _dot(h, wfl_ref[...]) + bfl_ref[...]
    logf = (jnp.minimum(z, 0.0) - jnp.log1p(jnp.exp(-jnp.abs(z)))) * LOG2E
    row = lax.broadcasted_iota(jnp.int32, (tm, tm), 0)
    col = lax.broadcasted_iota(jnp.int32, (tm, tm), 1)
    tri = (col <= row).astype(BF16)
    parts = _dot(tri, _split3(logf))
    f = (parts[:, 0:LANES] + parts[:, LANES:2 * LANES]) + parts[:, 2 * LANES:] + carry_scr[...]
    carry_scr[...] = f[tm - 1:tm, :]
    fend_ref[...] = f[tm - 1:tm, :]
    aug = _dot(_split3(f), spread_ref[...]) + ones_ref[...]
    fka_ref[...] = aug[:, 0:LANES].astype(BF16)
    fqa_ref[...] = aug[:, LANES:].astype(BF16)


def _fox_aug_constants():
    spread = np.zeros((3 * LANES, 2 * LANES), np.float32)
    ones = np.zeros((1, 2 * LANES), np.float32)
    half = LANES // 2
    for h in range(FOX_HEADS):
        for i in range(3):
            spread[i * LANES + h, 3 * h + i] = 1.0
            spread[i * LANES + h, LANES + half + 3 * h + i] = 1.0
            ones[0, half + 3 * h + i] = 1.0
            ones[0, LANES + 3 * h + i] = -1.0
    return jnp.asarray(spread, BF16), jnp.asarray(ones, F32)


def _in_proj(x, g, wqk, wrest, wvt, wfl, bfl, *, tm):
    bn, l, _ = x.shape
    nc = tm // S5_CHUNK
    nt = l // tm
    pairs = FOX_HEADS // 2
    spread, ones = _fox_aug_constants()
    tok = lambda width: pl.BlockSpec((None, tm, width), lambda b, i: (b, i, 0))
    per_tile = pl.BlockSpec((None, None, 1, LANES), lambda b, i: (b, i, 0, 0))
    qk = jax.ShapeDtypeStruct((bn, l, D_FOX), BF16)
    aug = jax.ShapeDtypeStruct((bn, l, LANES), BF16)
    stat = jax.ShapeDtypeStruct((bn, nt, 1, LANES), F32)
    return pl.pallas_call(
        functools.partial(_in_proj_kernel, tm=tm),
        grid=(bn, nt),
        in_specs=[tok(D_MODEL),
                  _const_spec((1, D_MODEL)),
                  _const_spec((D_MODEL, 2 * D_FOX)),
                  pl.BlockSpec((D_MODEL, D_S5), lambda *_: (0, R_U // D_S5), pipeline_mode=pl.Buffered(1)),
                  _const_spec((D_FOX, D_MODEL)),
                  _const_spec((D_MODEL, LANES)),
                  _const_spec((1, LANES)),
                  _const_spec((3 * LANES, 2 * LANES)),
                  _const_spec((1, 2 * LANES))],
        out_specs=[tok(D_FOX), tok(D_FOX),
                   pl.BlockSpec((None, pairs, None, LANES, tm), lambda b, i: (b, 0, i, 0, 0)),
                   pl.BlockSpec((None, nc, S5_LANE_GROUPS * S5_CW), lambda b, i: (b, i, 0)),
                   tok(LANES), tok(LANES), per_tile, per_tile, per_tile],
        out_shape=[qk, qk,
                   jax.ShapeDtypeStruct((bn, pairs, nt, LANES, tm), BF16),
                   jax.ShapeDtypeStruct((bn, l // S5_CHUNK, S5_LANE_GROUPS * S5_CW), BF16),
                   aug, aug, stat, stat, stat],
        scratch_shapes=[pltpu.VMEM((S5_LANE_GROUPS, tm, LANES), F32), pltpu.VMEM((1, LANES), F32)],
        compiler_params=pltpu.CompilerParams(
            dimension_semantics=("arbitrary", "arbitrary"), vmem_limit_bytes=VMEM_LIMIT),
        name="in_proj",
    )(x, g, wqk, wrest, wvt, wfl, bfl, spread, ones)


def _fox_kernel(fend_ref, qkb_ref, q_ref, fqa_ref, k_ref, fka_ref, vt_ref, o_ref,
                m_scr, l_scr, alpha_scr, acc_scr, s_scr, p_scr, *, t, nt):
    b = pl.program_id(0)
    pair = pl.program_id(1)
    qi = pl.program_id(2)
    lane = lax.broadcasted_iota(jnp.int32, (1, LANES), 1)
    q = q_ref[...]
    fqa = fqa_ref[...]
    key_pos = lax.broadcasted_iota(jnp.int32, (t, t), 0)
    query_pos = lax.broadcasted_iota(jnp.int32, (t, t), 1)

    heads = (0, 1)
    q_aug, base = [], []
    for e in heads:
        head = 2 * pair + e
        head_lanes = (lane >= e * FOX_HEAD_DIM) & (lane < (e + 1) * FOX_HEAD_DIM)
        f_lanes = lax.rem(lane, LANES // 2) // 3 == head
        q_aug.append(jnp.concatenate([jnp.where(head_lanes, q, jnp.zeros_like(q)),
                                      jnp.where(f_lanes, fqa, jnp.zeros_like(fqa))], axis=-1))
        base.append((b * FOX_HEADS + head) * nt)

    def scores(e, j):
        k0 = pl.multiple_of(jnp.maximum(j, 0) * t, t)
        k_aug = jnp.concatenate([k_ref[pl.ds(k0, t), :], fka_ref[pl.ds(k0, t), :]], axis=-1)
        return _dot_nt(k_aug, q_aug[e])

    def weighted_values(e, j, slot):
        acc_scr[e] = alpha_scr[e, slot] * acc_scr[e] + _dot(vt_ref[j], p_scr[e, slot])

    def softmax(e, s_slot, p_slot):
        s = s_scr[e, s_slot]
        m_old = m_scr[e]
        m_new = jnp.maximum(m_old, jnp.max(s, axis=0, keepdims=True))
        p = jnp.exp2(s - m_new)
        alpha = jnp.exp2(m_old - m_new)
        m_scr[e] = m_new
        l_scr[e] = alpha * l_scr[e] + jnp.sum(p, axis=0, keepdims=True)
        p_scr[e, p_slot] = p.astype(BF16)
        alpha_scr[e, p_slot] = alpha

    slack = []
    for e in heads:
        s = jnp.where(key_pos <= query_pos, scores(e, qi), NEG)
        m = jnp.max(s, axis=0, keepdims=True)
        p = jnp.exp2(s - m)
        m_scr[e] = m
        l_scr[e] = jnp.sum(p, axis=0, keepdims=True)
        p_scr[e, 0] = p.astype(BF16)
        alpha_scr[e, 0] = jnp.zeros_like(m)
        acc_scr[e] = jnp.zeros((LANES, t), F32)
        slack.append(qkb_ref[base[e] + qi] - jnp.min(m) + SKIP_LOG2)
        s_scr[e, 0] = scores(e, qi - 1)

    def wanted(e, j):
        f_hi = fend_ref[base[e] + jnp.maximum(qi - 1, 0)]
        reach = f_hi - fend_ref[base[e] + jnp.maximum(j, 0)] + slack[e]
        return jnp.logical_and(j >= 0, reach >= 0.0).astype(jnp.int32)

    def trip(j, which, tiles):
        assert tiles == 1 or tiles % 2 == 0
        for i in range(tiles):
            here, there = i % 2, (i + 1) % 2 if tiles > 1 else 0
            for e in which:
                weighted_values(e, j + 1 - i, here)
            for e in which:
                softmax(e, here, there)
            for e in which:
                s_scr[e, there] = scores(e, j - 1 - i)

    def walk(j, which, tiles):
        def go(jj):
            flag = wanted(which[0], jj - (tiles - 1))
            for e in which[1:]:
                flag = flag * wanted(e, jj - (tiles - 1))
            return flag

        def body(carry):
            trip(carry[0], which, tiles)
            return carry[0] - tiles, go(carry[0] - tiles)

        return lax.while_loop(lambda c: c[1] > 0, body, (j, go(j)))[0]

    j_both = walk(qi - 1, heads, 1)
    outs = []
    for e in heads:
        j_stop = walk(walk(j_both, (e,), 2), (e,), 1)
        weighted_values(e, j_stop + 1, 0)
        outs.append(acc_scr[e] * (1.0 / l_scr[e]))
    dim = lax.broadcasted_iota(jnp.int32, (LANES, 1), 0)
    o_ref[...] = jnp.where(dim < FOX_HEAD_DIM, outs[0], outs[1]).T.astype(BF16)


def _fox(q, fqa, k, fka, vt, fend, qkb, *, t):
    bn, l, _ = q.shape
    pairs = FOX_HEADS // 2
    nt = l // t
    query_tile = pl.BlockSpec((None, t, LANES), lambda b, p, i, *_: (b, i, p))
    grid_spec = pltpu.PrefetchScalarGridSpec(
        num_scalar_prefetch=2,
        grid=(bn, pairs, nt),
        in_specs=[query_tile,
                  pl.BlockSpec((None, t, LANES), lambda b, p, i, *_: (b, i, 0)),
                  pl.BlockSpec((None, l, LANES), lambda b, p, i, *_: (b, 0, p)),
                  pl.BlockSpec((None, l, LANES), lambda b, p, i, *_: (b, 0, 0)),
                  pl.BlockSpec((None, None, nt, LANES, t), lambda b, p, i, *_: (b, p, 0, 0, 0))],
        out_specs=query_tile,
        scratch_shapes=[pltpu.VMEM((2, 1, t), F32), pltpu.VMEM((2, 1, t), F32), pltpu.VMEM((2, 2, 1, t), F32),
                        pltpu.VMEM((2, LANES, t), F32), pltpu.VMEM((2, 2, t, t), F32),
                        pltpu.VMEM((2, 2, t, t), BF16)])
    return pl.pallas_call(
        functools.partial(_fox_kernel, t=t, nt=nt),
        grid_spec=grid_spec,
        out_shape=jax.ShapeDtypeStruct((bn, l, D_FOX), BF16),
        compiler_params=pltpu.CompilerParams(
            dimension_semantics=("arbitrary", "arbitrary", "arbitrary"), vmem_limit_bytes=VMEM_LIMIT),
        name="fox",
    )(fend, qkb, q, fqa, k, fka, vt)


def _fox_skip_tables(fend, nq, nk):
    per_head = lambda a: jnp.swapaxes(a[:, :, 0, :FOX_HEADS], 1, 2)
    qmax = jnp.sqrt(per_head(nq))
    kmax = jnp.sqrt(jnp.max(per_head(nk), axis=2, keepdims=True))
    return per_head(fend).reshape(-1), (NORM_MARGIN * qmax * kmax).reshape(-1)


def _s5_tables_kernel(lr_ref, li_ref, ls_ref, btr_ref, bti_ref, cr_ref, ci_ref, d_ref,
                      wtoe_ref, win_ref, voutt_ref, lam_ref):
    lr, li = lr_ref[...], li_ref[...]
    step = jnp.exp(ls_ref[...])
    mag = jnp.exp(lr * step)
    ab_re, ab_im = mag * jnp.cos(li * step), mag * jnp.sin(li * step)
    den = lr * lr + li * li
    nr, ni = ab_re - 1.0, ab_im
    f_re = (nr * lr + ni * li) / den
    f_im = (ni * lr - nr * li) / den
    btr, bti = btr_ref[...], bti_ref[...]
    bb_re = f_re * btr - f_im * bti
    bb_im = f_re * bti + f_im * btr
    cr, ci = cr_ref[...], ci_ref[...]
    eye = (lax.broadcasted_iota(jnp.int32, (LANES, LANES), 0)
           == lax.broadcasted_iota(jnp.int32, (LANES, LANES), 1))
    tile = lambda j: slice(j * LANES, (j + 1) * LANES)
    zeros = jnp.zeros((LANES, LANES), BF16)
    for jp in range(S5_CHUNK):
        for j in range(jp):
            wtoe_ref[tile(jp), tile(j)] = zeros
    pr = jnp.ones_like(lr)
    pi = jnp.zeros_like(lr)
    for tau in range(S5_CHUNK + 1):
        a_re = cr * pr - ci * pi
        a_im = cr * pi + ci * pr
        if tau < S5_CHUNK:
            kt = _dot_nt_split(bb_re, a_re) - _dot_nt_split(bb_im, a_im)
            if tau == 0:
                kt = kt + jnp.where(eye, d_ref[...], 0.0)
            kt = kt.astype(BF16)
            for jp in range(S5_CHUNK - tau):
                wtoe_ref[tile(jp), tile(jp + tau)] = kt
            j = S5_CHUNK - 1 - tau
            win_ref[tile(j), 0:S5_HALF] = (pr * bb_re - pi * bb_im).astype(BF16)
            win_ref[tile(j), S5_HALF:] = (pr * bb_im + pi * bb_re).astype(BF16)
        if tau >= 1:
            voutt_ref[tile(tau - 1), 0:S5_HALF] = a_re.astype(BF16)
            voutt_ref[tile(tau - 1), S5_HALF:] = (-a_im).astype(BF16)
        if tau == S5_CHUNK:
            lam_ref[:, 0:S5_HALF] = pr
            lam_ref[:, S5_HALF:] = pi
        pr, pi = pr * ab_re - pi * ab_im, pr * ab_im + pi * ab_re


def _s5_tables(lam_re, lam_im, log_step, b_re, b_im, c_re, c_im, d):
    lg, gp = S5_LANE_GROUPS, S5_GROUPS_PER_LANE_GROUP
    eye = jnp.eye(gp, dtype=F32)

    def block_diag(a):
        a = a.astype(F32).reshape(lg, gp, S5_GROUP, S5_STATE)
        return jnp.einsum("Ggcp,gh->Ggchp", a, eye).reshape(lg, LANES, S5_HALF)

    lanes = lambda a: a.astype(F32).reshape(lg, 1, S5_HALF)
    step = jnp.broadcast_to(log_step.astype(F32)[:, None], (S5_GROUPS, S5_STATE))
    spec = lambda r, c: pl.BlockSpec((None, r, c), lambda g: (g, 0, 0))
    return pl.pallas_call(
        _s5_tables_kernel,
        grid=(lg,),
        in_specs=[spec(1, S5_HALF)] * 3 + [spec(LANES, S5_HALF)] * 4 + [spec(1, LANES)],
        out_specs=[spec(S5_CW, S5_CW), spec(S5_CW, S5_SW), spec(S5_CW, S5_SW), spec(1, S5_SW)],
        out_shape=[jax.ShapeDtypeStruct((lg, S5_CW, S5_CW), BF16),
                   jax.ShapeDtypeStruct((lg, S5_CW, S5_SW), BF16),
                   jax.ShapeDtypeStruct((lg, S5_CW, S5_SW), BF16),
                   jax.ShapeDtypeStruct((lg, 1, S5_SW), F32)],
        compiler_params=pltpu.CompilerParams(
            dimension_semantics=("arbitrary",), vmem_limit_bytes=VMEM_LIMIT),
        name="s5_tables",
    )(lanes(lam_re), lanes(lam_im), lanes(step),
      block_diag(jnp.swapaxes(b_re, 1, 2)), block_diag(jnp.swapaxes(b_im, 1, 2)),
      block_diag(c_re), block_diag(c_im), d.astype(F32).reshape(lg, 1, LANES))


def _s5_kernel(uc_ref, wtoe_ref, win_ref, voutt_ref, lam_ref, yc_ref, e_scr, hs_scr, *, n):
    x = uc_ref[...]
    e = _dot(x, win_ref[...])
    tiles = S5_SW // LANES
    half = tiles // 2
    lane_tile = lambda c: slice(c * LANES, (c + 1) * LANES)
    for c in range(tiles):
        e_scr[c, 0:n, :] = e[:, lane_tile(c)]
        e_scr[c, n:, :] = jnp.zeros((S5_SEGMENTS, LANES), F32)
    lam = jnp.broadcast_to(lam_ref[...], (S5_SEGMENTS, S5_SW))
    lam_r = [lam[:, lane_tile(c)] for c in range(half)]
    lam_i = [lam[:, lane_tile(half + c)] for c in range(half)]
    cmul = lambda ar, ai, br, bi: (ar * br - ai * bi, ar * bi + ai * br)

    seg = n // S5_SEGMENTS + 1
    run_rows = lambda r: pl.ds(r, S5_SEGMENTS, stride=seg)
    zero = jnp.zeros((S5_SEGMENTS, LANES), F32)
    one = jnp.ones((S5_SEGMENTS, LANES), F32)
    zero_row = jnp.zeros((1, LANES), F32)

    def scan_runs(r, carry):
        h, pw = carry
        h_next, pw_next = [None] * tiles, [None] * tiles
        for c in range(half):
            hs_scr[c, run_rows(r), :] = h[c]
            hs_scr[half + c, run_rows(r), :] = h[half + c]
            nr, ni = cmul(lam_r[c], lam_i[c], h[c], h[half + c])
            h_next[c] = nr + e_scr[c, run_rows(r), :]
            h_next[half + c] = ni + e_scr[half + c, run_rows(r), :]
            pw_next[c], pw_next[half + c] = cmul(lam_r[c], lam_i[c], pw[c], pw[half + c])
        return tuple(h_next), tuple(pw_next)

    unit = (one,) * half + (zero,) * half
    run_end, run_mult = lax.fori_loop(0, seg, scan_runs, ((zero,) * tiles, unit))

    start = [None] * tiles
    for c in range(half):
        sr, si, rows_r, rows_i = zero_row, zero_row, [], []
        for s in range(S5_SEGMENTS):
            rows_r.append(sr)
            rows_i.append(si)
            nr, ni = cmul(run_mult[c][0:1], run_mult[half + c][0:1], sr, si)
            sr, si = nr + run_end[c][s:s + 1], ni + run_end[half + c][s:s + 1]
        start[c], start[half + c] = jnp.concatenate(rows_r, axis=0), jnp.concatenate(rows_i, axis=0)

    def add_run_starts(r, pw):
        pw_next = [None] * tiles
        for c in range(half):
            ar, ai = cmul(pw[c], pw[half + c], start[c], start[half + c])
            hs_scr[c, run_rows(r), :] = hs_scr[c, run_rows(r), :] + ar
            hs_scr[half + c, run_rows(r), :] = hs_scr[half + c, run_rows(r), :] + ai
            pw_next[c], pw_next[half + c] = cmul(lam_r[c], lam_i[c], pw[c], pw[half + c])
        return tuple(pw_next)

    lax.fori_loop(0, seg, add_run_starts, unit)
    hs = jnp.concatenate([hs_scr[c, 0:n, :] for c in range(tiles)], axis=-1).astype(BF16)
    blk = 2 * LANES
    for jb in range(S5_CW // blk):
        cols = slice(jb * blk, (jb + 1) * blk)
        kk = (jb + 1) * blk
        y = _dot(x[:, :kk], wtoe_ref[0:kk, cols]) + _dot_nt(hs, voutt_ref[cols, :])
        yc_ref[:, cols] = y.astype(BF16)


def _s5(uc, wtoe, win, voutt, lam):
    bn, n, _ = uc.shape
    per_lg = lambda r, c: pl.BlockSpec((None, r, c), lambda g, b: (g, 0, 0))
    return pl.pallas_call(
        functools.partial(_s5_kernel, n=n),
        grid=(S5_LANE_GROUPS, bn),
        in_specs=[pl.BlockSpec((None, n, S5_CW), lambda g, b: (b, 0, g)),
                  per_lg(S5_CW, S5_CW), per_lg(S5_CW, S5_SW), per_lg(S5_CW, S5_SW), per_lg(1, S5_SW)],
        out_specs=pl.BlockSpec((None, n, S5_CW), lambda g, b: (b, 0, g)),
        out_shape=jax.ShapeDtypeStruct(uc.shape, BF16),
        scratch_shapes=[pltpu.VMEM((S5_SW // LANES, n + S5_SEGMENTS, LANES), F32)] * 2,
        compiler_params=pltpu.CompilerParams(
            dimension_semantics=("arbitrary", "arbitrary"), vmem_limit_bytes=VMEM_LIMIT),
        name="s5",
    )(uc, wtoe, win, voutt, lam)


def _final_kernel(x_ref, yfox_ref, yc_ref, mk_ref, mv_ref, gn_ref, gfin_ref, wg_ref, bm_ref,
                  wglu_ref, bglu_ref, wpf_ref, wps_ref, wpm_ref, wout_ref, o_ref, ys_scr, *, tm):
    x = x_ref[...]
    h = _rms(x, gn_ref[...]).astype(BF16)

    def proj(c0, width):
        return _dot(h, wg_ref[:, c0:c0 + width])

    def gate(i):
        return _sigmoid(proj(R_GL + i * D_MODEL, D_MODEL) + bm_ref[:, i * D_MODEL:(i + 1) * D_MODEL])

    a = (yfox_ref[...].astype(F32) * _silu(proj(R_GF, D_FOX))).astype(BF16)
    merged = gate(0) * _dot(a, wpf_ref[...])

    nc = tm // S5_CHUNK
    for lg in range(S5_LANE_GROUPS):
        for j in range(S5_CHUNK):
            c0 = lg * S5_CW + j * LANES
            ys_scr[lg, pl.ds(j, nc, stride=S5_CHUNK), :] = yc_ref[:, c0:c0 + LANES].astype(F32)
    y = _gelu_tanh(jnp.concatenate([ys_scr[lg] for lg in range(S5_LANE_GROUPS)], axis=-1))
    y = y * _sigmoid(_dot(y.astype(BF16), wglu_ref[...]) + bglu_ref[...])
    y = (y * _silu(proj(R_GS, D_S5))).astype(BF16)
    merged = merged + gate(1) * _dot(y, wps_ref[...])

    qm = (proj(R_QM, D_MEM) * (MEM_HEAD_DIM ** -0.5)).astype(BF16)
    heads = []
    for hd in range(MEM_HEADS):
        cols = slice(hd * MEM_HEAD_DIM, (hd + 1) * MEM_HEAD_DIM)
        s = _dot_nt(qm[:, cols], mk_ref[:, cols])
        p = jnp.exp(s - jnp.max(s, axis=-1, keepdims=True))
        inv = 1.0 / jnp.sum(p, axis=-1, keepdims=True)
        heads.append(_dot(p.astype(BF16), mv_ref[:, cols]) * inv)
    ymem = jnp.concatenate(heads, axis=-1)
    a = (ymem * _silu(proj(R_GM, D_MEM))).astype(BF16)
    merged = merged + gate(2) * _dot(a, wpm_ref[...])

    out = x + _dot(merged.astype(BF16), wout_ref[...])
    o_ref[...] = _rms(out, gfin_ref[...])


def _final(x, yfox, yc, mk, mv, gn, gfin, wg, bm, wglu, bglu, wpf, wps, wpm, wout, *, tm):
    bn, l, _ = x.shape
    m = mk.shape[1]
    nc = tm // S5_CHUNK
    tok = lambda width: pl.BlockSpec((None, tm, width), lambda b, i: (b, i, 0))
    mem = pl.BlockSpec((None, m, D_MEM), lambda b, i: (b, 0, 0))
    return pl.pallas_call(
        functools.partial(_final_kernel, tm=tm),
        grid=(bn, l // tm),
        in_specs=[tok(D_MODEL), tok(D_FOX),
                  pl.BlockSpec((None, nc, S5_LANE_GROUPS * S5_CW), lambda b, i: (b, i, 0)),
                  mem, mem,
                  _const_spec((1, D_MODEL)), _const_spec((1, D_MODEL)),
                  _const_spec((D_MODEL, R_COLS)), _const_spec((1, 3 * D_MODEL)),
                  _const_spec((D_S5, D_S5)), _const_spec((1, D_S5)),
                  _const_spec((D_FOX, D_MODEL)), _const_spec((D_S5, D_MODEL)),
                  _const_spec((D_MEM, D_MODEL)), _const_spec((D_MODEL, D_MODEL))],
        out_specs=tok(D_MODEL),
        out_shape=jax.ShapeDtypeStruct(x.shape, x.dtype),
        scratch_shapes=[pltpu.VMEM((S5_LANE_GROUPS, tm, LANES), F32)],
        compiler_params=pltpu.CompilerParams(
            dimension_semantics=("arbitrary", "arbitrary"), vmem_limit_bytes=VMEM_LIMIT),
        name="final",
    )(x, yfox, yc, mk, mv, gn, gfin, wg, bm, wglu, bglu, wpf, wps, wpm, wout)


def _layer(x, mem, g_norm, g_mem_norm, w_in, b_forget, b_merge, w_mem_kv, lam_re, lam_im, log_step,
           s5_b_re, s5_b_im, s5_c_re, s5_c_im, s5_d, w_glu, b_glu, w_proj_fox, w_proj_s5, w_proj_mem,
           w_out, g_out, *, tm_in, t_attn, tm_out):
    bn, l, _ = x.shape
    assert tm_in == t_attn, "the per-tile q norms from in_proj are indexed by attention query tile"
    row = lambda a: a.reshape(1, -1).astype(F32)
    wqk, wvt, wfl_pad, wrest = _weight_prep(w_in)
    bfl_pad = jnp.pad(row(b_forget), ((0, 0), (0, LANES - FOX_HEADS)))

    mk, mv = _mem_kv(mem, row(g_mem_norm), w_mem_kv.astype(BF16))
    q, k, vt, uc, fqa, fka, fend, nq, nk = _in_proj(x, row(g_norm), wqk, wrest, wvt, wfl_pad, bfl_pad, tm=tm_in)
    yfox = _fox(q, fqa, k, fka, vt, *_fox_skip_tables(fend, nq, nk), t=t_attn)
    tables = _s5_tables(lam_re, lam_im, log_step, s5_b_re, s5_b_im, s5_c_re, s5_c_im, s5_d)
    yc = _s5(uc, *tables)
    return _final(x, yfox, yc, mk, mv, row(g_norm), row(g_out), wrest, row(b_merge),
                  w_glu.astype(BF16), row(b_glu), w_proj_fox.astype(BF16), w_proj_s5.astype(BF16),
                  w_proj_mem.astype(BF16), w_out.astype(BF16), tm=tm_out)


def kernel(x, mem, g_norm, g_mem_norm, g_final, w_in, b_forget, b_merge, w_mem_kv, lam_re, lam_im, log_step,
           s5_b_re, s5_b_im, s5_c_re, s5_c_im, s5_d, w_glu, b_glu, w_proj_fox, w_proj_s5, w_proj_mem, w_out):
    depth = w_in.shape[0]
    assert depth == 1, "the fused final kernel applies the closing RMSNorm: one layer only"
    l = x.shape[1]
    tm_in = min(512, l)
    t_attn = min(512, l)
    tm_out = min(512, l)
    return _layer(x, mem, g_norm[0], g_mem_norm[0], w_in, b_forget[0], b_merge[0], w_mem_kv[0],
                  lam_re[0], lam_im[0], log_step[0], s5_b_re[0], s5_b_im[0], s5_c_re[0], s5_c_im[0],
                  s5_d[0], w_glu[0], b_glu[0], w_proj_fox[0], w_proj_s5[0], w_proj_mem[0], w_out[0],
                  g_final, tm_in=tm_in, t_attn=t_attn, tm_out=tm_out)
```

```python
import functools
import math

import jax
import jax.numpy as jnp
import numpy as np
from jax import lax
from jax.experimental import pallas as pl
from jax.experimental.pallas import tpu as pltpu

F32 = jnp.float32
BF16 = jnp.bfloat16

D_MODEL = 1024
EPS = 1e-6
NEG = -1e30
LOG2E = math.log2(math.e)
SKIP_LOG2 = 127.0
NORM_MARGIN = 1.03

FOX_HEAD_DIM = 64
D_FOX = 768
FOX_HEADS = D_FOX // FOX_HEAD_DIM

D_S5 = 768
S5_GROUP = 16
S5_GROUPS = D_S5 // S5_GROUP
S5_STATE = 64

D_MEM = 512
MEM_HEADS = 4
MEM_HEAD_DIM = D_MEM // MEM_HEADS

LANES = 128
S5_CHUNK = 16
S5_LANE_GROUPS = D_S5 // LANES
S5_GROUPS_PER_LANE_GROUP = LANES // S5_GROUP
S5_CW = S5_CHUNK * LANES
S5_HALF = S5_GROUPS_PER_LANE_GROUP * S5_STATE
S5_SW = 2 * S5_HALF
S5_SEGMENTS = 8

W_FL = 3 * D_FOX
W_REST = W_FL + FOX_HEADS
R_GF, R_U, R_GS, R_QM, R_GM, R_GL = 0, 768, 1536, 2304, 2816, 3328
R_COLS = R_GL + 3 * D_MODEL

VMEM_LIMIT = 56 * 1024 * 1024


def _dot(a, b, **kw):
    return jnp.dot(a, b, preferred_element_type=F32, **kw)


def _dot_nt(a, b, **kw):
    return lax.dot_general(a, b, (((1,), (1,)), ((), ())), preferred_element_type=F32, **kw)


def _dot_nt_split(a, b):
    a_hi, b_hi = a.astype(BF16), b.astype(BF16)
    a_lo = (a - a_hi.astype(F32)).astype(BF16)
    b_lo = (b - b_hi.astype(F32)).astype(BF16)
    return _dot_nt(a_hi, b_hi) + (_dot_nt(a_hi, b_lo) + _dot_nt(a_lo, b_hi))


def _rms(xf, g):
    return xf * lax.rsqrt(jnp.mean(xf * xf, axis=-1, keepdims=True) + EPS) * g


def _sigmoid(z):
    return 1.0 / (1.0 + jnp.exp(-z))


def _silu(z):
    return z * _sigmoid(z)


def _gelu_tanh(y):
    return 0.5 * y * (1.0 + jnp.tanh(math.sqrt(2.0 / math.pi) * (y + 0.044715 * (y * y * y))))


def _split3(a):
    hi = a.astype(BF16)
    rest = a - hi.astype(F32)
    mid = rest.astype(BF16)
    lo = (rest - mid.astype(F32)).astype(BF16)
    return jnp.concatenate([hi, mid, lo], axis=-1)


def _const_spec(shape):
    nd = len(shape)
    return pl.BlockSpec(shape, lambda *_: (0,) * nd, pipeline_mode=pl.Buffered(1))


def _columns_kernel(wt_ref, o_ref, *, valid):
    cols = wt_ref[:, 0, :].T
    if valid < LANES:
        cols = jnp.where(lax.broadcasted_iota(jnp.int32, (1, LANES), 1) < valid, cols, 0.0)
    o_ref[...] = cols.astype(BF16)


def _rows_kernel(wt_ref, o_ref):
    o_ref[...] = wt_ref[:, 0, :].astype(BF16)


def _weight_prep(w_in):
    depth, d, n_in = w_in.shape
    assert depth == 1 and n_in == W_REST + R_COLS
    wt = jnp.transpose(w_in, (2, 0, 1))

    def columns(first, count, valid=LANES, name="weight_cols"):
        blocks = pl.cdiv(count, LANES)
        return pl.pallas_call(
            functools.partial(_columns_kernel, valid=valid),
            grid=(blocks,),
            in_specs=[pl.BlockSpec((pl.Element(LANES), pl.Element(1), pl.Element(d)),
                                   lambda c: (first + LANES * c, 0, 0))],
            out_specs=pl.BlockSpec((d, LANES), lambda c: (0, c)),
            out_shape=jax.ShapeDtypeStruct((d, blocks * LANES), BF16),
            name=name,
        )(wt)

    wqk = columns(0, 2 * D_FOX, name="weight_qk")
    wfl = columns(W_FL, LANES, valid=FOX_HEADS, name="weight_forget")
    wrest = columns(W_REST, R_COLS, name="weight_rest")
    wvt = pl.pallas_call(
        _rows_kernel,
        grid=(D_FOX // LANES,),
        in_specs=[pl.BlockSpec((LANES, 1, d), lambda r: (2 * D_FOX // LANES + r, 0, 0))],
        out_specs=pl.BlockSpec((LANES, d), lambda r: (r, 0)),
        out_shape=jax.ShapeDtypeStruct((D_FOX, d), BF16),
        name="weight_vt",
    )(wt)
    return wqk, wvt, wfl, wrest


def _mem_kv_kernel(mem_ref, g_ref, w_ref, mk_ref, mv_ref):
    h = _rms(mem_ref[...], g_ref[...]).astype(BF16)
    kv = _dot(h, w_ref[...])
    mk_ref[...] = kv[:, :D_MEM].astype(BF16)
    mv_ref[...] = kv[:, D_MEM:].astype(BF16)


def _mem_kv(mem, g, w):
    bn, m, _ = mem.shape
    out = jax.ShapeDtypeStruct((bn, m, D_MEM), BF16)
    return pl.pallas_call(
        _mem_kv_kernel,
        grid=(bn,),
        in_specs=[pl.BlockSpec((None, m, D_MODEL), lambda b: (b, 0, 0)),
                  _const_spec((1, D_MODEL)),
                  _const_spec((D_MODEL, 2 * D_MEM))],
        out_specs=[pl.BlockSpec((None, m, D_MEM), lambda b: (b, 0, 0))] * 2,
        out_shape=[out, out],
        name="mem_kv",
    )(mem, g, w)


def _in_proj_kernel(x_ref, g_ref, w_ref, wu_ref, wvt_ref, wfl_ref, bfl_ref, spread_ref, ones_ref,
                    q_ref, k_ref, vt_ref, uc_ref, fqa_ref, fka_ref, fend_ref, nq_ref, nk_ref,
                    us_scr, carry_scr, *, tm):
    @pl.when(pl.program_id(1) == 0)
    def _():
        carry_scr[...] = jnp.zeros_like(carry_scr)

    h = _rms(x_ref[...], g_ref[...]).astype(BF16)
    qb = (_dot(h, w_ref[:, 0:D_FOX]) * (LOG2E * FOX_HEAD_DIM ** -0.5)).astype(BF16)
    kb = _dot(h, w_ref[:, D_FOX:2 * D_FOX]).astype(BF16)
    q_ref[...] = qb
    k_ref[...] = kb
    for pair in range(FOX_HEADS // 2):
        vt_ref[pair] = _dot_nt(wvt_ref[pair * LANES:(pair + 1) * LANES, :], h).astype(BF16)

    head_of_col = lax.broadcasted_iota(jnp.int32, (D_FOX, LANES), 0) // FOX_HEAD_DIM
    sel = (head_of_col == lax.broadcasted_iota(jnp.int32, (D_FOX, LANES), 1)).astype(BF16)
    sq = lambda a: (a.astype(F32) * a.astype(F32)).astype(BF16)
    nq_ref[...] = jnp.max(_dot(sq(qb), sel), axis=0, keepdims=True)
    nk_ref[...] = jnp.max(_dot(sq(kb), sel), axis=0, keepdims=True)

    u = _dot(h, wu_ref[...])
    nc = tm // S5_CHUNK
    for lg in range(S5_LANE_GROUPS):
        us_scr[lg] = u[:, lg * LANES:(lg + 1) * LANES]
        for j in range(S5_CHUNK):
            c0 = lg * S5_CW + j * LANES
            uc_ref[:, c0:c0 + LANES] = us_scr[lg, pl.ds(j, nc, stride=S5_CHUNK), :].astype(BF16)

    z = _dot(h, wfl_ref[...]) + bfl_ref[...]
    logf = (jnp.minimum(z, 0.0) - jnp.log1p(jnp.exp(-jnp.abs(z)))) * LOG2E
    row = lax.broadcasted_iota(jnp.int32, (tm, tm), 0)
    col = lax.broadcasted_iota(jnp.int32, (tm, tm), 1)
    tri = (col <= row).astype(BF16)
    parts = _dot(tri, _split3(logf))
    f = (parts[:, 0:LANES] + parts[:, LANES:2 * LANES]) + parts[:, 2 * LANES:] + carry_scr[...]
    carry_scr[...] = f[tm - 1:tm, :]
    fend_ref[...] = f[tm - 1:tm, :]
    aug = _dot(_split3(f), spread_ref[...]) + ones_ref[...]
    fka_ref[...] = aug[:, 0:LANES].astype(BF16)
    fqa_ref[...] = aug[:, LANES:].astype(BF16)


def _fox_aug_constants():
    spread = np.zeros((3 * LANES, 2 * LANES), np.float32)
    ones = np.zeros((1, 2 * LANES), np.float32)
    half = LANES // 2
    for h in range(FOX_HEADS):
        for i in range(3):
            spread[i * LANES + h, 3 * h + i] = 1.0
            spread[i * LANES + h, LANES + half + 3 * h + i] = 1.0
            ones[0, half + 3 * h + i] = 1.0
            ones[0, LANES + 3 * h + i] = -1.0
    return jnp.asarray(spread, BF16), jnp.asarray(ones, F32)


def _in_proj(x, g, wqk, wrest, wvt, wfl, bfl, *, tm):
    bn, l, _ = x.shape
    nc = tm // S5_CHUNK
    nt = l // tm
    pairs = FOX_HEADS // 2
    spread, ones = _fox_aug_constants()
    tok = lambda width: pl.BlockSpec((None, tm, width), lambda b, i: (b, i, 0))
    per_tile = pl.BlockSpec((None, None, 1, LANES), lambda b, i: (b, i, 0, 0))
    qk = jax.ShapeDtypeStruct((bn, l, D_FOX), BF16)
    aug = jax.ShapeDtypeStruct((bn, l, LANES), BF16)
    stat = jax.ShapeDtypeStruct((bn, nt, 1, LANES), F32)
    return pl.pallas_call(
        functools.partial(_in_proj_kernel, tm=tm),
        grid=(bn, nt),
        in_specs=[tok(D_MODEL),
                  _const_spec((1, D_MODEL)),
                  _const_spec((D_MODEL, 2 * D_FOX)),
                  pl.BlockSpec((D_MODEL, D_S5), lambda *_: (0, R_U // D_S5), pipeline_mode=pl.Buffered(1)),
                  _const_spec((D_FOX, D_MODEL)),
                  _const_spec((D_MODEL, LANES)),
                  _const_spec((1, LANES)),
                  _const_spec((3 * LANES, 2 * LANES)),
                  _const_spec((1, 2 * LANES))],
        out_specs=[tok(D_FOX), tok(D_FOX),
                   pl.BlockSpec((None, pairs, None, LANES, tm), lambda b, i: (b, 0, i, 0, 0)),
                   pl.BlockSpec((None, nc, S5_LANE_GROUPS * S5_CW), lambda b, i: (b, i, 0)),
                   tok(LANES), tok(LANES), per_tile, per_tile, per_tile],
        out_shape=[qk, qk,
                   jax.ShapeDtypeStruct((bn, pairs, nt, LANES, tm), BF16),
                   jax.ShapeDtypeStruct((bn, l // S5_CHUNK, S5_LANE_GROUPS * S5_CW), BF16),
                   aug, aug, stat, stat, stat],
        scratch_shapes=[pltpu.VMEM((S5_LANE_GROUPS, tm, LANES), F32), pltpu.VMEM((1, LANES), F32)],
        compiler_params=pltpu.CompilerParams(
            dimension_semantics=("arbitrary", "arbitrary"), vmem_limit_bytes=VMEM_LIMIT),
        name="in_proj",
    )(x, g, wqk, wrest, wvt, wfl, bfl, spread, ones)


def _fox_kernel(fend_ref, qkb_ref, q_ref, fqa_ref, k_ref, fka_ref, vt_ref, o_ref,
                m_scr, l_scr, alpha_scr, acc_scr, s_scr, p_scr, smax_scr, *, t, nt):
    b = pl.program_id(0)
    pair = pl.program_id(1)
    qi = pl.program_id(2)
    lane = lax.broadcasted_iota(jnp.int32, (1, LANES), 1)
    q = q_ref[...]
    fqa = fqa_ref[...]

    heads = (0, 1)
    q_rows, base = [], []
    for e in heads:
        head = 2 * pair + e
        head_lanes = (lane >= e * FOX_HEAD_DIM) & (lane < (e + 1) * FOX_HEAD_DIM)
        f_lanes = lax.rem(lane, LANES // 2) // 3 == head
        q_rows.append(jnp.concatenate([jnp.where(head_lanes, q, jnp.zeros_like(q)),
                                       jnp.where(f_lanes, fqa, jnp.zeros_like(fqa))], axis=-1))
        base.append((b * FOX_HEADS + head) * nt)
    q_aug = jnp.concatenate(q_rows, axis=0)

    def span(which):
        return slice(which[0] * t, (which[-1] + 1) * t)

    def scores(which, j):
        k0 = pl.multiple_of(jnp.maximum(j, 0) * t, t)
        k_aug = jnp.concatenate([k_ref[pl.ds(k0, t), :], fka_ref[pl.ds(k0, t), :]], axis=-1)
        return _dot_nt(k_aug, q_aug[span(which)])

    def stage_scores(which, j, slot):
        s = scores(which, j)
        s_scr[slot, :, span(which)] = s
        smax_scr[slot, :, span(which)] = jnp.max(s, axis=0, keepdims=True)

    def weighted_values(which, j, slot):
        c = span(which)
        acc_scr[:, c] = alpha_scr[slot, :, c] * acc_scr[:, c] + _dot(vt_ref[j], p_scr[slot, :, c])

    def softmax(which, s_slot, p_slot):
        c = span(which)
        m_old = m_scr[:, c]
        m_new = jnp.maximum(m_old, smax_scr[s_slot, :, c])
        p = jnp.exp2(s_scr[s_slot, :, c] - m_new)
        alpha = jnp.exp2(m_old - m_new)
        m_scr[:, c] = m_new
        l_scr[:, c] = alpha * l_scr[:, c] + jnp.sum(p, axis=0, keepdims=True)
        p_scr[p_slot, :, c] = p.astype(BF16)
        alpha_scr[p_slot, :, c] = alpha

    key_pos = lax.broadcasted_iota(jnp.int32, (t, 2 * t), 0)
    query_pos = lax.rem(lax.broadcasted_iota(jnp.int32, (t, 2 * t), 1), t)
    s = jnp.where(key_pos <= query_pos, scores(heads, qi), NEG)
    m = jnp.max(s, axis=0, keepdims=True)
    p = jnp.exp2(s - m)
    m_scr[...] = m
    l_scr[...] = jnp.sum(p, axis=0, keepdims=True)
    p_scr[0] = p.astype(BF16)
    alpha_scr[0] = jnp.zeros_like(m)
    acc_scr[...] = jnp.zeros_like(acc_scr)
    slack = [qkb_ref[base[e] + qi] - jnp.min(m[:, span((e,))]) + SKIP_LOG2 for e in heads]
    stage_scores(heads, qi - 1, 0)

    def wanted(e, j):
        f_hi = fend_ref[base[e] + jnp.maximum(qi - 1, 0)]
        reach = f_hi - fend_ref[base[e] + jnp.maximum(j, 0)] + slack[e]
        return jnp.logical_and(j >= 0, reach >= 0.0).astype(jnp.int32)

    def trip(j, which, tiles):
        assert tiles == 1 or tiles % 2 == 0
        for i in range(tiles):
            here, there = i % 2, (i + 1) % 2 if tiles > 1 else 0
            weighted_values(which, j + 1 - i, here)
            softmax(which, here, there)
            stage_scores(which, j - 1 - i, there)

    def walk(j, which, tiles):
        def go(jj):
            flag = wanted(which[0], jj - (tiles - 1))
            for e in which[1:]:
                flag = flag * wanted(e, jj - (tiles - 1))
            return flag

        def body(carry):
            trip(carry[0], which, tiles)
            return carry[0] - tiles, go(carry[0] - tiles)

        return lax.while_loop(lambda c: c[1] > 0, body, (j, go(j)))[0]

    j_both = walk(qi - 1, heads, 1)
    for e in heads:
        j_stop = walk(walk(j_both, (e,), 2), (e,), 1)
        weighted_values((e,), j_stop + 1, 0)
    out = acc_scr[...] * (1.0 / l_scr[...])
    dim = lax.broadcasted_iota(jnp.int32, (LANES, 1), 0)
    o_ref[...] = jnp.where(dim < FOX_HEAD_DIM, out[:, span((0,))], out[:, span((1,))]).T.astype(BF16)


def _fox(q, fqa, k, fka, vt, fend, qkb, *, t):
    bn, l, _ = q.shape
    pairs = FOX_HEADS // 2
    nt = l // t
    query_tile = pl.BlockSpec((None, t, LANES), lambda b, p, i, *_: (b, i, p))
    grid_spec = pltpu.PrefetchScalarGridSpec(
        num_scalar_prefetch=2,
        grid=(bn, pairs, nt),
        in_specs=[query_tile,
                  pl.BlockSpec((None, t, LANES), lambda b, p, i, *_: (b, i, 0)),
                  pl.BlockSpec((None, l, LANES), lambda b, p, i, *_: (b, 0, p)),
                  pl.BlockSpec((None, l, LANES), lambda b, p, i, *_: (b, 0, 0)),
                  pl.BlockSpec((None, None, nt, LANES, t), lambda b, p, i, *_: (b, p, 0, 0, 0))],
        out_specs=query_tile,
        scratch_shapes=[pltpu.VMEM((1, 2 * t), F32), pltpu.VMEM((1, 2 * t), F32), pltpu.VMEM((2, 1, 2 * t), F32),
                        pltpu.VMEM((LANES, 2 * t), F32), pltpu.VMEM((2, t, 2 * t), F32),
                        pltpu.VMEM((2, t, 2 * t), BF16), pltpu.VMEM((2, 1, 2 * t), F32)])
    return pl.pallas_call(
        functools.partial(_fox_kernel, t=t, nt=nt),
        grid_spec=grid_spec,
        out_shape=jax.ShapeDtypeStruct((bn, l, D_FOX), BF16),
        compiler_params=pltpu.CompilerParams(
            dimension_semantics=("arbitrary", "arbitrary", "arbitrary"), vmem_limit_bytes=VMEM_LIMIT),
        name="fox",
    )(fend, qkb, q, fqa, k, fka, vt)


def _fox_skip_tables(fend, nq, nk):
    per_head = lambda a: jnp.swapaxes(a[:, :, 0, :FOX_HEADS], 1, 2)
    qmax = jnp.sqrt(per_head(nq))
    kmax = jnp.sqrt(jnp.max(per_head(nk), axis=2, keepdims=True))
    return per_head(fend).reshape(-1), (NORM_MARGIN * qmax * kmax).reshape(-1)


def _s5_tables_kernel(lr_ref, li_ref, ls_ref, btr_ref, bti_ref, cr_ref, ci_ref, d_ref,
                      wtoe_ref, win_ref, voutt_ref, lam_ref):
    lr, li = lr_ref[...], li_ref[...]
    step = jnp.exp(ls_ref[...])
    mag = jnp.exp(lr * step)
    ab_re, ab_im = mag * jnp.cos(li * step), mag * jnp.sin(li * step)
    den = lr * lr + li * li
    nr, ni = ab_re - 1.0, ab_im
    f_re = (nr * lr + ni * li) / den
    f_im = (ni * lr - nr * li) / den
    btr, bti = btr_ref[...], bti_ref[...]
    bb_re = f_re * btr - f_im * bti
    bb_im = f_re * bti + f_im * btr
    cr, ci = cr_ref[...], ci_ref[...]
    eye = (lax.broadcasted_iota(jnp.int32, (LANES, LANES), 0)
           == lax.broadcasted_iota(jnp.int32, (LANES, LANES), 1))
    tile = lambda j: slice(j * LANES, (j + 1) * LANES)
    zeros = jnp.zeros((LANES, LANES), BF16)
    for jp in range(S5_CHUNK):
        for j in range(jp):
            wtoe_ref[tile(jp), tile(j)] = zeros
    pr = jnp.ones_like(lr)
    pi = jnp.zeros_like(lr)
    for tau in range(S5_CHUNK + 1):
        a_re = cr * pr - ci * pi
        a_im = cr * pi + ci * pr
        if tau < S5_CHUNK:
            kt = _dot_nt_split(bb_re, a_re) - _dot_nt_split(bb_im, a_im)
            if tau == 0:
                kt = kt + jnp.where(eye, d_ref[...], 0.0)
            kt = kt.astype(BF16)
            for jp in range(S5_CHUNK - tau):
                wtoe_ref[tile(jp), tile(jp + tau)] = kt
            j = S5_CHUNK - 1 - tau
            win_ref[tile(j), 0:S5_HALF] = (pr * bb_re - pi * bb_im).astype(BF16)
            win_ref[tile(j), S5_HALF:] = (pr * bb_im + pi * bb_re).astype(BF16)
        if tau >= 1:
            voutt_ref[tile(tau - 1), 0:S5_HALF] = a_re.astype(BF16)
            voutt_ref[tile(tau - 1), S5_HALF:] = (-a_im).astype(BF16)
        if tau == S5_CHUNK:
            lam_ref[:, 0:S5_HALF] = pr
            lam_ref[:, S5_HALF:] = pi
        pr, pi = pr * ab_re - pi * ab_im, pr * ab_im + pi * ab_re


def _s5_tables(lam_re, lam_im, log_step, b_re, b_im, c_re, c_im, d):
    lg, gp = S5_LANE_GROUPS, S5_GROUPS_PER_LANE_GROUP
    eye = jnp.eye(gp, dtype=F32)

    def block_diag(a):
        a = a.astype(F32).reshape(lg, gp, S5_GROUP, S5_STATE)
        return jnp.einsum("Ggcp,gh->Ggchp", a, eye).reshape(lg, LANES, S5_HALF)

    lanes = lambda a: a.astype(F32).reshape(lg, 1, S5_HALF)
    step = jnp.broadcast_to(log_step.astype(F32)[:, None], (S5_GROUPS, S5_STATE))
    spec = lambda r, c: pl.BlockSpec((None, r, c), lambda g: (g, 0, 0))
    return pl.pallas_call(
        _s5_tables_kernel,
        grid=(lg,),
        in_specs=[spec(1, S5_HALF)] * 3 + [spec(LANES, S5_HALF)] * 4 + [spec(1, LANES)],
        out_specs=[spec(S5_CW, S5_CW), spec(S5_CW, S5_SW), spec(S5_CW, S5_SW), spec(1, S5_SW)],
        out_shape=[jax.ShapeDtypeStruct((lg, S5_CW, S5_CW), BF16),
                   jax.ShapeDtypeStruct((lg, S5_CW, S5_SW), BF16),
                   jax.ShapeDtypeStruct((lg, S5_CW, S5_SW), BF16),
                   jax.ShapeDtypeStruct((lg, 1, S5_SW), F32)],
        compiler_params=pltpu.CompilerParams(
            dimension_semantics=("arbitrary",), vmem_limit_bytes=VMEM_LIMIT),
        name="s5_tables",
    )(lanes(lam_re), lanes(lam_im), lanes(step),
      block_diag(jnp.swapaxes(b_re, 1, 2)), block_diag(jnp.swapaxes(b_im, 1, 2)),
      block_diag(c_re), block_diag(c_im), d.astype(F32).reshape(lg, 1, LANES))


def _s5_kernel(uc_ref, wtoe_ref, win_ref, voutt_ref, lam_ref, yc_ref, e_scr, hs_scr, *, n):
    x = uc_ref[...]
    e = _dot(x, win_ref[...])
    tiles = S5_SW // LANES
    half = tiles // 2
    lane_tile = lambda c: slice(c * LANES, (c + 1) * LANES)
    for c in range(tiles):
        e_scr[c, 0:n, :] = e[:, lane_tile(c)]
        e_scr[c, n:, :] = jnp.zeros((S5_SEGMENTS, LANES), F32)
    lam = jnp.broadcast_to(lam_ref[...], (S5_SEGMENTS, S5_SW))
    lam_r = [lam[:, lane_tile(c)] for c in range(half)]
    lam_i = [lam[:, lane_tile(half + c)] for c in range(half)]
    cmul = lambda ar, ai, br, bi: (ar * br - ai * bi, ar * bi + ai * br)

    seg = n // S5_SEGMENTS + 1
    run_rows = lambda r: pl.ds(r, S5_SEGMENTS, stride=seg)
    zero = jnp.zeros((S5_SEGMENTS, LANES), F32)
    one = jnp.ones((S5_SEGMENTS, LANES), F32)
    zero_row = jnp.zeros((1, LANES), F32)

    def scan_runs(r, carry):
        h, pw = carry
        h_next, pw_next = [None] * tiles, [None] * tiles
        for c in range(half):
            hs_scr[c, run_rows(r), :] = h[c]
            hs_scr[half + c, run_rows(r), :] = h[half + c]
            nr, ni = cmul(lam_r[c], lam_i[c], h[c], h[half + c])
            h_next[c] = nr + e_scr[c, run_rows(r), :]
            h_next[half + c] = ni + e_scr[half + c, run_rows(r), :]
            pw_next[c], pw_next[half + c] = cmul(lam_r[c], lam_i[c], pw[c], pw[half + c])
        return tuple(h_next), tuple(pw_next)

    unit = (one,) * half + (zero,) * half
    run_end, run_mult = lax.fori_loop(0, seg, scan_runs, ((zero,) * tiles, unit))

    start = [None] * tiles
    for c in range(half):
        sr, si, rows_r, rows_i = zero_row, zero_row, [], []
        for s in range(S5_SEGMENTS):
            rows_r.append(sr)
            rows_i.append(si)
            nr, ni = cmul(run_mult[c][0:1], run_mult[half + c][0:1], sr, si)
            sr, si = nr + run_end[c][s:s + 1], ni + run_end[half + c][s:s + 1]
        start[c], start[half + c] = jnp.concatenate(rows_r, axis=0), jnp.concatenate(rows_i, axis=0)

    def add_run_starts(r, pw):
        pw_next = [None] * tiles
        for c in range(half):
            ar, ai = cmul(pw[c], pw[half + c], start[c], start[half + c])
            hs_scr[c, run_rows(r), :] = hs_scr[c, run_rows(r), :] + ar
            hs_scr[half + c, run_rows(r), :] = hs_scr[half + c, run_rows(r), :] + ai
            pw_next[c], pw_next[half + c] = cmul(lam_r[c], lam_i[c], pw[c], pw[half + c])
        return tuple(pw_next)

    lax.fori_loop(0, seg, add_run_starts, unit)
    hs = jnp.concatenate([hs_scr[c, 0:n, :] for c in range(tiles)], axis=-1).astype(BF16)
    blk = 2 * LANES
    for jb in range(S5_CW // blk):
        cols = slice(jb * blk, (jb + 1) * blk)
        kk = (jb + 1) * blk
        y = _dot(x[:, :kk], wtoe_ref[0:kk, cols]) + _dot_nt(hs, voutt_ref[cols, :])
        yc_ref[:, cols] = y.astype(BF16)


def _s5(uc, wtoe, win, voutt, lam):
    bn, n, _ = uc.shape
    per_lg = lambda r, c: pl.BlockSpec((None, r, c), lambda g, b: (g, 0, 0))
    return pl.pallas_call(
        functools.partial(_s5_kernel, n=n),
        grid=(S5_LANE_GROUPS, bn),
        in_specs=[pl.BlockSpec((None, n, S5_CW), lambda g, b: (b, 0, g)),
                  per_lg(S5_CW, S5_CW), per_lg(S5_CW, S5_SW), per_lg(S5_CW, S5_SW), per_lg(1, S5_SW)],
        out_specs=pl.BlockSpec((None, n, S5_CW), lambda g, b: (b, 0, g)),
        out_shape=jax.ShapeDtypeStruct(uc.shape, BF16),
        scratch_shapes=[pltpu.VMEM((S5_SW // LANES, n + S5_SEGMENTS, LANES), F32)] * 2,
        compiler_params=pltpu.CompilerParams(
            dimension_semantics=("arbitrary", "arbitrary"), vmem_limit_bytes=VMEM_LIMIT),
        name="s5",
    )(uc, wtoe, win, voutt, lam)


def _final_kernel(x_ref, yfox_ref, yc_ref, mk_ref, mv_ref, gn_ref, gfin_ref, wg_ref, bm_ref,
                  wglu_ref, bglu_ref, wpf_ref, wps_ref, wpm_ref, wout_ref, o_ref, ys_scr, *, tm):
    x = x_ref[...]
    h = _rms(x, gn_ref[...]).astype(BF16)

    def proj(c0, width):
        return _dot(h, wg_ref[:, c0:c0 + width])

    def gate(i):
        return _sigmoid(proj(R_GL + i * D_MODEL, D_MODEL) + bm_ref[:, i * D_MODEL:(i + 1) * D_MODEL])

    a = (yfox_ref[...].astype(F32) * _silu(proj(R_GF, D_FOX))).astype(BF16)
    merged = gate(0) * _dot(a, wpf_ref[...])

    nc = tm // S5_CHUNK
    for lg in range(S5_LANE_GROUPS):
        for j in range(S5_CHUNK):
            c0 = lg * S5_CW + j * LANES
            ys_scr[lg, pl.ds(j, nc, stride=S5_CHUNK), :] = yc_ref[:, c0:c0 + LANES].astype(F32)
    y = _gelu_tanh(jnp.concatenate([ys_scr[lg] for lg in range(S5_LANE_GROUPS)], axis=-1))
    y = y * _sigmoid(_dot(y.astype(BF16), wglu_ref[...]) + bglu_ref[...])
    y = (y * _silu(proj(R_GS, D_S5))).astype(BF16)
    merged = merged + gate(1) * _dot(y, wps_ref[...])

    qm = (proj(R_QM, D_MEM) * (MEM_HEAD_DIM ** -0.5)).astype(BF16)
    heads = []
    for hd in range(MEM_HEADS):
        cols = slice(hd * MEM_HEAD_DIM, (hd + 1) * MEM_HEAD_DIM)
        s = _dot_nt(qm[:, cols], mk_ref[:, cols])
        p = jnp.exp(s - jnp.max(s, axis=-1, keepdims=True))
        inv = 1.0 / jnp.sum(p, axis=-1, keepdims=True)
        heads.append(_dot(p.astype(BF16), mv_ref[:, cols]) * inv)
    ymem = jnp.concatenate(heads, axis=-1)
    a = (ymem * _silu(proj(R_GM, D_MEM))).astype(BF16)
    merged = merged + gate(2) * _dot(a, wpm_ref[...])

    out = x + _dot(merged.astype(BF16), wout_ref[...])
    o_ref[...] = _rms(out, gfin_ref[...])


def _final(x, yfox, yc, mk, mv, gn, gfin, wg, bm, wglu, bglu, wpf, wps, wpm, wout, *, tm):
    bn, l, _ = x.shape
    m = mk.shape[1]
    nc = tm // S5_CHUNK
    tok = lambda width: pl.BlockSpec((None, tm, width), lambda b, i: (b, i, 0))
    mem = pl.BlockSpec((None, m, D_MEM), lambda b, i: (b, 0, 0))
    return pl.pallas_call(
        functools.partial(_final_kernel, tm=tm),
        grid=(bn, l // tm),
        in_specs=[tok(D_MODEL), tok(D_FOX),
                  pl.BlockSpec((None, nc, S5_LANE_GROUPS * S5_CW), lambda b, i: (b, i, 0)),
                  mem, mem,
                  _const_spec((1, D_MODEL)), _const_spec((1, D_MODEL)),
                  _const_spec((D_MODEL, R_COLS)), _const_spec((1, 3 * D_MODEL)),
                  _const_spec((D_S5, D_S5)), _const_spec((1, D_S5)),
                  _const_spec((D_FOX, D_MODEL)), _const_spec((D_S5, D_MODEL)),
                  _const_spec((D_MEM, D_MODEL)), _const_spec((D_MODEL, D_MODEL))],
        out_specs=tok(D_MODEL),
        out_shape=jax.ShapeDtypeStruct(x.shape, x.dtype),
        scratch_shapes=[pltpu.VMEM((S5_LANE_GROUPS, tm, LANES), F32)],
        compiler_params=pltpu.CompilerParams(
            dimension_semantics=("arbitrary", "arbitrary"), vmem_limit_bytes=VMEM_LIMIT),
        name="final",
    )(x, yfox, yc, mk, mv, gn, gfin, wg, bm, wglu, bglu, wpf, wps, wpm, wout)


def _layer(x, mem, g_norm, g_mem_norm, w_in, b_forget, b_merge, w_mem_kv, lam_re, lam_im, log_step,
           s5_b_re, s5_b_im, s5_c_re, s5_c_im, s5_d, w_glu, b_glu, w_proj_fox, w_proj_s5, w_proj_mem,
           w_out, g_out, *, tm_in, t_attn, tm_out):
    bn, l, _ = x.shape
    assert tm_in == t_attn, "the per-tile q norms from in_proj are indexed by attention query tile"
    row = lambda a: a.reshape(1, -1).astype(F32)
    wqk, wvt, wfl_pad, wrest = _weight_prep(w_in)
    bfl_pad = jnp.pad(row(b_forget), ((0, 0), (0, LANES - FOX_HEADS)))

    mk, mv = _mem_kv(mem, row(g_mem_norm), w_mem_kv.astype(BF16))
    q, k, vt, uc, fqa, fka, fend, nq, nk = _in_proj(x, row(g_norm), wqk, wrest, wvt, wfl_pad, bfl_pad, tm=tm_in)
    yfox = _fox(q, fqa, k, fka, vt, *_fox_skip_tables(fend, nq, nk), t=t_attn)
    tables = _s5_tables(lam_re, lam_im, log_step, s5_b_re, s5_b_im, s5_c_re, s5_c_im, s5_d)
    yc = _s5(uc, *tables)
    return _final(x, yfox, yc, mk, mv, row(g_norm), row(g_out), wrest, row(b_merge),
                  w_glu.astype(BF16), row(b_glu), w_proj_fox.astype(BF16), w_proj_s5.astype(BF16),
                  w_proj_mem.astype(BF16), w_out.astype(BF16), tm=tm_out)


def kernel(x, mem, g_norm, g_mem_norm, g_final, w_in, b_forget, b_merge, w_mem_kv, lam_re, lam_im, log_step,
           s5_b_re, s5_b_im, s5_c_re, s5_c_im, s5_d, w_glu, b_glu, w_proj_fox, w_proj_s5, w_proj_mem, w_out):
    depth = w_in.shape[0]
    assert depth == 1, "the fused final kernel applies the closing RMSNorm: one layer only"
    l = x.shape[1]
    tm_in = min(512, l)
    t_attn = min(512, l)
    tm_out = min(512, l)
    return _layer(x, mem, g_norm[0], g_mem_norm[0], w_in, b_forget[0], b_merge[0], w_mem_kv[0],
                  lam_re[0], lam_im[0], log_step[0], s5_b_re[0], s5_b_im[0], s5_c_re[0], s5_c_im[0],
                  s5_d[0], w_glu[0], b_glu[0], w_proj_fox[0], w_proj_s5[0], w_proj_mem[0], w_out[0],
                  g_final, tm_in=tm_in, t_attn=t_attn, tm_out=tm_out)
```

```python
import functools
import math

import jax
import jax.numpy as jnp
import numpy as np
from jax import lax
from jax.experimental import pallas as pl
from jax.experimental.pallas import tpu as pltpu

F32 = jnp.float32
BF16 = jnp.bfloat16

D_MODEL = 1024
EPS = 1e-6
NEG = -1e30
LOG2E = math.log2(math.e)
SKIP_LOG2 = 127.0
NORM_MARGIN = 1.03

FOX_HEAD_DIM = 64
D_FOX = 768
FOX_HEADS = D_FOX // FOX_HEAD_DIM

D_S5 = 768
S5_GROUP = 16
S5_GROUPS = D_S5 // S5_GROUP
S5_STATE = 64

D_MEM = 512
MEM_HEADS = 4
MEM_HEAD_DIM = D_MEM // MEM_HEADS

LANES = 128
S5_CHUNK = 16
S5_LANE_GROUPS = D_S5 // LANES
S5_GROUPS_PER_LANE_GROUP = LANES // S5_GROUP
S5_CW = S5_CHUNK * LANES
S5_HALF = S5_GROUPS_PER_LANE_GROUP * S5_STATE
S5_SW = 2 * S5_HALF
S5_SEGMENTS = 8

W_FL = 3 * D_FOX
W_REST = W_FL + FOX_HEADS
R_GF, R_U, R_GS, R_QM, R_GM, R_GL = 0, 768, 1536, 2304, 2816, 3328
R_COLS = R_GL + 3 * D_MODEL

VMEM_LIMIT = 56 * 1024 * 1024


def _dot(a, b, **kw):
    return jnp.dot(a, b, preferred_element_type=F32, **kw)


def _dot_nt(a, b, **kw):
    return lax.dot_general(a, b, (((1,), (1,)), ((), ())), preferred_element_type=F32, **kw)


def _dot_nt_split(a, b):
    a_hi, b_hi = a.astype(BF16), b.astype(BF16)
    a_lo = (a - a_hi.astype(F32)).astype(BF16)
    b_lo = (b - b_hi.astype(F32)).astype(BF16)
    return _dot_nt(a_hi, b_hi) + (_dot_nt(a_hi, b_lo) + _dot_nt(a_lo, b_hi))


def _rms(xf, g):
    return xf * lax.rsqrt(jnp.mean(xf * xf, axis=-1, keepdims=True) + EPS) * g


def _sigmoid(z):
    return 1.0 / (1.0 + jnp.exp(-z))


def _silu(z):
    return z * _sigmoid(z)


def _gelu_tanh(y):
    return 0.5 * y * (1.0 + jnp.tanh(math.sqrt(2.0 / math.pi) * (y + 0.044715 * (y * y * y))))


def _split3(a):
    hi = a.astype(BF16)
    rest = a - hi.astype(F32)
    mid = rest.astype(BF16)
    lo = (rest - mid.astype(F32)).astype(BF16)
    return jnp.concatenate([hi, mid, lo], axis=-1)


def _const_spec(shape):
    nd = len(shape)
    return pl.BlockSpec(shape, lambda *_: (0,) * nd, pipeline_mode=pl.Buffered(1))


def _columns_kernel(wt_ref, o_ref, *, valid):
    cols = wt_ref[:, 0, :].T
    width = cols.shape[1]
    if valid < width:
        cols = jnp.where(lax.broadcasted_iota(jnp.int32, (1, width), 1) < valid, cols, 0.0)
    o_ref[...] = cols.astype(BF16)


def _rows_kernel(wt_ref, o_ref):
    o_ref[...] = wt_ref[:, 0, :].astype(BF16)


def _weight_prep(w_in):
    depth, d, n_in = w_in.shape
    assert depth == 1 and n_in == W_REST + R_COLS
    wt = jnp.transpose(w_in, (2, 0, 1))

    def columns(first, count, width, valid=None, name="weight_cols"):
        assert count % width == 0
        return pl.pallas_call(
            functools.partial(_columns_kernel, valid=width if valid is None else valid),
            grid=(count // width,),
            in_specs=[pl.BlockSpec((pl.Element(width), pl.Element(1), pl.Element(d)),
                                   lambda c: (first + width * c, 0, 0))],
            out_specs=pl.BlockSpec((d, width), lambda c: (0, c)),
            out_shape=jax.ShapeDtypeStruct((d, count), BF16),
            name=name,
        )(wt)

    wqk = columns(0, 2 * D_FOX, 2 * LANES, name="weight_qk")
    wfl = columns(W_FL, LANES, LANES, valid=FOX_HEADS, name="weight_forget")
    wrest = columns(W_REST, R_COLS, 2 * LANES, name="weight_rest")
    wvt = pl.pallas_call(
        _rows_kernel,
        grid=(D_FOX // LANES,),
        in_specs=[pl.BlockSpec((LANES, 1, d), lambda r: (2 * D_FOX // LANES + r, 0, 0))],
        out_specs=pl.BlockSpec((LANES, d), lambda r: (r, 0)),
        out_shape=jax.ShapeDtypeStruct((D_FOX, d), BF16),
        name="weight_vt",
    )(wt)
    return wqk, wvt, wfl, wrest


def _mem_kv_kernel(mem_ref, g_ref, w_ref, mk_ref, mv_ref):
    h = _rms(mem_ref[...], g_ref[...]).astype(BF16)
    kv = _dot(h, w_ref[...])
    mk_ref[...] = kv[:, :D_MEM].astype(BF16)
    mv_ref[...] = kv[:, D_MEM:].astype(BF16)


def _mem_kv(mem, g, w):
    bn, m, _ = mem.shape
    out = jax.ShapeDtypeStruct((bn, m, D_MEM), BF16)
    return pl.pallas_call(
        _mem_kv_kernel,
        grid=(bn,),
        in_specs=[pl.BlockSpec((None, m, D_MODEL), lambda b: (b, 0, 0)),
                  _const_spec((1, D_MODEL)),
                  _const_spec((D_MODEL, 2 * D_MEM))],
        out_specs=[pl.BlockSpec((None, m, D_MEM), lambda b: (b, 0, 0))] * 2,
        out_shape=[out, out],
        name="mem_kv",
    )(mem, g, w)


def _in_proj_kernel(x_ref, g_ref, w_ref, wu_ref, wvt_ref, wfl_ref, bfl_ref, spread_ref, ones_ref,
                    q_ref, k_ref, vt_ref, uc_ref, fqa_ref, fka_ref, fend_ref, nq_ref, nk_ref,
                    us_scr, carry_scr, *, tm):
    @pl.when(pl.program_id(1) == 0)
    def _():
        carry_scr[...] = jnp.zeros_like(carry_scr)

    h = _rms(x_ref[...], g_ref[...]).astype(BF16)
    qb = (_dot(h, w_ref[:, 0:D_FOX]) * (LOG2E * FOX_HEAD_DIM ** -0.5)).astype(BF16)
    kb = _dot(h, w_ref[:, D_FOX:2 * D_FOX]).astype(BF16)
    q_ref[...] = qb
    k_ref[...] = kb
    for pair in range(FOX_HEADS // 2):
        vt_ref[pair] = _dot_nt(wvt_ref[pair * LANES:(pair + 1) * LANES, :], h).astype(BF16)

    head_of_col = lax.broadcasted_iota(jnp.int32, (D_FOX, LANES), 0) // FOX_HEAD_DIM
    sel = (head_of_col == lax.broadcasted_iota(jnp.int32, (D_FOX, LANES), 1)).astype(BF16)
    sq = lambda a: (a.astype(F32) * a.astype(F32)).astype(BF16)
    nq_ref[...] = jnp.max(_dot(sq(qb), sel), axis=0, keepdims=True)
    nk_ref[...] = jnp.max(_dot(sq(kb), sel), axis=0, keepdims=True)

    u = _dot(h, wu_ref[...])
    nc = tm // S5_CHUNK
    for lg in range(S5_LANE_GROUPS):
        us_scr[lg] = u[:, lg * LANES:(lg + 1) * LANES]
        for j in range(S5_CHUNK):
            c0 = lg * S5_CW + j * LANES
            uc_ref[:, c0:c0 + LANES] = us_scr[lg, pl.ds(j, nc, stride=S5_CHUNK), :].astype(BF16)

    z = _dot(h, wfl_ref[...]) + bfl_ref[...]
    logf = (jnp.minimum(z, 0.0) - jnp.log1p(jnp.exp(-jnp.abs(z)))) * LOG2E
    row = lax.broadcasted_iota(jnp.int32, (tm, tm), 0)
    col = lax.broadcasted_iota(jnp.int32, (tm, tm), 1)
    tri = (col <= row).astype(BF16)
    parts = _dot(tri, _split3(logf))
    f = (parts[:, 0:LANES] + parts[:, LANES:2 * LANES]) + parts[:, 2 * LANES:] + carry_scr[...]
    carry_scr[...] = f[tm - 1:tm, :]
    fend_ref[...] = f[tm - 1:tm, :]
    aug = _dot(_split3(f), spread_ref[...]) + ones_ref[...]
    fka_ref[...] = aug[:, 0:LANES].astype(BF16)
    fqa_ref[...] = aug[:, LANES:].astype(BF16)


def _fox_aug_constants():
    spread = np.zeros((3 * LANES, 2 * LANES), np.float32)
    ones = np.zeros((1, 2 * LANES), np.float32)
    half = LANES // 2
    for h in range(FOX_HEADS):
        for i in range(3):
            spread[i * LANES + h, 3 * h + i] = 1.0
            spread[i * LANES + h, LANES + half + 3 * h + i] = 1.0
            ones[0, half + 3 * h + i] = 1.0
            ones[0, LANES + 3 * h + i] = -1.0
    return jnp.asarray(spread, BF16), jnp.asarray(ones, F32)


def _in_proj(x, g, wqk, wrest, wvt, wfl, bfl, *, tm):
    bn, l, _ = x.shape
    nc = tm // S5_CHUNK
    nt = l // tm
    pairs = FOX_HEADS // 2
    spread, ones = _fox_aug_constants()
    tok = lambda width: pl.BlockSpec((None, tm, width), lambda b, i: (b, i, 0))
    per_tile = pl.BlockSpec((None, None, 1, LANES), lambda b, i: (b, i, 0, 0))
    qk = jax.ShapeDtypeStruct((bn, l, D_FOX), BF16)
    aug = jax.ShapeDtypeStruct((bn, l, LANES), BF16)
    stat = jax.ShapeDtypeStruct((bn, nt, 1, LANES), F32)
    return pl.pallas_call(
        functools.partial(_in_proj_kernel, tm=tm),
        grid=(bn, nt),
        in_specs=[tok(D_MODEL),
                  _const_spec((1, D_MODEL)),
                  _const_spec((D_MODEL, 2 * D_FOX)),
                  pl.BlockSpec((D_MODEL, D_S5), lambda *_: (0, R_U // D_S5), pipeline_mode=pl.Buffered(1)),
                  _const_spec((D_FOX, D_MODEL)),
                  _const_spec((D_MODEL, LANES)),
                  _const_spec((1, LANES)),
                  _const_spec((3 * LANES, 2 * LANES)),
                  _const_spec((1, 2 * LANES))],
        out_specs=[tok(D_FOX), tok(D_FOX),
                   pl.BlockSpec((None, pairs, None, LANES, tm), lambda b, i: (b, 0, i, 0, 0)),
                   pl.BlockSpec((None, nc, S5_LANE_GROUPS * S5_CW), lambda b, i: (b, i, 0)),
                   tok(LANES), tok(LANES), per_tile, per_tile, per_tile],
        out_shape=[qk, qk,
                   jax.ShapeDtypeStruct((bn, pairs, nt, LANES, tm), BF16),
                   jax.ShapeDtypeStruct((bn, l // S5_CHUNK, S5_LANE_GROUPS * S5_CW), BF16),
                   aug, aug, stat, stat, stat],
        scratch_shapes=[pltpu.VMEM((S5_LANE_GROUPS, tm, LANES), F32), pltpu.VMEM((1, LANES), F32)],
        compiler_params=pltpu.CompilerParams(
            dimension_semantics=("arbitrary", "arbitrary"), vmem_limit_bytes=VMEM_LIMIT),
        name="in_proj",
    )(x, g, wqk, wrest, wvt, wfl, bfl, spread, ones)


def _fox_kernel(fend_ref, qkb_ref, q_ref, fqa_ref, k_ref, fka_ref, vt_ref, o_ref,
                m_scr, l_scr, alpha_scr, acc_scr, s_scr, p_scr, smax_scr, *, t, nt):
    b = pl.program_id(0)
    pair = pl.program_id(1)
    qi = pl.program_id(2)
    lane = lax.broadcasted_iota(jnp.int32, (1, LANES), 1)
    q = q_ref[...]
    fqa = fqa_ref[...]

    heads = (0, 1)
    q_rows, base = [], []
    for e in heads:
        head = 2 * pair + e
        head_lanes = (lane >= e * FOX_HEAD_DIM) & (lane < (e + 1) * FOX_HEAD_DIM)
        f_lanes = lax.rem(lane, LANES // 2) // 3 == head
        q_rows.append(jnp.concatenate([jnp.where(head_lanes, q, jnp.zeros_like(q)),
                                       jnp.where(f_lanes, fqa, jnp.zeros_like(fqa))], axis=-1))
        base.append((b * FOX_HEADS + head) * nt)
    q_aug = jnp.concatenate(q_rows, axis=0)

    def span(which):
        return slice(which[0] * t, (which[-1] + 1) * t)

    def scores(which, j):
        k0 = pl.multiple_of(jnp.maximum(j, 0) * t, t)
        k_aug = jnp.concatenate([k_ref[pl.ds(k0, t), :], fka_ref[pl.ds(k0, t), :]], axis=-1)
        return _dot_nt(k_aug, q_aug[span(which)])

    def stage_scores(which, j, slot):
        s = scores(which, j)
        s_scr[slot, :, span(which)] = s
        smax_scr[slot, :, span(which)] = jnp.max(s, axis=0, keepdims=True)

    def weighted_values(which, j, slot):
        c = span(which)
        acc_scr[:, c] = alpha_scr[slot, :, c] * acc_scr[:, c] + _dot(vt_ref[j], p_scr[slot, :, c])

    def softmax(which, s_slot, p_slot):
        c = span(which)
        m_old = m_scr[:, c]
        m_new = jnp.maximum(m_old, smax_scr[s_slot, :, c])
        p = jnp.exp2(s_scr[s_slot, :, c] - m_new)
        alpha = jnp.exp2(m_old - m_new)
        m_scr[:, c] = m_new
        l_scr[:, c] = alpha * l_scr[:, c] + jnp.sum(p, axis=0, keepdims=True)
        p_scr[p_slot, :, c] = p.astype(BF16)
        alpha_scr[p_slot, :, c] = alpha

    key_pos = lax.broadcasted_iota(jnp.int32, (t, 2 * t), 0)
    query_pos = lax.rem(lax.broadcasted_iota(jnp.int32, (t, 2 * t), 1), t)
    s = jnp.where(key_pos <= query_pos, scores(heads, qi), NEG)
    s_scr[0] = s
    smax_scr[0] = jnp.max(s, axis=0, keepdims=True)
    m_scr[...] = jnp.full_like(m_scr, NEG)
    l_scr[...] = jnp.zeros_like(l_scr)
    acc_scr[...] = jnp.zeros_like(acc_scr)
    softmax(heads, 0, 0)
    m = m_scr[...]
    slack = [qkb_ref[base[e] + qi] - jnp.min(m[:, span((e,))]) + SKIP_LOG2 for e in heads]
    stage_scores(heads, qi - 1, 0)

    def wanted(e, j):
        f_hi = fend_ref[base[e] + jnp.maximum(qi - 1, 0)]
        reach = f_hi - fend_ref[base[e] + jnp.maximum(j, 0)] + slack[e]
        return jnp.logical_and(j >= 0, reach >= 0.0).astype(jnp.int32)

    def trip(j, which, tiles):
        assert tiles == 1 or tiles % 2 == 0
        for i in range(tiles):
            here, there = i % 2, (i + 1) % 2 if tiles > 1 else 0
            weighted_values(which, j + 1 - i, here)
            softmax(which, here, there)
            stage_scores(which, j - 1 - i, there)

    def walk(j, which, tiles):
        def go(jj):
            flag = wanted(which[0], jj - (tiles - 1))
            for e in which[1:]:
                flag = flag * wanted(e, jj - (tiles - 1))
            return flag

        def body(carry):
            trip(carry[0], which, tiles)
            return carry[0] - tiles, go(carry[0] - tiles)

        return lax.while_loop(lambda c: c[1] > 0, body, (j, go(j)))[0]

    j_both = walk(qi - 1, heads, 1)
    for e in heads:
        j_stop = walk(walk(j_both, (e,), 2), (e,), 1)
        weighted_values((e,), j_stop + 1, 0)
    out = acc_scr[...] * (1.0 / l_scr[...])
    dim = lax.broadcasted_iota(jnp.int32, (LANES, 1), 0)
    o_ref[...] = jnp.where(dim < FOX_HEAD_DIM, out[:, span((0,))], out[:, span((1,))]).T.astype(BF16)


def _fox(q, fqa, k, fka, vt, fend, qkb, *, t):
    bn, l, _ = q.shape
    pairs = FOX_HEADS // 2
    nt = l // t
    query_tile = pl.BlockSpec((None, t, LANES), lambda b, p, i, *_: (b, i, p))
    grid_spec = pltpu.PrefetchScalarGridSpec(
        num_scalar_prefetch=2,
        grid=(bn, pairs, nt),
        in_specs=[query_tile,
                  pl.BlockSpec((None, t, LANES), lambda b, p, i, *_: (b, i, 0)),
                  pl.BlockSpec((None, l, LANES), lambda b, p, i, *_: (b, 0, p)),
                  pl.BlockSpec((None, l, LANES), lambda b, p, i, *_: (b, 0, 0)),
                  pl.BlockSpec((None, None, nt, LANES, t), lambda b, p, i, *_: (b, p, 0, 0, 0))],
        out_specs=query_tile,
        scratch_shapes=[pltpu.VMEM((1, 2 * t), F32), pltpu.VMEM((1, 2 * t), F32), pltpu.VMEM((2, 1, 2 * t), F32),
                        pltpu.VMEM((LANES, 2 * t), F32), pltpu.VMEM((2, t, 2 * t), F32),
                        pltpu.VMEM((2, t, 2 * t), BF16), pltpu.VMEM((2, 1, 2 * t), F32)])
    return pl.pallas_call(
        functools.partial(_fox_kernel, t=t, nt=nt),
        grid_spec=grid_spec,
        out_shape=jax.ShapeDtypeStruct((bn, l, D_FOX), BF16),
        compiler_params=pltpu.CompilerParams(
            dimension_semantics=("arbitrary", "arbitrary", "arbitrary"), vmem_limit_bytes=VMEM_LIMIT),
        name="fox",
    )(fend, qkb, q, fqa, k, fka, vt)


def _fox_skip_tables(fend, nq, nk):
    per_head = lambda a: jnp.swapaxes(a[:, :, 0, :FOX_HEADS], 1, 2)
    qmax = jnp.sqrt(per_head(nq))
    kmax = jnp.sqrt(jnp.max(per_head(nk), axis=2, keepdims=True))
    return per_head(fend).reshape(-1), (NORM_MARGIN * qmax * kmax).reshape(-1)


def _s5_tables_kernel(lr_ref, li_ref, ls_ref, btr_ref, bti_ref, cr_ref, ci_ref, d_ref,
                      wtoe_ref, win_ref, voutt_ref, lam_ref):
    lr, li = lr_ref[...], li_ref[...]
    step = jnp.exp(ls_ref[...])
    mag = jnp.exp(lr * step)
    ab_re, ab_im = mag * jnp.cos(li * step), mag * jnp.sin(li * step)
    den = lr * lr + li * li
    nr, ni = ab_re - 1.0, ab_im
    f_re = (nr * lr + ni * li) / den
    f_im = (ni * lr - nr * li) / den
    btr, bti = btr_ref[...], bti_ref[...]
    bb_re = f_re * btr - f_im * bti
    bb_im = f_re * bti + f_im * btr
    cr, ci = cr_ref[...], ci_ref[...]
    eye = (lax.broadcasted_iota(jnp.int32, (LANES, LANES), 0)
           == lax.broadcasted_iota(jnp.int32, (LANES, LANES), 1))
    tile = lambda j: slice(j * LANES, (j + 1) * LANES)
    zeros = jnp.zeros((LANES, LANES), BF16)
    for jp in range(S5_CHUNK):
        for j in range(jp):
            wtoe_ref[tile(jp), tile(j)] = zeros
    pr = jnp.ones_like(lr)
    pi = jnp.zeros_like(lr)
    for tau in range(S5_CHUNK + 1):
        a_re = cr * pr - ci * pi
        a_im = cr * pi + ci * pr
        if tau < S5_CHUNK:
            kt = _dot_nt_split(bb_re, a_re) - _dot_nt_split(bb_im, a_im)
            if tau == 0:
                kt = kt + jnp.where(eye, d_ref[...], 0.0)
            kt = kt.astype(BF16)
            for jp in range(S5_CHUNK - tau):
                wtoe_ref[tile(jp), tile(jp + tau)] = kt
            j = S5_CHUNK - 1 - tau
            win_ref[tile(j), 0:S5_HALF] = (pr * bb_re - pi * bb_im).astype(BF16)
            win_ref[tile(j), S5_HALF:] = (pr * bb_im + pi * bb_re).astype(BF16)
        if tau >= 1:
            voutt_ref[tile(tau - 1), 0:S5_HALF] = a_re.astype(BF16)
            voutt_ref[tile(tau - 1), S5_HALF:] = (-a_im).astype(BF16)
        if tau == S5_CHUNK:
            lam_ref[:, 0:S5_HALF] = pr
            lam_ref[:, S5_HALF:] = pi
        pr, pi = pr * ab_re - pi * ab_im, pr * ab_im + pi * ab_re


def _s5_tables(lam_re, lam_im, log_step, b_re, b_im, c_re, c_im, d):
    lg, gp = S5_LANE_GROUPS, S5_GROUPS_PER_LANE_GROUP
    eye = jnp.eye(gp, dtype=F32)

    def block_diag(a):
        a = a.astype(F32).reshape(lg, gp, S5_GROUP, S5_STATE)
        return jnp.einsum("Ggcp,gh->Ggchp", a, eye).reshape(lg, LANES, S5_HALF)

    lanes = lambda a: a.astype(F32).reshape(lg, 1, S5_HALF)
    step = jnp.broadcast_to(log_step.astype(F32)[:, None], (S5_GROUPS, S5_STATE))
    spec = lambda r, c: pl.BlockSpec((None, r, c), lambda g: (g, 0, 0))
    return pl.pallas_call(
        _s5_tables_kernel,
        grid=(lg,),
        in_specs=[spec(1, S5_HALF)] * 3 + [spec(LANES, S5_HALF)] * 4 + [spec(1, LANES)],
        out_specs=[spec(S5_CW, S5_CW), spec(S5_CW, S5_SW), spec(S5_CW, S5_SW), spec(1, S5_SW)],
        out_shape=[jax.ShapeDtypeStruct((lg, S5_CW, S5_CW), BF16),
                   jax.ShapeDtypeStruct((lg, S5_CW, S5_SW), BF16),
                   jax.ShapeDtypeStruct((lg, S5_CW, S5_SW), BF16),
                   jax.ShapeDtypeStruct((lg, 1, S5_SW), F32)],
        compiler_params=pltpu.CompilerParams(
            dimension_semantics=("arbitrary",), vmem_limit_bytes=VMEM_LIMIT),
        name="s5_tables",
    )(lanes(lam_re), lanes(lam_im), lanes(step),
      block_diag(jnp.swapaxes(b_re, 1, 2)), block_diag(jnp.swapaxes(b_im, 1, 2)),
      block_diag(c_re), block_diag(c_im), d.astype(F32).reshape(lg, 1, LANES))


def _s5_kernel(uc_ref, wtoe_ref, win_ref, voutt_ref, lam_ref, yc_ref, e_scr, hs_scr, *, n):
    x = uc_ref[...]
    e = _dot(x, win_ref[...])
    tiles = S5_SW // LANES
    half = tiles // 2
    lane_tile = lambda c: slice(c * LANES, (c + 1) * LANES)
    for c in range(tiles):
        e_scr[c, 0:n, :] = e[:, lane_tile(c)]
        e_scr[c, n:, :] = jnp.zeros((S5_SEGMENTS, LANES), F32)
    lam = jnp.broadcast_to(lam_ref[...], (S5_SEGMENTS, S5_SW))
    lam_r = [lam[:, lane_tile(c)] for c in range(half)]
    lam_i = [lam[:, lane_tile(half + c)] for c in range(half)]
    cmul = lambda ar, ai, br, bi: (ar * br - ai * bi, ar * bi + ai * br)

    seg = n // S5_SEGMENTS + 1
    run_rows = lambda r: pl.ds(r, S5_SEGMENTS, stride=seg)
    zero = jnp.zeros((S5_SEGMENTS, LANES), F32)
    one = jnp.ones((S5_SEGMENTS, LANES), F32)
    zero_row = jnp.zeros((1, LANES), F32)

    def scan_runs(r, carry):
        h, pw = carry
        h_next, pw_next = [None] * tiles, [None] * tiles
        for c in range(half):
            hs_scr[c, run_rows(r), :] = h[c]
            hs_scr[half + c, run_rows(r), :] = h[half + c]
            nr, ni = cmul(lam_r[c], lam_i[c], h[c], h[half + c])
            h_next[c] = nr + e_scr[c, run_rows(r), :]
            h_next[half + c] = ni + e_scr[half + c, run_rows(r), :]
            pw_next[c], pw_next[half + c] = cmul(lam_r[c], lam_i[c], pw[c], pw[half + c])
        return tuple(h_next), tuple(pw_next)

    unit = (one,) * half + (zero,) * half
    run_end, run_mult = lax.fori_loop(0, seg, scan_runs, ((zero,) * tiles, unit))

    start = [None] * tiles
    for c in range(half):
        sr, si, rows_r, rows_i = zero_row, zero_row, [], []
        for s in range(S5_SEGMENTS):
            rows_r.append(sr)
            rows_i.append(si)
            nr, ni = cmul(run_mult[c][0:1], run_mult[half + c][0:1], sr, si)
            sr, si = nr + run_end[c][s:s + 1], ni + run_end[half + c][s:s + 1]
        start[c], start[half + c] = jnp.concatenate(rows_r, axis=0), jnp.concatenate(rows_i, axis=0)

    def add_run_starts(r, pw):
        pw_next = [None] * tiles
        for c in range(half):
            ar, ai = cmul(pw[c], pw[half + c], start[c], start[half + c])
            hs_scr[c, run_rows(r), :] = hs_scr[c, run_rows(r), :] + ar
            hs_scr[half + c, run_rows(r), :] = hs_scr[half + c, run_rows(r), :] + ai
            pw_next[c], pw_next[half + c] = cmul(lam_r[c], lam_i[c], pw[c], pw[half + c])
        return tuple(pw_next)

    lax.fori_loop(0, seg, add_run_starts, unit)
    hs = jnp.concatenate([hs_scr[c, 0:n, :] for c in range(tiles)], axis=-1).astype(BF16)
    blk = 2 * LANES
    for jb in range(S5_CW // blk):
        cols = slice(jb * blk, (jb + 1) * blk)
        kk = (jb + 1) * blk
        y = _dot(x[:, :kk], wtoe_ref[0:kk, cols]) + _dot_nt(hs, voutt_ref[cols, :])
        yc_ref[:, cols] = y.astype(BF16)


def _s5(uc, wtoe, win, voutt, lam):
    bn, n, _ = uc.shape
    per_lg = lambda r, c: pl.BlockSpec((None, r, c), lambda g, b: (g, 0, 0))
    return pl.pallas_call(
        functools.partial(_s5_kernel, n=n),
        grid=(S5_LANE_GROUPS, bn),
        in_specs=[pl.BlockSpec((None, n, S5_CW), lambda g, b: (b, 0, g)),
                  per_lg(S5_CW, S5_CW), per_lg(S5_CW, S5_SW), per_lg(S5_CW, S5_SW), per_lg(1, S5_SW)],
        out_specs=pl.BlockSpec((None, n, S5_CW), lambda g, b: (b, 0, g)),
        out_shape=jax.ShapeDtypeStruct(uc.shape, BF16),
        scratch_shapes=[pltpu.VMEM((S5_SW // LANES, n + S5_SEGMENTS, LANES), F32)] * 2,
        compiler_params=pltpu.CompilerParams(
            dimension_semantics=("arbitrary", "arbitrary"), vmem_limit_bytes=VMEM_LIMIT),
        name="s5",
    )(uc, wtoe, win, voutt, lam)


def _final_kernel(x_ref, yfox_ref, yc_ref, mk_ref, mv_ref, gn_ref, gfin_ref, wg_ref, bm_ref,
                  wglu_ref, bglu_ref, wpf_ref, wps_ref, wpm_ref, wout_ref, o_ref, ys_scr, *, tm):
    x = x_ref[...]
    h = _rms(x, gn_ref[...]).astype(BF16)

    def proj(c0, width):
        return _dot(h, wg_ref[:, c0:c0 + width])

    def gate(i):
        return _sigmoid(proj(R_GL + i * D_MODEL, D_MODEL) + bm_ref[:, i * D_MODEL:(i + 1) * D_MODEL])

    a = (yfox_ref[...].astype(F32) * _silu(proj(R_GF, D_FOX))).astype(BF16)
    merged = gate(0) * _dot(a, wpf_ref[...])

    nc = tm // S5_CHUNK
    for lg in range(S5_LANE_GROUPS):
        for j in range(S5_CHUNK):
            c0 = lg * S5_CW + j * LANES
            ys_scr[lg, pl.ds(j, nc, stride=S5_CHUNK), :] = yc_ref[:, c0:c0 + LANES].astype(F32)
    y = _gelu_tanh(jnp.concatenate([ys_scr[lg] for lg in range(S5_LANE_GROUPS)], axis=-1))
    y = y * _sigmoid(_dot(y.astype(BF16), wglu_ref[...]) + bglu_ref[...])
    y = (y * _silu(proj(R_GS, D_S5))).astype(BF16)
    merged = merged + gate(1) * _dot(y, wps_ref[...])

    qm = (proj(R_QM, D_MEM) * (MEM_HEAD_DIM ** -0.5)).astype(BF16)
    heads = []
    for hd in range(MEM_HEADS):
        cols = slice(hd * MEM_HEAD_DIM, (hd + 1) * MEM_HEAD_DIM)
        s = _dot_nt(qm[:, cols], mk_ref[:, cols])
        p = jnp.exp(s - jnp.max(s, axis=-1, keepdims=True))
        inv = 1.0 / jnp.sum(p, axis=-1, keepdims=True)
        heads.append(_dot(p.astype(BF16), mv_ref[:, cols]) * inv)
    ymem = jnp.concatenate(heads, axis=-1)
    a = (ymem * _silu(proj(R_GM, D_MEM))).astype(BF16)
    merged = merged + gate(2) * _dot(a, wpm_ref[...])

    out = x + _dot(merged.astype(BF16), wout_ref[...])
    o_ref[...] = _rms(out, gfin_ref[...])


def _final(x, yfox, yc, mk, mv, gn, gfin, wg, bm, wglu, bglu, wpf, wps, wpm, wout, *, tm):
    bn, l, _ = x.shape
    m = mk.shape[1]
    nc = tm // S5_CHUNK
    tok = lambda width: pl.BlockSpec((None, tm, width), lambda b, i: (b, i, 0))
    mem = pl.BlockSpec((None, m, D_MEM), lambda b, i: (b, 0, 0))
    return pl.pallas_call(
        functools.partial(_final_kernel, tm=tm),
        grid=(bn, l // tm),
        in_specs=[tok(D_MODEL), tok(D_FOX),
                  pl.BlockSpec((None, nc, S5_LANE_GROUPS * S5_CW), lambda b, i: (b, i, 0)),
                  mem, mem,
                  _const_spec((1, D_MODEL)), _const_spec((1, D_MODEL)),
                  _const_spec((D_MODEL, R_COLS)), _const_spec((1, 3 * D_MODEL)),
                  _const_spec((D_S5, D_S5)), _const_spec((1, D_S5)),
                  _const_spec((D_FOX, D_MODEL)), _const_spec((D_S5, D_MODEL)),
                  _const_spec((D_MEM, D_MODEL)), _const_spec((D_MODEL, D_MODEL))],
        out_specs=tok(D_MODEL),
        out_shape=jax.ShapeDtypeStruct(x.shape, x.dtype),
        scratch_shapes=[pltpu.VMEM((S5_LANE_GROUPS, tm, LANES), F32)],
        compiler_params=pltpu.CompilerParams(
            dimension_semantics=("arbitrary", "arbitrary"), vmem_limit_bytes=VMEM_LIMIT),
        name="final",
    )(x, yfox, yc, mk, mv, gn, gfin, wg, bm, wglu, bglu, wpf, wps, wpm, wout)


def _layer(x, mem, g_norm, g_mem_norm, w_in, b_forget, b_merge, w_mem_kv, lam_re, lam_im, log_step,
           s5_b_re, s5_b_im, s5_c_re, s5_c_im, s5_d, w_glu, b_glu, w_proj_fox, w_proj_s5, w_proj_mem,
           w_out, g_out, *, tm_in, t_attn, tm_out):
    bn, l, _ = x.shape
    assert tm_in == t_attn, "the per-tile q norms from in_proj are indexed by attention query tile"
    row = lambda a: a.reshape(1, -1).astype(F32)
    wqk, wvt, wfl_pad, wrest = _weight_prep(w_in)
    bfl_pad = jnp.pad(row(b_forget), ((0, 0), (0, LANES - FOX_HEADS)))

    mk, mv = _mem_kv(mem, row(g_mem_norm), w_mem_kv.astype(BF16))
    q, k, vt, uc, fqa, fka, fend, nq, nk = _in_proj(x, row(g_norm), wqk, wrest, wvt, wfl_pad, bfl_pad, tm=tm_in)
    yfox = _fox(q, fqa, k, fka, vt, *_fox_skip_tables(fend, nq, nk), t=t_attn)
    tables = _s5_tables(lam_re, lam_im, log_step, s5_b_re, s5_b_im, s5_c_re, s5_c_im, s5_d)
    yc = _s5(uc, *tables)
    return _final(x, yfox, yc, mk, mv, row(g_norm), row(g_out), wrest, row(b_merge),
                  w_glu.astype(BF16), row(b_glu), w_proj_fox.astype(BF16), w_proj_s5.astype(BF16),
                  w_proj_mem.astype(BF16), w_out.astype(BF16), tm=tm_out)


def kernel(x, mem, g_norm, g_mem_norm, g_final, w_in, b_forget, b_merge, w_mem_kv, lam_re, lam_im, log_step,
           s5_b_re, s5_b_im, s5_c_re, s5_c_im, s5_d, w_glu, b_glu, w_proj_fox, w_proj_s5, w_proj_mem, w_out):
    depth = w_in.shape[0]
    assert depth == 1, "the fused final kernel applies the closing RMSNorm: one layer only"
    l = x.shape[1]
    tm_in = min(512, l)
    t_attn = min(512, l)
    tm_out = min(512, l)
    return _layer(x, mem, g_norm[0], g_mem_norm[0], w_in, b_forget[0], b_merge[0], w_mem_kv[0],
                  lam_re[0], lam_im[0], log_step[0], s5_b_re[0], s5_b_im[0], s5_c_re[0], s5_c_im[0],
                  s5_d[0], w_glu[0], b_glu[0], w_proj_fox[0], w_proj_s5[0], w_proj_mem[0], w_out[0],
                  g_final, tm_in=tm_in, t_attn=t_attn, tm_out=tm_out)
```

```python
import functools
import math

import jax
import jax.numpy as jnp
import numpy as np
from jax import lax
from jax.experimental import pallas as pl
from jax.experimental.pallas import tpu as pltpu

F32 = jnp.float32
BF16 = jnp.bfloat16

D_MODEL = 1024
EPS = 1e-6
NEG = -1e30
LOG2E = math.log2(math.e)
SKIP_LOG2 = 127.0
NORM_MARGIN = 1.03

FOX_HEAD_DIM = 64
D_FOX = 768
FOX_HEADS = D_FOX // FOX_HEAD_DIM

D_S5 = 768
S5_GROUP = 16
S5_GROUPS = D_S5 // S5_GROUP
S5_STATE = 64

D_MEM = 512
MEM_HEADS = 4
MEM_HEAD_DIM = D_MEM // MEM_HEADS

LANES = 128
S5_CHUNK = 16
S5_LANE_GROUPS = D_S5 // LANES
S5_GROUPS_PER_LANE_GROUP = LANES // S5_GROUP
S5_CW = S5_CHUNK * LANES
S5_HALF = S5_GROUPS_PER_LANE_GROUP * S5_STATE
S5_SW = 2 * S5_HALF
S5_SEGMENTS = 8

W_FL = 3 * D_FOX
W_REST = W_FL + FOX_HEADS
R_GF, R_U, R_GS, R_QM, R_GM, R_GL = 0, 768, 1536, 2304, 2816, 3328
R_COLS = R_GL + 3 * D_MODEL

VMEM_LIMIT = 56 * 1024 * 1024


def _dot(a, b, **kw):
    return jnp.dot(a, b, preferred_element_type=F32, **kw)


def _dot_nt(a, b, **kw):
    return lax.dot_general(a, b, (((1,), (1,)), ((), ())), preferred_element_type=F32, **kw)


def _dot_nt_split(a, b):
    a_hi, b_hi = a.astype(BF16), b.astype(BF16)
    a_lo = (a - a_hi.astype(F32)).astype(BF16)
    b_lo = (b - b_hi.astype(F32)).astype(BF16)
    return _dot_nt(a_hi, b_hi) + (_dot_nt(a_hi, b_lo) + _dot_nt(a_lo, b_hi))


def _rms(xf, g):
    return xf * lax.rsqrt(jnp.mean(xf * xf, axis=-1, keepdims=True) + EPS) * g


def _sigmoid(z):
    return 1.0 / (1.0 + jnp.exp(-z))


def _silu(z):
    return z * _sigmoid(z)


def _gelu_tanh(y):
    return 0.5 * y * (1.0 + jnp.tanh(math.sqrt(2.0 / math.pi) * (y + 0.044715 * (y * y * y))))


def _split3(a):
    hi = a.astype(BF16)
    rest = a - hi.astype(F32)
    mid = rest.astype(BF16)
    lo = (rest - mid.astype(F32)).astype(BF16)
    return jnp.concatenate([hi, mid, lo], axis=-1)


def _const_spec(shape):
    nd = len(shape)
    return pl.BlockSpec(shape, lambda *_: (0,) * nd, pipeline_mode=pl.Buffered(1))


def _columns_kernel(wt_ref, o_ref, *, valid):
    cols = wt_ref[:, 0, :].T
    width = cols.shape[1]
    if valid < width:
        cols = jnp.where(lax.broadcasted_iota(jnp.int32, (1, width), 1) < valid, cols, 0.0)
    o_ref[...] = cols.astype(BF16)


def _rows_kernel(wt_ref, o_ref):
    o_ref[...] = wt_ref[:, 0, :].astype(BF16)


def _weight_prep(w_in):
    depth, d, n_in = w_in.shape
    assert depth == 1 and n_in == W_REST + R_COLS
    wt = jnp.transpose(w_in, (2, 0, 1))

    def columns(first, count, width, valid=None, name="weight_cols"):
        assert count % width == 0
        return pl.pallas_call(
            functools.partial(_columns_kernel, valid=width if valid is None else valid),
            grid=(count // width,),
            in_specs=[pl.BlockSpec((pl.Element(width), pl.Element(1), pl.Element(d)),
                                   lambda c: (first + width * c, 0, 0))],
            out_specs=pl.BlockSpec((d, width), lambda c: (0, c)),
            out_shape=jax.ShapeDtypeStruct((d, count), BF16),
            name=name,
        )(wt)

    wqk = columns(0, 2 * D_FOX, 2 * LANES, name="weight_qk")
    wfl = columns(W_FL, LANES, LANES, valid=FOX_HEADS, name="weight_forget")
    wrest = columns(W_REST, R_COLS, 2 * LANES, name="weight_rest")
    wvt = pl.pallas_call(
        _rows_kernel,
        grid=(D_FOX // LANES,),
        in_specs=[pl.BlockSpec((LANES, 1, d), lambda r: (2 * D_FOX // LANES + r, 0, 0))],
        out_specs=pl.BlockSpec((LANES, d), lambda r: (r, 0)),
        out_shape=jax.ShapeDtypeStruct((D_FOX, d), BF16),
        name="weight_vt",
    )(wt)
    return wqk, wvt, wfl, wrest


def _mem_kv_kernel(mem_ref, g_ref, w_ref, mk_ref, mv_ref):
    h = _rms(mem_ref[...], g_ref[...]).astype(BF16)
    kv = _dot(h, w_ref[...])
    mk_ref[...] = kv[:, :D_MEM].astype(BF16)
    mv_ref[...] = kv[:, D_MEM:].astype(BF16)


def _mem_kv(mem, g, w):
    bn, m, _ = mem.shape
    out = jax.ShapeDtypeStruct((bn, m, D_MEM), BF16)
    return pl.pallas_call(
        _mem_kv_kernel,
        grid=(bn,),
        in_specs=[pl.BlockSpec((None, m, D_MODEL), lambda b: (b, 0, 0)),
                  _const_spec((1, D_MODEL)),
                  _const_spec((D_MODEL, 2 * D_MEM))],
        out_specs=[pl.BlockSpec((None, m, D_MEM), lambda b: (b, 0, 0))] * 2,
        out_shape=[out, out],
        name="mem_kv",
    )(mem, g, w)


def _in_proj_kernel(x_ref, g_ref, w_ref, wu_ref, wvt_ref, wfl_ref, bfl_ref, spread_ref, ones_ref,
                    q_ref, k_ref, vt_ref, uc_ref, fqa_ref, fka_ref, fend_ref, nq_ref, nk_ref,
                    us_scr, carry_scr, *, tm):
    @pl.when(pl.program_id(1) == 0)
    def _():
        carry_scr[...] = jnp.zeros_like(carry_scr)

    h = _rms(x_ref[...], g_ref[...]).astype(BF16)
    qb = (_dot(h, w_ref[:, 0:D_FOX]) * (LOG2E * FOX_HEAD_DIM ** -0.5)).astype(BF16)
    kb = _dot(h, w_ref[:, D_FOX:2 * D_FOX]).astype(BF16)
    q_ref[...] = qb
    k_ref[...] = kb
    for pair in range(FOX_HEADS // 2):
        vt_ref[pair] = _dot_nt(wvt_ref[pair * LANES:(pair + 1) * LANES, :], h).astype(BF16)

    head_of_col = lax.broadcasted_iota(jnp.int32, (D_FOX, LANES), 0) // FOX_HEAD_DIM
    sel = (head_of_col == lax.broadcasted_iota(jnp.int32, (D_FOX, LANES), 1)).astype(BF16)
    sq = lambda a: (a.astype(F32) * a.astype(F32)).astype(BF16)
    nq_ref[...] = jnp.max(_dot(sq(qb), sel), axis=0, keepdims=True)
    nk_ref[...] = jnp.max(_dot(sq(kb), sel), axis=0, keepdims=True)

    u = _dot(h, wu_ref[...])
    nc = tm // S5_CHUNK
    for lg in range(S5_LANE_GROUPS):
        us_scr[lg] = u[:, lg * LANES:(lg + 1) * LANES]
        for j in range(S5_CHUNK):
            c0 = lg * S5_CW + j * LANES
            uc_ref[:, c0:c0 + LANES] = us_scr[lg, pl.ds(j, nc, stride=S5_CHUNK), :].astype(BF16)

    z = _dot(h, wfl_ref[...]) + bfl_ref[...]
    logf = (jnp.minimum(z, 0.0) - jnp.log1p(jnp.exp(-jnp.abs(z)))) * LOG2E
    row = lax.broadcasted_iota(jnp.int32, (tm, tm), 0)
    col = lax.broadcasted_iota(jnp.int32, (tm, tm), 1)
    tri = (col <= row).astype(BF16)
    parts = _dot(tri, _split3(logf))
    f = (parts[:, 0:LANES] + parts[:, LANES:2 * LANES]) + parts[:, 2 * LANES:] + carry_scr[...]
    carry_scr[...] = f[tm - 1:tm, :]
    fend_ref[...] = f[tm - 1:tm, :]
    aug = _dot(_split3(f), spread_ref[...]) + ones_ref[...]
    fka_ref[...] = aug[:, 0:LANES].astype(BF16)
    fqa_ref[...] = aug[:, LANES:].astype(BF16)


def _fox_aug_constants():
    spread = np.zeros((3 * LANES, 2 * LANES), np.float32)
    ones = np.zeros((1, 2 * LANES), np.float32)
    half = LANES // 2
    for h in range(FOX_HEADS):
        for i in range(3):
            spread[i * LANES + h, 3 * h + i] = 1.0
            spread[i * LANES + h, LANES + half + 3 * h + i] = 1.0
            ones[0, half + 3 * h + i] = 1.0
            ones[0, LANES + 3 * h + i] = -1.0
    return jnp.asarray(spread, BF16), jnp.asarray(ones, F32)


def _in_proj(x, g, wqk, wrest, wvt, wfl, bfl, *, tm):
    bn, l, _ = x.shape
    nc = tm // S5_CHUNK
    nt = l // tm
    pairs = FOX_HEADS // 2
    spread, ones = _fox_aug_constants()
    tok = lambda width: pl.BlockSpec((None, tm, width), lambda b, i: (b, i, 0))
    per_tile = pl.BlockSpec((None, None, 1, LANES), lambda b, i: (b, i, 0, 0))
    qk = jax.ShapeDtypeStruct((bn, l, D_FOX), BF16)
    aug = jax.ShapeDtypeStruct((bn, l, LANES), BF16)
    stat = jax.ShapeDtypeStruct((bn, nt, 1, LANES), F32)
    return pl.pallas_call(
        functools.partial(_in_proj_kernel, tm=tm),
        grid=(bn, nt),
        in_specs=[tok(D_MODEL),
                  _const_spec((1, D_MODEL)),
                  _const_spec((D_MODEL, 2 * D_FOX)),
                  pl.BlockSpec((D_MODEL, D_S5), lambda *_: (0, R_U // D_S5), pipeline_mode=pl.Buffered(1)),
                  _const_spec((D_FOX, D_MODEL)),
                  _const_spec((D_MODEL, LANES)),
                  _const_spec((1, LANES)),
                  _const_spec((3 * LANES, 2 * LANES)),
                  _const_spec((1, 2 * LANES))],
        out_specs=[tok(D_FOX), tok(D_FOX),
                   pl.BlockSpec((None, pairs, None, LANES, tm), lambda b, i: (b, 0, i, 0, 0)),
                   pl.BlockSpec((None, nc, S5_LANE_GROUPS * S5_CW), lambda b, i: (b, i, 0)),
                   tok(LANES), tok(LANES), per_tile, per_tile, per_tile],
        out_shape=[qk, qk,
                   jax.ShapeDtypeStruct((bn, pairs, nt, LANES, tm), BF16),
                   jax.ShapeDtypeStruct((bn, l // S5_CHUNK, S5_LANE_GROUPS * S5_CW), BF16),
                   aug, aug, stat, stat, stat],
        scratch_shapes=[pltpu.VMEM((S5_LANE_GROUPS, tm, LANES), F32), pltpu.VMEM((1, LANES), F32)],
        compiler_params=pltpu.CompilerParams(
            dimension_semantics=("arbitrary", "arbitrary"), vmem_limit_bytes=VMEM_LIMIT),
        name="in_proj",
    )(x, g, wqk, wrest, wvt, wfl, bfl, spread, ones)


def _fox_kernel(fend_ref, qkb_ref, q_ref, fqa_ref, k_ref, fka_ref, vt_ref, o_ref,
                m_scr, l_scr, alpha_scr, acc_scr, s_scr, p_scr, smax_scr, *, t, nt):
    b = pl.program_id(0)
    pair = pl.program_id(1)
    qi = pl.program_id(2)
    lane = lax.broadcasted_iota(jnp.int32, (1, LANES), 1)
    q = q_ref[...]
    fqa = fqa_ref[...]

    heads = (0, 1)
    q_rows, base = [], []
    for e in heads:
        head = 2 * pair + e
        head_lanes = (lane >= e * FOX_HEAD_DIM) & (lane < (e + 1) * FOX_HEAD_DIM)
        f_lanes = lax.rem(lane, LANES // 2) // 3 == head
        q_rows.append(jnp.concatenate([jnp.where(head_lanes, q, jnp.zeros_like(q)),
                                       jnp.where(f_lanes, fqa, jnp.zeros_like(fqa))], axis=-1))
        base.append((b * FOX_HEADS + head) * nt)
    q_aug = jnp.concatenate(q_rows, axis=0)

    def span(which):
        return slice(which[0] * t, (which[-1] + 1) * t)

    def scores(which, j):
        k0 = pl.multiple_of(jnp.maximum(j, 0) * t, t)
        k_aug = jnp.concatenate([k_ref[pl.ds(k0, t), :], fka_ref[pl.ds(k0, t), :]], axis=-1)
        return _dot_nt(k_aug, q_aug[span(which)])

    def stage_scores(which, j, slot):
        s = scores(which, j)
        s_scr[slot, :, span(which)] = s
        smax_scr[slot, :, span(which)] = jnp.max(s, axis=0, keepdims=True)

    def weighted_values(which, j, slot):
        c = span(which)
        acc_scr[:, c] = alpha_scr[slot, :, c] * acc_scr[:, c] + _dot(vt_ref[j], p_scr[slot, :, c])

    def softmax(which, s_slot, p_slot):
        c = span(which)
        m_old = m_scr[:, c]
        m_new = jnp.maximum(m_old, smax_scr[s_slot, :, c])
        p = jnp.exp2(s_scr[s_slot, :, c] - m_new)
        alpha = jnp.exp2(m_old - m_new)
        m_scr[:, c] = m_new
        l_scr[:, c] = alpha * l_scr[:, c] + jnp.sum(p, axis=0, keepdims=True)
        p_scr[p_slot, :, c] = p.astype(BF16)
        alpha_scr[p_slot, :, c] = alpha

    key_pos = lax.broadcasted_iota(jnp.int32, (t, 2 * t), 0)
    query_pos = lax.rem(lax.broadcasted_iota(jnp.int32, (t, 2 * t), 1), t)
    s = jnp.where(key_pos <= query_pos, scores(heads, qi), NEG)
    s_scr[0] = s
    smax_scr[0] = jnp.max(s, axis=0, keepdims=True)
    m_scr[...] = jnp.full_like(m_scr, NEG)
    l_scr[...] = jnp.zeros_like(l_scr)
    acc_scr[...] = jnp.zeros_like(acc_scr)
    softmax(heads, 0, 0)
    m = m_scr[...]
    slack = [qkb_ref[base[e] + qi] - jnp.min(m[:, span((e,))]) + SKIP_LOG2 for e in heads]
    stage_scores(heads, qi - 1, 0)

    def wanted(e, j):
        f_hi = fend_ref[base[e] + jnp.maximum(qi - 1, 0)]
        reach = f_hi - fend_ref[base[e] + jnp.maximum(j, 0)] + slack[e]
        return jnp.logical_and(j >= 0, reach >= 0.0).astype(jnp.int32)

    def trip(j, which, tiles):
        assert tiles == 1 or tiles % 2 == 0
        for i in range(tiles):
            here, there = i % 2, (i + 1) % 2 if tiles > 1 else 0
            weighted_values(which, j + 1 - i, here)
            softmax(which, here, there)
            stage_scores(which, j - 1 - i, there)

    def walk(j, which, tiles):
        def go(jj):
            flag = wanted(which[0], jj - (tiles - 1))
            for e in which[1:]:
                flag = flag * wanted(e, jj - (tiles - 1))
            return flag

        def body(carry):
            trip(carry[0], which, tiles)
            return carry[0] - tiles, go(carry[0] - tiles)

        return lax.while_loop(lambda c: c[1] > 0, body, (j, go(j)))[0]

    j_both = walk(qi - 1, heads, 1)
    for e in heads:
        j_stop = walk(walk(j_both, (e,), 2), (e,), 1)
        weighted_values((e,), j_stop + 1, 0)
    out = acc_scr[...] * (1.0 / l_scr[...])
    dim = lax.broadcasted_iota(jnp.int32, (LANES, 1), 0)
    o_ref[...] = jnp.where(dim < FOX_HEAD_DIM, out[:, span((0,))], out[:, span((1,))]).T.astype(BF16)


def _fox(q, fqa, k, fka, vt, fend, qkb, *, t):
    bn, l, _ = q.shape
    pairs = FOX_HEADS // 2
    nt = l // t
    query_tile = pl.BlockSpec((None, t, LANES), lambda b, p, i, *_: (b, i, p))
    grid_spec = pltpu.PrefetchScalarGridSpec(
        num_scalar_prefetch=2,
        grid=(bn, pairs, nt),
        in_specs=[query_tile,
                  pl.BlockSpec((None, t, LANES), lambda b, p, i, *_: (b, i, 0)),
                  pl.BlockSpec((None, l, LANES), lambda b, p, i, *_: (b, 0, p)),
                  pl.BlockSpec((None, l, LANES), lambda b, p, i, *_: (b, 0, 0)),
                  pl.BlockSpec((None, None, nt, LANES, t), lambda b, p, i, *_: (b, p, 0, 0, 0))],
        out_specs=query_tile,
        scratch_shapes=[pltpu.VMEM((1, 2 * t), F32), pltpu.VMEM((1, 2 * t), F32), pltpu.VMEM((2, 1, 2 * t), F32),
                        pltpu.VMEM((LANES, 2 * t), F32), pltpu.VMEM((2, t, 2 * t), F32),
                        pltpu.VMEM((2, t, 2 * t), BF16), pltpu.VMEM((2, 1, 2 * t), F32)])
    return pl.pallas_call(
        functools.partial(_fox_kernel, t=t, nt=nt),
        grid_spec=grid_spec,
        out_shape=jax.ShapeDtypeStruct((bn, l, D_FOX), BF16),
        compiler_params=pltpu.CompilerParams(
            dimension_semantics=("arbitrary", "arbitrary", "arbitrary"), vmem_limit_bytes=VMEM_LIMIT),
        name="fox",
    )(fend, qkb, q, fqa, k, fka, vt)


def _fox_skip_tables(fend, nq, nk):
    per_head = lambda a: jnp.swapaxes(a[:, :, 0, :FOX_HEADS], 1, 2)
    qmax = jnp.sqrt(per_head(nq))
    kmax = jnp.sqrt(jnp.max(per_head(nk), axis=2, keepdims=True))
    return per_head(fend).reshape(-1), (NORM_MARGIN * qmax * kmax).reshape(-1)


def _s5_tables_kernel(lam_in_ref, mats_ref, d_ref, wtoe_ref, win_ref, voutt_ref, lam_ref):
    lr, li = lam_in_ref[0], lam_in_ref[1]
    step = jnp.exp(lam_in_ref[2])
    mag = jnp.exp(lr * step)
    ab_re, ab_im = mag * jnp.cos(li * step), mag * jnp.sin(li * step)
    den = lr * lr + li * li
    nr, ni = ab_re - 1.0, ab_im
    f_re = (nr * lr + ni * li) / den
    f_im = (ni * lr - nr * li) / den
    btr, bti = mats_ref[0], mats_ref[1]
    bb_re = f_re * btr - f_im * bti
    bb_im = f_re * bti + f_im * btr
    cr, ci = mats_ref[2], mats_ref[3]
    eye = (lax.broadcasted_iota(jnp.int32, (LANES, LANES), 0)
           == lax.broadcasted_iota(jnp.int32, (LANES, LANES), 1))
    tile = lambda j: slice(j * LANES, (j + 1) * LANES)
    zeros = jnp.zeros((LANES, LANES), BF16)
    for jp in range(S5_CHUNK):
        for j in range(jp):
            wtoe_ref[tile(jp), tile(j)] = zeros
    pr = jnp.ones_like(lr)
    pi = jnp.zeros_like(lr)
    for tau in range(S5_CHUNK + 1):
        a_re = cr * pr - ci * pi
        a_im = cr * pi + ci * pr
        if tau < S5_CHUNK:
            kt = _dot_nt_split(bb_re, a_re) - _dot_nt_split(bb_im, a_im)
            if tau == 0:
                kt = kt + jnp.where(eye, d_ref[...], 0.0)
            kt = kt.astype(BF16)
            for jp in range(S5_CHUNK - tau):
                wtoe_ref[tile(jp), tile(jp + tau)] = kt
            j = S5_CHUNK - 1 - tau
            win_ref[tile(j), 0:S5_HALF] = (pr * bb_re - pi * bb_im).astype(BF16)
            win_ref[tile(j), S5_HALF:] = (pr * bb_im + pi * bb_re).astype(BF16)
        if tau >= 1:
            voutt_ref[tile(tau - 1), 0:S5_HALF] = a_re.astype(BF16)
            voutt_ref[tile(tau - 1), S5_HALF:] = (-a_im).astype(BF16)
        if tau == S5_CHUNK:
            lam_ref[:, 0:S5_HALF] = pr
            lam_ref[:, S5_HALF:] = pi
        pr, pi = pr * ab_re - pi * ab_im, pr * ab_im + pi * ab_re


def _s5_tables(lam_re, lam_im, log_step, b_re, b_im, c_re, c_im, d):
    lg, gp = S5_LANE_GROUPS, S5_GROUPS_PER_LANE_GROUP
    eye = jnp.eye(gp, dtype=F32)

    def block_diag(a):
        a = a.astype(F32).reshape(4, lg, gp, S5_GROUP, S5_STATE)
        return jnp.einsum("kGgcp,gh->kGgchp", a, eye).reshape(4, lg, LANES, S5_HALF)

    step = jnp.broadcast_to(log_step.astype(F32)[:, None], (S5_GROUPS, S5_STATE))
    lam = jnp.stack([lam_re.astype(F32), lam_im.astype(F32), step]).reshape(3, lg, 1, S5_HALF)
    mats = block_diag(jnp.stack([jnp.swapaxes(b_re, 1, 2), jnp.swapaxes(b_im, 1, 2), c_re, c_im]))
    spec = lambda r, c: pl.BlockSpec((None, r, c), lambda g: (g, 0, 0))
    stacked = lambda k, r, c: pl.BlockSpec((k, None, r, c), lambda g: (0, g, 0, 0))
    return pl.pallas_call(
        _s5_tables_kernel,
        grid=(lg,),
        in_specs=[stacked(3, 1, S5_HALF), stacked(4, LANES, S5_HALF), spec(1, LANES)],
        out_specs=[spec(S5_CW, S5_CW), spec(S5_CW, S5_SW), spec(S5_CW, S5_SW), spec(1, S5_SW)],
        out_shape=[jax.ShapeDtypeStruct((lg, S5_CW, S5_CW), BF16),
                   jax.ShapeDtypeStruct((lg, S5_CW, S5_SW), BF16),
                   jax.ShapeDtypeStruct((lg, S5_CW, S5_SW), BF16),
                   jax.ShapeDtypeStruct((lg, 1, S5_SW), F32)],
        compiler_params=pltpu.CompilerParams(
            dimension_semantics=("arbitrary",), vmem_limit_bytes=VMEM_LIMIT),
        name="s5_tables",
    )(lam, mats, d.astype(F32).reshape(lg, 1, LANES))


def _s5_kernel(uc_ref, wtoe_ref, win_ref, voutt_ref, lam_ref, yc_ref, e_scr, hs_scr, *, n):
    x = uc_ref[...]
    e = _dot(x, win_ref[...])
    tiles = S5_SW // LANES
    half = tiles // 2
    lane_tile = lambda c: slice(c * LANES, (c + 1) * LANES)
    for c in range(tiles):
        e_scr[c, 0:n, :] = e[:, lane_tile(c)]
        e_scr[c, n:, :] = jnp.zeros((S5_SEGMENTS, LANES), F32)
    lam = jnp.broadcast_to(lam_ref[...], (S5_SEGMENTS, S5_SW))
    lam_r = [lam[:, lane_tile(c)] for c in range(half)]
    lam_i = [lam[:, lane_tile(half + c)] for c in range(half)]
    cmul = lambda ar, ai, br, bi: (ar * br - ai * bi, ar * bi + ai * br)

    seg = n // S5_SEGMENTS + 1
    run_rows = lambda r: pl.ds(r, S5_SEGMENTS, stride=seg)
    zero = jnp.zeros((S5_SEGMENTS, LANES), F32)
    one = jnp.ones((S5_SEGMENTS, LANES), F32)
    zero_row = jnp.zeros((1, LANES), F32)

    def scan_runs(r, carry):
        h, pw = carry
        h_next, pw_next = [None] * tiles, [None] * tiles
        for c in range(half):
            hs_scr[c, run_rows(r), :] = h[c]
            hs_scr[half + c, run_rows(r), :] = h[half + c]
            nr, ni = cmul(lam_r[c], lam_i[c], h[c], h[half + c])
            h_next[c] = nr + e_scr[c, run_rows(r), :]
            h_next[half + c] = ni + e_scr[half + c, run_rows(r), :]
            pw_next[c], pw_next[half + c] = cmul(lam_r[c], lam_i[c], pw[c], pw[half + c])
        return tuple(h_next), tuple(pw_next)

    unit = (one,) * half + (zero,) * half
    run_end, run_mult = lax.fori_loop(0, seg, scan_runs, ((zero,) * tiles, unit))

    start = [None] * tiles
    for c in range(half):
        sr, si, rows_r, rows_i = zero_row, zero_row, [], []
        for s in range(S5_SEGMENTS):
            rows_r.append(sr)
            rows_i.append(si)
            nr, ni = cmul(run_mult[c][0:1], run_mult[half + c][0:1], sr, si)
            sr, si = nr + run_end[c][s:s + 1], ni + run_end[half + c][s:s + 1]
        start[c], start[half + c] = jnp.concatenate(rows_r, axis=0), jnp.concatenate(rows_i, axis=0)

    def add_run_starts(r, pw):
        pw_next = [None] * tiles
        for c in range(half):
            ar, ai = cmul(pw[c], pw[half + c], start[c], start[half + c])
            hs_scr[c, run_rows(r), :] = hs_scr[c, run_rows(r), :] + ar
            hs_scr[half + c, run_rows(r), :] = hs_scr[half + c, run_rows(r), :] + ai
            pw_next[c], pw_next[half + c] = cmul(lam_r[c], lam_i[c], pw[c], pw[half + c])
        return tuple(pw_next)

    lax.fori_loop(0, seg, add_run_starts, unit)
    hs = jnp.concatenate([hs_scr[c, 0:n, :] for c in range(tiles)], axis=-1).astype(BF16)
    blk = 2 * LANES
    for jb in range(S5_CW // blk):
        cols = slice(jb * blk, (jb + 1) * blk)
        kk = (jb + 1) * blk
        y = _dot(x[:, :kk], wtoe_ref[0:kk, cols]) + _dot_nt(hs, voutt_ref[cols, :])
        yc_ref[:, cols] = y.astype(BF16)


def _s5(uc, wtoe, win, voutt, lam):
    bn, n, _ = uc.shape
    per_lg = lambda r, c: pl.BlockSpec((None, r, c), lambda g, b: (g, 0, 0))
    return pl.pallas_call(
        functools.partial(_s5_kernel, n=n),
        grid=(S5_LANE_GROUPS, bn),
        in_specs=[pl.BlockSpec((None, n, S5_CW), lambda g, b: (b, 0, g)),
                  per_lg(S5_CW, S5_CW), per_lg(S5_CW, S5_SW), per_lg(S5_CW, S5_SW), per_lg(1, S5_SW)],
        out_specs=pl.BlockSpec((None, n, S5_CW), lambda g, b: (b, 0, g)),
        out_shape=jax.ShapeDtypeStruct(uc.shape, BF16),
        scratch_shapes=[pltpu.VMEM((S5_SW // LANES, n + S5_SEGMENTS, LANES), F32)] * 2,
        compiler_params=pltpu.CompilerParams(
            dimension_semantics=("arbitrary", "arbitrary"), vmem_limit_bytes=VMEM_LIMIT),
        name="s5",
    )(uc, wtoe, win, voutt, lam)


def _final_kernel(x_ref, yfox_ref, yc_ref, mk_ref, mv_ref, gn_ref, gfin_ref, wg_ref, bm_ref,
                  wglu_ref, bglu_ref, wpf_ref, wps_ref, wpm_ref, wout_ref, o_ref, ys_scr, *, tm):
    x = x_ref[...]
    h = _rms(x, gn_ref[...]).astype(BF16)

    def proj(c0, width):
        return _dot(h, wg_ref[:, c0:c0 + width])

    def gate(i):
        return _sigmoid(proj(R_GL + i * D_MODEL, D_MODEL) + bm_ref[:, i * D_MODEL:(i + 1) * D_MODEL])

    a = (yfox_ref[...].astype(F32) * _silu(proj(R_GF, D_FOX))).astype(BF16)
    merged = gate(0) * _dot(a, wpf_ref[...])

    nc = tm // S5_CHUNK
    for lg in range(S5_LANE_GROUPS):
        for j in range(S5_CHUNK):
            c0 = lg * S5_CW + j * LANES
            ys_scr[lg, pl.ds(j, nc, stride=S5_CHUNK), :] = yc_ref[:, c0:c0 + LANES].astype(F32)
    y = _gelu_tanh(jnp.concatenate([ys_scr[lg] for lg in range(S5_LANE_GROUPS)], axis=-1))
    y = y * _sigmoid(_dot(y.astype(BF16), wglu_ref[...]) + bglu_ref[...])
    y = (y * _silu(proj(R_GS, D_S5))).astype(BF16)
    merged = merged + gate(1) * _dot(y, wps_ref[...])

    qm = (proj(R_QM, D_MEM) * (MEM_HEAD_DIM ** -0.5)).astype(BF16)
    heads = []
    for hd in range(MEM_HEADS):
        cols = slice(hd * MEM_HEAD_DIM, (hd + 1) * MEM_HEAD_DIM)
        s = _dot_nt(qm[:, cols], mk_ref[:, cols])
        p = jnp.exp(s - jnp.max(s, axis=-1, keepdims=True))
        inv = 1.0 / jnp.sum(p, axis=-1, keepdims=True)
        heads.append(_dot(p.astype(BF16), mv_ref[:, cols]) * inv)
    ymem = jnp.concatenate(heads, axis=-1)
    a = (ymem * _silu(proj(R_GM, D_MEM))).astype(BF16)
    merged = merged + gate(2) * _dot(a, wpm_ref[...])

    out = x + _dot(merged.astype(BF16), wout_ref[...])
    o_ref[...] = _rms(out, gfin_ref[...])


def _final(x, yfox, yc, mk, mv, gn, gfin, wg, bm, wglu, bglu, wpf, wps, wpm, wout, *, tm):
    bn, l, _ = x.shape
    m = mk.shape[1]
    nc = tm // S5_CHUNK
    tok = lambda width: pl.BlockSpec((None, tm, width), lambda b, i: (b, i, 0))
    mem = pl.BlockSpec((None, m, D_MEM), lambda b, i: (b, 0, 0))
    return pl.pallas_call(
        functools.partial(_final_kernel, tm=tm),
        grid=(bn, l // tm),
        in_specs=[tok(D_MODEL), tok(D_FOX),
                  pl.BlockSpec((None, nc, S5_LANE_GROUPS * S5_CW), lambda b, i: (b, i, 0)),
                  mem, mem,
                  _const_spec((1, D_MODEL)), _const_spec((1, D_MODEL)),
                  _const_spec((D_MODEL, R_COLS)), _const_spec((1, 3 * D_MODEL)),
                  _const_spec((D_S5, D_S5)), _const_spec((1, D_S5)),
                  _const_spec((D_FOX, D_MODEL)), _const_spec((D_S5, D_MODEL)),
                  _const_spec((D_MEM, D_MODEL)), _const_spec((D_MODEL, D_MODEL))],
        out_specs=tok(D_MODEL),
        out_shape=jax.ShapeDtypeStruct(x.shape, x.dtype),
        scratch_shapes=[pltpu.VMEM((S5_LANE_GROUPS, tm, LANES), F32)],
        compiler_params=pltpu.CompilerParams(
            dimension_semantics=("arbitrary", "arbitrary"), vmem_limit_bytes=VMEM_LIMIT),
        name="final",
    )(x, yfox, yc, mk, mv, gn, gfin, wg, bm, wglu, bglu, wpf, wps, wpm, wout)


def _layer(x, mem, g_norm, g_mem_norm, w_in, b_forget, b_merge, w_mem_kv, lam_re, lam_im, log_step,
           s5_b_re, s5_b_im, s5_c_re, s5_c_im, s5_d, w_glu, b_glu, w_proj_fox, w_proj_s5, w_proj_mem,
           w_out, g_out, *, tm_in, t_attn, tm_out):
    bn, l, _ = x.shape
    assert tm_in == t_attn, "the per-tile q norms from in_proj are indexed by attention query tile"
    row = lambda a: a.reshape(1, -1).astype(F32)
    wqk, wvt, wfl_pad, wrest = _weight_prep(w_in)
    bfl_pad = jnp.pad(row(b_forget), ((0, 0), (0, LANES - FOX_HEADS)))

    mk, mv = _mem_kv(mem, row(g_mem_norm), w_mem_kv.astype(BF16))
    q, k, vt, uc, fqa, fka, fend, nq, nk = _in_proj(x, row(g_norm), wqk, wrest, wvt, wfl_pad, bfl_pad, tm=tm_in)
    yfox = _fox(q, fqa, k, fka, vt, *_fox_skip_tables(fend, nq, nk), t=t_attn)
    tables = _s5_tables(lam_re, lam_im, log_step, s5_b_re, s5_b_im, s5_c_re, s5_c_im, s5_d)
    yc = _s5(uc, *tables)
    return _final(x, yfox, yc, mk, mv, row(g_norm), row(g_out), wrest, row(b_merge),
                  w_glu.astype(BF16), row(b_glu), w_proj_fox.astype(BF16), w_proj_s5.astype(BF16),
                  w_proj_mem.astype(BF16), w_out.astype(BF16), tm=tm_out)


def kernel(x, mem, g_norm, g_mem_norm, g_final, w_in, b_forget, b_merge, w_mem_kv, lam_re, lam_im, log_step,
           s5_b_re, s5_b_im, s5_c_re, s5_c_im, s5_d, w_glu, b_glu, w_proj_fox, w_proj_s5, w_proj_mem, w_out):
    depth = w_in.shape[0]
    assert depth == 1, "the fused final kernel applies the closing RMSNorm: one layer only"
    l = x.shape[1]
    tm_in = min(512, l)
    t_attn = min(512, l)
    tm_out = min(512, l)
    return _layer(x, mem, g_norm[0], g_mem_norm[0], w_in, b_forget[0], b_merge[0], w_mem_kv[0],
                  lam_re[0], lam_im[0], log_step[0], s5_b_re[0], s5_b_im[0], s5_c_re[0], s5_c_im[0],
                  s5_d[0], w_glu[0], b_glu[0], w_proj_fox[0], w_proj_s5[0], w_proj_mem[0], w_out[0],
                  g_final, tm_in=tm_in, t_attn=t_attn, tm_out=tm_out)
```

```python
import functools
import math

import jax
import jax.numpy as jnp
import numpy as np
from jax import lax
from jax.experimental import pallas as pl
from jax.experimental.pallas import tpu as pltpu

F32 = jnp.float32
BF16 = jnp.bfloat16

D_MODEL = 1024
EPS = 1e-6
NEG = -1e30
LOG2E = math.log2(math.e)
SKIP_LOG2 = 127.0
NORM_MARGIN = 1.03

FOX_HEAD_DIM = 64
D_FOX = 768
FOX_HEADS = D_FOX // FOX_HEAD_DIM

D_S5 = 768
S5_GROUP = 16
S5_GROUPS = D_S5 // S5_GROUP
S5_STATE = 64

D_MEM = 512
MEM_HEADS = 4
MEM_HEAD_DIM = D_MEM // MEM_HEADS

LANES = 128
S5_CHUNK = 16
S5_LANE_GROUPS = D_S5 // LANES
S5_GROUPS_PER_LANE_GROUP = LANES // S5_GROUP
S5_CW = S5_CHUNK * LANES
S5_HALF = S5_GROUPS_PER_LANE_GROUP * S5_STATE
S5_SW = 2 * S5_HALF
S5_SEGMENTS = 8

W_FL = 3 * D_FOX
W_REST = W_FL + FOX_HEADS
R_GF, R_U, R_GS, R_QM, R_GM, R_GL = 0, 768, 1536, 2304, 2816, 3328
R_COLS = R_GL + 3 * D_MODEL

VMEM_LIMIT = 56 * 1024 * 1024
MERGE_BLOCK = 256


def _dot(a, b, **kw):
    return jnp.dot(a, b, preferred_element_type=F32, **kw)


def _dot_nt(a, b, **kw):
    return lax.dot_general(a, b, (((1,), (1,)), ((), ())), preferred_element_type=F32, **kw)


def _dot_nt_split(a, b):
    a_hi, b_hi = a.astype(BF16), b.astype(BF16)
    a_lo = (a - a_hi.astype(F32)).astype(BF16)
    b_lo = (b - b_hi.astype(F32)).astype(BF16)
    return _dot_nt(a_hi, b_hi) + (_dot_nt(a_hi, b_lo) + _dot_nt(a_lo, b_hi))


def _rms(xf, g):
    return xf * lax.rsqrt(jnp.mean(xf * xf, axis=-1, keepdims=True) + EPS) * g


def _sigmoid(z):
    return 1.0 / (1.0 + jnp.exp(-z))


def _silu(z):
    return z * _sigmoid(z)


def _gelu_tanh(y):
    return 0.5 * y * (1.0 + jnp.tanh(math.sqrt(2.0 / math.pi) * (y + 0.044715 * (y * y * y))))


def _split3(a):
    hi = a.astype(BF16)
    rest = a - hi.astype(F32)
    mid = rest.astype(BF16)
    lo = (rest - mid.astype(F32)).astype(BF16)
    return jnp.concatenate([hi, mid, lo], axis=-1)


def _const_spec(shape):
    nd = len(shape)
    return pl.BlockSpec(shape, lambda *_: (0,) * nd, pipeline_mode=pl.Buffered(1))


def _columns_kernel(wt_ref, o_ref, *, valid):
    cols = wt_ref[:, 0, :].T
    width = cols.shape[1]
    if valid < width:
        cols = jnp.where(lax.broadcasted_iota(jnp.int32, (1, width), 1) < valid, cols, 0.0)
    o_ref[...] = cols.astype(BF16)


def _rows_kernel(wt_ref, o_ref):
    o_ref[...] = wt_ref[:, 0, :].astype(BF16)


def _weight_prep(w_in):
    depth, d, n_in = w_in.shape
    assert depth == 1 and n_in == W_REST + R_COLS
    wt = jnp.transpose(w_in, (2, 0, 1))

    def columns(first, count, width, valid=None, name="weight_cols"):
        assert count % width == 0
        return pl.pallas_call(
            functools.partial(_columns_kernel, valid=width if valid is None else valid),
            grid=(count // width,),
            in_specs=[pl.BlockSpec((pl.Element(width), pl.Element(1), pl.Element(d)),
                                   lambda c: (first + width * c, 0, 0))],
            out_specs=pl.BlockSpec((d, width), lambda c: (0, c)),
            out_shape=jax.ShapeDtypeStruct((d, count), BF16),
            name=name,
        )(wt)

    wqk = columns(0, 2 * D_FOX, 2 * LANES, name="weight_qk")
    wfl = columns(W_FL, LANES, LANES, valid=FOX_HEADS, name="weight_forget")
    wrest = columns(W_REST, R_COLS, 2 * LANES, name="weight_rest")
    wvt = pl.pallas_call(
        _rows_kernel,
        grid=(D_FOX // LANES,),
        in_specs=[pl.BlockSpec((LANES, 1, d), lambda r: (2 * D_FOX // LANES + r, 0, 0))],
        out_specs=pl.BlockSpec((LANES, d), lambda r: (r, 0)),
        out_shape=jax.ShapeDtypeStruct((D_FOX, d), BF16),
        name="weight_vt",
    )(wt)
    return wqk, wvt, wfl, wrest


def _mem_kv_kernel(mem_ref, g_ref, w_ref, mk_ref, mv_ref):
    h = _rms(mem_ref[...], g_ref[...]).astype(BF16)
    kv = _dot(h, w_ref[...])
    mk_ref[...] = kv[:, :D_MEM].astype(BF16)
    mv_ref[...] = kv[:, D_MEM:].astype(BF16)


def _mem_kv(mem, g, w):
    bn, m, _ = mem.shape
    out = jax.ShapeDtypeStruct((bn, m, D_MEM), BF16)
    return pl.pallas_call(
        _mem_kv_kernel,
        grid=(bn,),
        in_specs=[pl.BlockSpec((None, m, D_MODEL), lambda b: (b, 0, 0)),
                  _const_spec((1, D_MODEL)),
                  _const_spec((D_MODEL, 2 * D_MEM))],
        out_specs=[pl.BlockSpec((None, m, D_MEM), lambda b: (b, 0, 0))] * 2,
        out_shape=[out, out],
        name="mem_kv",
    )(mem, g, w)


def _in_proj_kernel(x_ref, g_ref, w_ref, wu_ref, wvt_ref, wfl_ref, bfl_ref, spread_ref, ones_ref,
                    q_ref, k_ref, vt_ref, uc_ref, fqa_ref, fka_ref, fend_ref, nq_ref, nk_ref,
                    us_scr, carry_scr, *, tm):
    @pl.when(pl.program_id(1) == 0)
    def _():
        carry_scr[...] = jnp.zeros_like(carry_scr)

    h = _rms(x_ref[...], g_ref[...]).astype(BF16)
    qb = (_dot(h, w_ref[:, 0:D_FOX]) * (LOG2E * FOX_HEAD_DIM ** -0.5)).astype(BF16)
    kb = _dot(h, w_ref[:, D_FOX:2 * D_FOX]).astype(BF16)
    q_ref[...] = qb
    k_ref[...] = kb
    for pair in range(FOX_HEADS // 2):
        vt_ref[pair] = _dot_nt(wvt_ref[pair * LANES:(pair + 1) * LANES, :], h).astype(BF16)

    head_of_col = lax.broadcasted_iota(jnp.int32, (D_FOX, LANES), 0) // FOX_HEAD_DIM
    sel = (head_of_col == lax.broadcasted_iota(jnp.int32, (D_FOX, LANES), 1)).astype(BF16)
    sq = lambda a: (a.astype(F32) * a.astype(F32)).astype(BF16)
    nq_ref[...] = jnp.max(_dot(sq(qb), sel), axis=0, keepdims=True)
    nk_ref[...] = jnp.max(_dot(sq(kb), sel), axis=0, keepdims=True)

    u = _dot(h, wu_ref[...])
    nc = tm // S5_CHUNK
    for lg in range(S5_LANE_GROUPS):
        us_scr[lg] = u[:, lg * LANES:(lg + 1) * LANES]
        for j in range(S5_CHUNK):
            c0 = lg * S5_CW + j * LANES
            uc_ref[:, c0:c0 + LANES] = us_scr[lg, pl.ds(j, nc, stride=S5_CHUNK), :].astype(BF16)

    z = _dot(h, wfl_ref[...]) + bfl_ref[...]
    logf = (jnp.minimum(z, 0.0) - jnp.log1p(jnp.exp(-jnp.abs(z)))) * LOG2E
    row = lax.broadcasted_iota(jnp.int32, (tm, tm), 0)
    col = lax.broadcasted_iota(jnp.int32, (tm, tm), 1)
    tri = (col <= row).astype(BF16)
    parts = _dot(tri, _split3(logf))
    f = (parts[:, 0:LANES] + parts[:, LANES:2 * LANES]) + parts[:, 2 * LANES:] + carry_scr[...]
    carry_scr[...] = f[tm - 1:tm, :]
    fend_ref[...] = f[tm - 1:tm, :]
    aug = _dot(_split3(f), spread_ref[...]) + ones_ref[...]
    fka_ref[...] = aug[:, 0:LANES].astype(BF16)
    fqa_ref[...] = aug[:, LANES:].astype(BF16)


def _fox_aug_constants():
    spread = np.zeros((3 * LANES, 2 * LANES), np.float32)
    ones = np.zeros((1, 2 * LANES), np.float32)
    half = LANES // 2
    for h in range(FOX_HEADS):
        for i in range(3):
            spread[i * LANES + h, 3 * h + i] = 1.0
            spread[i * LANES + h, LANES + half + 3 * h + i] = 1.0
            ones[0, half + 3 * h + i] = 1.0
            ones[0, LANES + 3 * h + i] = -1.0
    return jnp.asarray(spread, BF16), jnp.asarray(ones, F32)


def _in_proj(x, g, wqk, wrest, wvt, wfl, bfl, *, tm):
    bn, l, _ = x.shape
    nc = tm // S5_CHUNK
    nt = l // tm
    pairs = FOX_HEADS // 2
    spread, ones = _fox_aug_constants()
    tok = lambda width: pl.BlockSpec((None, tm, width), lambda b, i: (b, i, 0))
    per_tile = pl.BlockSpec((None, None, 1, LANES), lambda b, i: (b, i, 0, 0))
    qk = jax.ShapeDtypeStruct((bn, l, D_FOX), BF16)
    aug = jax.ShapeDtypeStruct((bn, l, LANES), BF16)
    stat = jax.ShapeDtypeStruct((bn, nt, 1, LANES), F32)
    return pl.pallas_call(
        functools.partial(_in_proj_kernel, tm=tm),
        grid=(bn, nt),
        in_specs=[tok(D_MODEL),
                  _const_spec((1, D_MODEL)),
                  _const_spec((D_MODEL, 2 * D_FOX)),
                  pl.BlockSpec((D_MODEL, D_S5), lambda *_: (0, R_U // D_S5), pipeline_mode=pl.Buffered(1)),
                  _const_spec((D_FOX, D_MODEL)),
                  _const_spec((D_MODEL, LANES)),
                  _const_spec((1, LANES)),
                  _const_spec((3 * LANES, 2 * LANES)),
                  _const_spec((1, 2 * LANES))],
        out_specs=[tok(D_FOX), tok(D_FOX),
                   pl.BlockSpec((None, pairs, None, LANES, tm), lambda b, i: (b, 0, i, 0, 0)),
                   pl.BlockSpec((None, nc, S5_LANE_GROUPS * S5_CW), lambda b, i: (b, i, 0)),
                   tok(LANES), tok(LANES), per_tile, per_tile, per_tile],
        out_shape=[qk, qk,
                   jax.ShapeDtypeStruct((bn, pairs, nt, LANES, tm), BF16),
                   jax.ShapeDtypeStruct((bn, l // S5_CHUNK, S5_LANE_GROUPS * S5_CW), BF16),
                   aug, aug, stat, stat, stat],
        scratch_shapes=[pltpu.VMEM((S5_LANE_GROUPS, tm, LANES), F32), pltpu.VMEM((1, LANES), F32)],
        compiler_params=pltpu.CompilerParams(
            dimension_semantics=("arbitrary", "arbitrary"), vmem_limit_bytes=VMEM_LIMIT),
        name="in_proj",
    )(x, g, wqk, wrest, wvt, wfl, bfl, spread, ones)


def _fox_kernel(fend_ref, qkb_ref, q_ref, fqa_ref, k_ref, fka_ref, vt_ref, o_ref,
                m_scr, l_scr, alpha_scr, acc_scr, s_scr, p_scr, smax_scr, *, t, nt):
    b = pl.program_id(0)
    pair = pl.program_id(1)
    qi = pl.program_id(2)
    lane = lax.broadcasted_iota(jnp.int32, (1, LANES), 1)
    q = q_ref[...]
    fqa = fqa_ref[...]

    heads = (0, 1)
    q_rows, base = [], []
    for e in heads:
        head = 2 * pair + e
        head_lanes = (lane >= e * FOX_HEAD_DIM) & (lane < (e + 1) * FOX_HEAD_DIM)
        f_lanes = lax.rem(lane, LANES // 2) // 3 == head
        q_rows.append(jnp.concatenate([jnp.where(head_lanes, q, jnp.zeros_like(q)),
                                       jnp.where(f_lanes, fqa, jnp.zeros_like(fqa))], axis=-1))
        base.append((b * FOX_HEADS + head) * nt)
    q_aug = jnp.concatenate(q_rows, axis=0)

    def span(which):
        return slice(which[0] * t, (which[-1] + 1) * t)

    def scores(which, j):
        k0 = pl.multiple_of(jnp.maximum(j, 0) * t, t)
        k_aug = jnp.concatenate([k_ref[pl.ds(k0, t), :], fka_ref[pl.ds(k0, t), :]], axis=-1)
        return _dot_nt(k_aug, q_aug[span(which)])

    def stage_scores(which, j, slot):
        s = scores(which, j)
        s_scr[slot, :, span(which)] = s
        smax_scr[slot, :, span(which)] = jnp.max(s, axis=0, keepdims=True)

    def weighted_values(which, j, slot):
        c = span(which)
        acc_scr[:, c] = alpha_scr[slot, :, c] * acc_scr[:, c] + _dot(vt_ref[j], p_scr[slot, :, c])

    def softmax(which, s_slot, p_slot):
        c = span(which)
        m_old = m_scr[:, c]
        m_new = jnp.maximum(m_old, smax_scr[s_slot, :, c])
        p = jnp.exp2(s_scr[s_slot, :, c] - m_new)
        alpha = jnp.exp2(m_old - m_new)
        m_scr[:, c] = m_new
        l_scr[:, c] = alpha * l_scr[:, c] + jnp.sum(p, axis=0, keepdims=True)
        p_scr[p_slot, :, c] = p.astype(BF16)
        alpha_scr[p_slot, :, c] = alpha

    key_pos = lax.broadcasted_iota(jnp.int32, (t, 2 * t), 0)
    query_pos = lax.rem(lax.broadcasted_iota(jnp.int32, (t, 2 * t), 1), t)
    s = jnp.where(key_pos <= query_pos, scores(heads, qi), NEG)
    s_scr[0] = s
    smax_scr[0] = jnp.max(s, axis=0, keepdims=True)
    m_scr[...] = jnp.full_like(m_scr, NEG)
    l_scr[...] = jnp.zeros_like(l_scr)
    acc_scr[...] = jnp.zeros_like(acc_scr)
    softmax(heads, 0, 0)
    m = m_scr[...]
    slack = [qkb_ref[base[e] + qi] - jnp.min(m[:, span((e,))]) + SKIP_LOG2 for e in heads]
    stage_scores(heads, qi - 1, 0)

    def wanted(e, j):
        f_hi = fend_ref[base[e] + jnp.maximum(qi - 1, 0)]
        reach = f_hi - fend_ref[base[e] + jnp.maximum(j, 0)] + slack[e]
        return jnp.logical_and(j >= 0, reach >= 0.0).astype(jnp.int32)

    def trip(j, which, tiles):
        assert tiles == 1 or tiles % 2 == 0
        for i in range(tiles):
            here, there = i % 2, (i + 1) % 2 if tiles > 1 else 0
            weighted_values(which, j + 1 - i, here)
            softmax(which, here, there)
            stage_scores(which, j - 1 - i, there)

    def walk(j, which, tiles):
        def go(jj):
            flag = wanted(which[0], jj - (tiles - 1))
            for e in which[1:]:
                flag = flag * wanted(e, jj - (tiles - 1))
            return flag

        def body(carry):
            trip(carry[0], which, tiles)
            return carry[0] - tiles, go(carry[0] - tiles)

        return lax.while_loop(lambda c: c[1] > 0, body, (j, go(j)))[0]

    j_both = walk(qi - 1, heads, 1)
    for e in heads:
        j_stop = walk(walk(j_both, (e,), 2), (e,), 1)
        weighted_values((e,), j_stop + 1, 0)
    out = acc_scr[...] * (1.0 / l_scr[...])
    dim = lax.broadcasted_iota(jnp.int32, (LANES, 1), 0)
    o_ref[...] = jnp.where(dim < FOX_HEAD_DIM, out[:, span((0,))], out[:, span((1,))]).T.astype(BF16)


def _fox(q, fqa, k, fka, vt, fend, qkb, *, t):
    bn, l, _ = q.shape
    pairs = FOX_HEADS // 2
    nt = l // t
    query_tile = pl.BlockSpec((None, t, LANES), lambda b, p, i, *_: (b, i, p))
    grid_spec = pltpu.PrefetchScalarGridSpec(
        num_scalar_prefetch=2,
        grid=(bn, pairs, nt),
        in_specs=[query_tile,
                  pl.BlockSpec((None, t, LANES), lambda b, p, i, *_: (b, i, 0)),
                  pl.BlockSpec((None, l, LANES), lambda b, p, i, *_: (b, 0, p)),
                  pl.BlockSpec((None, l, LANES), lambda b, p, i, *_: (b, 0, 0)),
                  pl.BlockSpec((None, None, nt, LANES, t), lambda b, p, i, *_: (b, p, 0, 0, 0))],
        out_specs=query_tile,
        scratch_shapes=[pltpu.VMEM((1, 2 * t), F32), pltpu.VMEM((1, 2 * t), F32), pltpu.VMEM((2, 1, 2 * t), F32),
                        pltpu.VMEM((LANES, 2 * t), F32), pltpu.VMEM((2, t, 2 * t), F32),
                        pltpu.VMEM((2, t, 2 * t), BF16), pltpu.VMEM((2, 1, 2 * t), F32)])
    return pl.pallas_call(
        functools.partial(_fox_kernel, t=t, nt=nt),
        grid_spec=grid_spec,
        out_shape=jax.ShapeDtypeStruct((bn, l, D_FOX), BF16),
        compiler_params=pltpu.CompilerParams(
            dimension_semantics=("arbitrary", "arbitrary", "arbitrary"), vmem_limit_bytes=VMEM_LIMIT),
        name="fox",
    )(fend, qkb, q, fqa, k, fka, vt)


def _fox_skip_tables(fend, nq, nk):
    per_head = lambda a: jnp.swapaxes(a[:, :, 0, :FOX_HEADS], 1, 2)
    qmax = jnp.sqrt(per_head(nq))
    kmax = jnp.sqrt(jnp.max(per_head(nk), axis=2, keepdims=True))
    return per_head(fend).reshape(-1), (NORM_MARGIN * qmax * kmax).reshape(-1)


def _s5_tables_kernel(lam_in_ref, mats_ref, d_ref, wtoe_ref, win_ref, voutt_ref, lam_ref):
    lr, li = lam_in_ref[0], lam_in_ref[1]
    step = jnp.exp(lam_in_ref[2])
    mag = jnp.exp(lr * step)
    ab_re, ab_im = mag * jnp.cos(li * step), mag * jnp.sin(li * step)
    den = lr * lr + li * li
    nr, ni = ab_re - 1.0, ab_im
    f_re = (nr * lr + ni * li) / den
    f_im = (ni * lr - nr * li) / den
    btr, bti = mats_ref[0], mats_ref[1]
    bb_re = f_re * btr - f_im * bti
    bb_im = f_re * bti + f_im * btr
    cr, ci = mats_ref[2], mats_ref[3]
    eye = (lax.broadcasted_iota(jnp.int32, (LANES, LANES), 0)
           == lax.broadcasted_iota(jnp.int32, (LANES, LANES), 1))
    tile = lambda j: slice(j * LANES, (j + 1) * LANES)
    zeros = jnp.zeros((LANES, LANES), BF16)
    for jp in range(S5_CHUNK):
        for j in range(jp):
            wtoe_ref[tile(jp), tile(j)] = zeros
    pr = jnp.ones_like(lr)
    pi = jnp.zeros_like(lr)
    for tau in range(S5_CHUNK + 1):
        a_re = cr * pr - ci * pi
        a_im = cr * pi + ci * pr
        if tau < S5_CHUNK:
            kt = _dot_nt_split(bb_re, a_re) - _dot_nt_split(bb_im, a_im)
            if tau == 0:
                kt = kt + jnp.where(eye, d_ref[...], 0.0)
            kt = kt.astype(BF16)
            for jp in range(S5_CHUNK - tau):
                wtoe_ref[tile(jp), tile(jp + tau)] = kt
            j = S5_CHUNK - 1 - tau
            win_ref[tile(j), 0:S5_HALF] = (pr * bb_re - pi * bb_im).astype(BF16)
            win_ref[tile(j), S5_HALF:] = (pr * bb_im + pi * bb_re).astype(BF16)
        if tau >= 1:
            voutt_ref[tile(tau - 1), 0:S5_HALF] = a_re.astype(BF16)
            voutt_ref[tile(tau - 1), S5_HALF:] = (-a_im).astype(BF16)
        if tau == S5_CHUNK:
            lam_ref[:, 0:S5_HALF] = pr
            lam_ref[:, S5_HALF:] = pi
        pr, pi = pr * ab_re - pi * ab_im, pr * ab_im + pi * ab_re


def _s5_tables(lam_re, lam_im, log_step, b_re, b_im, c_re, c_im, d):
    lg, gp = S5_LANE_GROUPS, S5_GROUPS_PER_LANE_GROUP
    eye = jnp.eye(gp, dtype=F32)

    def block_diag(a):
        a = a.astype(F32).reshape(4, lg, gp, S5_GROUP, S5_STATE)
        return jnp.einsum("kGgcp,gh->kGgchp", a, eye).reshape(4, lg, LANES, S5_HALF)

    step = jnp.broadcast_to(log_step.astype(F32)[:, None], (S5_GROUPS, S5_STATE))
    lam = jnp.stack([lam_re.astype(F32), lam_im.astype(F32), step]).reshape(3, lg, 1, S5_HALF)
    mats = block_diag(jnp.stack([jnp.swapaxes(b_re, 1, 2), jnp.swapaxes(b_im, 1, 2), c_re, c_im]))
    spec = lambda r, c: pl.BlockSpec((None, r, c), lambda g: (g, 0, 0))
    stacked = lambda k, r, c: pl.BlockSpec((k, None, r, c), lambda g: (0, g, 0, 0))
    return pl.pallas_call(
        _s5_tables_kernel,
        grid=(lg,),
        in_specs=[stacked(3, 1, S5_HALF), stacked(4, LANES, S5_HALF), spec(1, LANES)],
        out_specs=[spec(S5_CW, S5_CW), spec(S5_CW, S5_SW), spec(S5_CW, S5_SW), spec(1, S5_SW)],
        out_shape=[jax.ShapeDtypeStruct((lg, S5_CW, S5_CW), BF16),
                   jax.ShapeDtypeStruct((lg, S5_CW, S5_SW), BF16),
                   jax.ShapeDtypeStruct((lg, S5_CW, S5_SW), BF16),
                   jax.ShapeDtypeStruct((lg, 1, S5_SW), F32)],
        compiler_params=pltpu.CompilerParams(
            dimension_semantics=("arbitrary",), vmem_limit_bytes=VMEM_LIMIT),
        name="s5_tables",
    )(lam, mats, d.astype(F32).reshape(lg, 1, LANES))


def _s5_kernel(uc_ref, wtoe_ref, win_ref, voutt_ref, lam_ref, yc_ref, e_scr, hs_scr, *, n):
    x = uc_ref[...]
    e = _dot(x, win_ref[...])
    tiles = S5_SW // LANES
    half = tiles // 2
    lane_tile = lambda c: slice(c * LANES, (c + 1) * LANES)
    for c in range(tiles):
        e_scr[c, 0:n, :] = e[:, lane_tile(c)]
        e_scr[c, n:, :] = jnp.zeros((S5_SEGMENTS, LANES), F32)
    lam = jnp.broadcast_to(lam_ref[...], (S5_SEGMENTS, S5_SW))
    lam_r = [lam[:, lane_tile(c)] for c in range(half)]
    lam_i = [lam[:, lane_tile(half + c)] for c in range(half)]
    cmul = lambda ar, ai, br, bi: (ar * br - ai * bi, ar * bi + ai * br)

    seg = n // S5_SEGMENTS + 1
    run_rows = lambda r: pl.ds(r, S5_SEGMENTS, stride=seg)
    zero = jnp.zeros((S5_SEGMENTS, LANES), F32)
    one = jnp.ones((S5_SEGMENTS, LANES), F32)
    zero_row = jnp.zeros((1, LANES), F32)

    def scan_runs(r, carry):
        h, pw = carry
        h_next, pw_next = [None] * tiles, [None] * tiles
        for c in range(half):
            hs_scr[c, run_rows(r), :] = h[c]
            hs_scr[half + c, run_rows(r), :] = h[half + c]
            nr, ni = cmul(lam_r[c], lam_i[c], h[c], h[half + c])
            h_next[c] = nr + e_scr[c, run_rows(r), :]
            h_next[half + c] = ni + e_scr[half + c, run_rows(r), :]
            pw_next[c], pw_next[half + c] = cmul(lam_r[c], lam_i[c], pw[c], pw[half + c])
        return tuple(h_next), tuple(pw_next)

    unit = (one,) * half + (zero,) * half
    run_end, run_mult = lax.fori_loop(0, seg, scan_runs, ((zero,) * tiles, unit))

    start = [None] * tiles
    for c in range(half):
        sr, si, rows_r, rows_i = zero_row, zero_row, [], []
        for s in range(S5_SEGMENTS):
            rows_r.append(sr)
            rows_i.append(si)
            nr, ni = cmul(run_mult[c][0:1], run_mult[half + c][0:1], sr, si)
            sr, si = nr + run_end[c][s:s + 1], ni + run_end[half + c][s:s + 1]
        start[c], start[half + c] = jnp.concatenate(rows_r, axis=0), jnp.concatenate(rows_i, axis=0)

    def add_run_starts(r, pw):
        pw_next = [None] * tiles
        for c in range(half):
            ar, ai = cmul(pw[c], pw[half + c], start[c], start[half + c])
            hs_scr[c, run_rows(r), :] = hs_scr[c, run_rows(r), :] + ar
            hs_scr[half + c, run_rows(r), :] = hs_scr[half + c, run_rows(r), :] + ai
            pw_next[c], pw_next[half + c] = cmul(lam_r[c], lam_i[c], pw[c], pw[half + c])
        return tuple(pw_next)

    lax.fori_loop(0, seg, add_run_starts, unit)
    hs = jnp.concatenate([hs_scr[c, 0:n, :] for c in range(tiles)], axis=-1).astype(BF16)
    blk = 2 * LANES
    for jb in range(S5_CW // blk):
        cols = slice(jb * blk, (jb + 1) * blk)
        kk = (jb + 1) * blk
        y = _dot(x[:, :kk], wtoe_ref[0:kk, cols]) + _dot_nt(hs, voutt_ref[cols, :])
        yc_ref[:, cols] = y.astype(BF16)


def _s5(uc, wtoe, win, voutt, lam):
    bn, n, _ = uc.shape
    per_lg = lambda r, c: pl.BlockSpec((None, r, c), lambda g, b: (g, 0, 0))
    return pl.pallas_call(
        functools.partial(_s5_kernel, n=n),
        grid=(S5_LANE_GROUPS, bn),
        in_specs=[pl.BlockSpec((None, n, S5_CW), lambda g, b: (b, 0, g)),
                  per_lg(S5_CW, S5_CW), per_lg(S5_CW, S5_SW), per_lg(S5_CW, S5_SW), per_lg(1, S5_SW)],
        out_specs=pl.BlockSpec((None, n, S5_CW), lambda g, b: (b, 0, g)),
        out_shape=jax.ShapeDtypeStruct(uc.shape, BF16),
        scratch_shapes=[pltpu.VMEM((S5_SW // LANES, n + S5_SEGMENTS, LANES), F32)] * 2,
        compiler_params=pltpu.CompilerParams(
            dimension_semantics=("arbitrary", "arbitrary"), vmem_limit_bytes=VMEM_LIMIT),
        name="s5",
    )(uc, wtoe, win, voutt, lam)


def _final_kernel(x_ref, yfox_ref, yc_ref, mk_ref, mv_ref, gn_ref, gfin_ref, wg_ref, bm_ref,
                  wglu_ref, bglu_ref, wpf_ref, wps_ref, wpm_ref, wout_ref, o_ref, ys_scr, merged_scr, *, tm):
    x = x_ref[...]
    h = _rms(x, gn_ref[...]).astype(BF16)

    def proj(c0, width):
        return _dot(h, wg_ref[:, c0:c0 + width])

    a_fox = (yfox_ref[...].astype(F32) * _silu(proj(R_GF, D_FOX))).astype(BF16)

    nc = tm // S5_CHUNK
    for lg in range(S5_LANE_GROUPS):
        for j in range(S5_CHUNK):
            c0 = lg * S5_CW + j * LANES
            ys_scr[lg, pl.ds(j, nc, stride=S5_CHUNK), :] = yc_ref[:, c0:c0 + LANES].astype(F32)
    y = _gelu_tanh(jnp.concatenate([ys_scr[lg] for lg in range(S5_LANE_GROUPS)], axis=-1))
    y = y * _sigmoid(_dot(y.astype(BF16), wglu_ref[...]) + bglu_ref[...])
    a_s5 = (y * _silu(proj(R_GS, D_S5))).astype(BF16)

    qm = (proj(R_QM, D_MEM) * (MEM_HEAD_DIM ** -0.5)).astype(BF16)
    heads = []
    for hd in range(MEM_HEADS):
        cols = slice(hd * MEM_HEAD_DIM, (hd + 1) * MEM_HEAD_DIM)
        s = _dot_nt(qm[:, cols], mk_ref[:, cols])
        p = jnp.exp(s - jnp.max(s, axis=-1, keepdims=True))
        inv = 1.0 / jnp.sum(p, axis=-1, keepdims=True)
        heads.append(_dot(p.astype(BF16), mv_ref[:, cols]) * inv)
    ymem = jnp.concatenate(heads, axis=-1)
    a_mem = (ymem * _silu(proj(R_GM, D_MEM))).astype(BF16)

    branches = ((a_fox, wpf_ref), (a_s5, wps_ref), (a_mem, wpm_ref))
    for c0 in range(0, D_MODEL, MERGE_BLOCK):
        cols = slice(c0, c0 + MERGE_BLOCK)
        merged = None
        for i, (a, w_ref) in enumerate(branches):
            g0 = i * D_MODEL + c0
            gate = _sigmoid(proj(R_GL + g0, MERGE_BLOCK) + bm_ref[:, g0:g0 + MERGE_BLOCK])
            term = gate * _dot(a, w_ref[:, cols])
            merged = term if merged is None else merged + term
        merged_scr[:, cols] = merged.astype(BF16)

    out = x + _dot(merged_scr[...], wout_ref[...])
    o_ref[...] = _rms(out, gfin_ref[...])


def _final(x, yfox, yc, mk, mv, gn, gfin, wg, bm, wglu, bglu, wpf, wps, wpm, wout, *, tm):
    bn, l, _ = x.shape
    m = mk.shape[1]
    nc = tm // S5_CHUNK
    tok = lambda width: pl.BlockSpec((None, tm, width), lambda b, i: (b, i, 0))
    mem = pl.BlockSpec((None, m, D_MEM), lambda b, i: (b, 0, 0))
    return pl.pallas_call(
        functools.partial(_final_kernel, tm=tm),
        grid=(bn, l // tm),
        in_specs=[tok(D_MODEL), tok(D_FOX),
                  pl.BlockSpec((None, nc, S5_LANE_GROUPS * S5_CW), lambda b, i: (b, i, 0)),
                  mem, mem,
                  _const_spec((1, D_MODEL)), _const_spec((1, D_MODEL)),
                  _const_spec((D_MODEL, R_COLS)), _const_spec((1, 3 * D_MODEL)),
                  _const_spec((D_S5, D_S5)), _const_spec((1, D_S5)),
                  _const_spec((D_FOX, D_MODEL)), _const_spec((D_S5, D_MODEL)),
                  _const_spec((D_MEM, D_MODEL)), _const_spec((D_MODEL, D_MODEL))],
        out_specs=tok(D_MODEL),
        out_shape=jax.ShapeDtypeStruct(x.shape, x.dtype),
        scratch_shapes=[pltpu.VMEM((S5_LANE_GROUPS, tm, LANES), F32), pltpu.VMEM((tm, D_MODEL), BF16)],
        compiler_params=pltpu.CompilerParams(
            dimension_semantics=("arbitrary", "arbitrary"), vmem_limit_bytes=VMEM_LIMIT),
        name="final",
    )(x, yfox, yc, mk, mv, gn, gfin, wg, bm, wglu, bglu, wpf, wps, wpm, wout)


def _layer(x, mem, g_norm, g_mem_norm, w_in, b_forget, b_merge, w_mem_kv, lam_re, lam_im, log_step,
           s5_b_re, s5_b_im, s5_c_re, s5_c_im, s5_d, w_glu, b_glu, w_proj_fox, w_proj_s5, w_proj_mem,
           w_out, g_out, *, tm_in, t_attn, tm_out):
    bn, l, _ = x.shape
    assert tm_in == t_attn, "the per-tile q norms from in_proj are indexed by attention query tile"
    row = lambda a: a.reshape(1, -1).astype(F32)
    wqk, wvt, wfl_pad, wrest = _weight_prep(w_in)
    bfl_pad = jnp.pad(row(b_forget), ((0, 0), (0, LANES - FOX_HEADS)))

    mk, mv = _mem_kv(mem, row(g_mem_norm), w_mem_kv.astype(BF16))
    q, k, vt, uc, fqa, fka, fend, nq, nk = _in_proj(x, row(g_norm), wqk, wrest, wvt, wfl_pad, bfl_pad, tm=tm_in)
    yfox = _fox(q, fqa, k, fka, vt, *_fox_skip_tables(fend, nq, nk), t=t_attn)
    tables = _s5_tables(lam_re, lam_im, log_step, s5_b_re, s5_b_im, s5_c_re, s5_c_im, s5_d)
    yc = _s5(uc, *tables)
    return _final(x, yfox, yc, mk, mv, row(g_norm), row(g_out), wrest, row(b_merge),
                  w_glu.astype(BF16), row(b_glu), w_proj_fox.astype(BF16), w_proj_s5.astype(BF16),
                  w_proj_mem.astype(BF16), w_out.astype(BF16), tm=tm_out)


def kernel(x, mem, g_norm, g_mem_norm, g_final, w_in, b_forget, b_merge, w_mem_kv, lam_re, lam_im, log_step,
           s5_b_re, s5_b_im, s5_c_re, s5_c_im, s5_d, w_glu, b_glu, w_proj_fox, w_proj_s5, w_proj_mem, w_out):
    depth = w_in.shape[0]
    assert depth == 1, "the fused final kernel applies the closing RMSNorm: one layer only"
    l = x.shape[1]
    tm_in = min(512, l)
    t_attn = min(512, l)
    tm_out = min(512, l)
    return _layer(x, mem, g_norm[0], g_mem_norm[0], w_in, b_forget[0], b_merge[0], w_mem_kv[0],
                  lam_re[0], lam_im[0], log_step[0], s5_b_re[0], s5_b_im[0], s5_c_re[0], s5_c_im[0],
                  s5_d[0], w_glu[0], b_glu[0], w_proj_fox[0], w_proj_s5[0], w_proj_mem[0], w_out[0],
                  g_final, tm_in=tm_in, t_attn=t_attn, tm_out=tm_out)
```

```python
import functools
import math

import jax
import jax.numpy as jnp
import numpy as np
from jax import lax
from jax.experimental import pallas as pl
from jax.experimental.pallas import tpu as pltpu

F32 = jnp.float32
BF16 = jnp.bfloat16

D_MODEL = 1024
EPS = 1e-6
NEG = -1e30
LOG2E = math.log2(math.e)
SKIP_LOG2 = 127.0
NORM_MARGIN = 1.03

FOX_HEAD_DIM = 64
D_FOX = 768
FOX_HEADS = D_FOX // FOX_HEAD_DIM

D_S5 = 768
S5_GROUP = 16
S5_GROUPS = D_S5 // S5_GROUP
S5_STATE = 64

D_MEM = 512
MEM_HEADS = 4
MEM_HEAD_DIM = D_MEM // MEM_HEADS

LANES = 128
S5_CHUNK = 16
S5_LANE_GROUPS = D_S5 // LANES
S5_GROUPS_PER_LANE_GROUP = LANES // S5_GROUP
S5_CW = S5_CHUNK * LANES
S5_HALF = S5_GROUPS_PER_LANE_GROUP * S5_STATE
S5_SW = 2 * S5_HALF
S5_SEGMENTS = 8

W_FL = 3 * D_FOX
W_REST = W_FL + FOX_HEADS
R_GF, R_U, R_GS, R_QM, R_GM, R_GL = 0, 768, 1536, 2304, 2816, 3328
R_COLS = R_GL + 3 * D_MODEL

VMEM_LIMIT = 56 * 1024 * 1024


def _dot(a, b, **kw):
    return jnp.dot(a, b, preferred_element_type=F32, **kw)


def _dot_nt(a, b, **kw):
    return lax.dot_general(a, b, (((1,), (1,)), ((), ())), preferred_element_type=F32, **kw)


def _dot_nt_split(a, b):
    a_hi, b_hi = a.astype(BF16), b.astype(BF16)
    a_lo = (a - a_hi.astype(F32)).astype(BF16)
    b_lo = (b - b_hi.astype(F32)).astype(BF16)
    return _dot_nt(a_hi, b_hi) + (_dot_nt(a_hi, b_lo) + _dot_nt(a_lo, b_hi))


def _rms(xf, g):
    return xf * lax.rsqrt(jnp.mean(xf * xf, axis=-1, keepdims=True) + EPS) * g


def _sigmoid(z):
    return 1.0 / (1.0 + jnp.exp(-z))


def _silu(z):
    return z * _sigmoid(z)


def _gelu_tanh(y):
    return 0.5 * y * (1.0 + jnp.tanh(math.sqrt(2.0 / math.pi) * (y + 0.044715 * (y * y * y))))


def _split3(a):
    hi = a.astype(BF16)
    rest = a - hi.astype(F32)
    mid = rest.astype(BF16)
    lo = (rest - mid.astype(F32)).astype(BF16)
    return jnp.concatenate([hi, mid, lo], axis=-1)


def _const_spec(shape):
    nd = len(shape)
    return pl.BlockSpec(shape, lambda *_: (0,) * nd, pipeline_mode=pl.Buffered(1))


def _columns_kernel(wt_ref, o_ref, *, valid):
    cols = wt_ref[:, 0, :].T
    width = cols.shape[1]
    if valid < width:
        cols = jnp.where(lax.broadcasted_iota(jnp.int32, (1, width), 1) < valid, cols, 0.0)
    o_ref[...] = cols.astype(BF16)


def _rows_kernel(wt_ref, o_ref):
    o_ref[...] = wt_ref[:, 0, :].astype(BF16)


def _weight_prep(w_in):
    depth, d, n_in = w_in.shape
    assert depth == 1 and n_in == W_REST + R_COLS
    wt = jnp.transpose(w_in, (2, 0, 1))

    def columns(first, count, width, valid=None, name="weight_cols"):
        assert count % width == 0
        return pl.pallas_call(
            functools.partial(_columns_kernel, valid=width if valid is None else valid),
            grid=(count // width,),
            in_specs=[pl.BlockSpec((pl.Element(width), pl.Element(1), pl.Element(d)),
                                   lambda c: (first + width * c, 0, 0))],
            out_specs=pl.BlockSpec((d, width), lambda c: (0, c)),
            out_shape=jax.ShapeDtypeStruct((d, count), BF16),
            name=name,
        )(wt)

    wqk = columns(0, 2 * D_FOX, 2 * LANES, name="weight_qk")
    wfl = columns(W_FL, LANES, LANES, valid=FOX_HEADS, name="weight_forget")
    wrest = columns(W_REST, R_COLS, 2 * LANES, name="weight_rest")
    wvt = pl.pallas_call(
        _rows_kernel,
        grid=(D_FOX // LANES,),
        in_specs=[pl.BlockSpec((LANES, 1, d), lambda r: (2 * D_FOX // LANES + r, 0, 0))],
        out_specs=pl.BlockSpec((LANES, d), lambda r: (r, 0)),
        out_shape=jax.ShapeDtypeStruct((D_FOX, d), BF16),
        name="weight_vt",
    )(wt)
    return wqk, wvt, wfl, wrest


def _mem_kv_kernel(mem_ref, g_ref, w_ref, mk_ref, mv_ref):
    h = _rms(mem_ref[...], g_ref[...]).astype(BF16)
    kv = _dot(h, w_ref[...])
    mk_ref[...] = kv[:, :D_MEM].astype(BF16)
    mv_ref[...] = kv[:, D_MEM:].astype(BF16)


def _mem_kv(mem, g, w):
    bn, m, _ = mem.shape
    out = jax.ShapeDtypeStruct((bn, m, D_MEM), BF16)
    return pl.pallas_call(
        _mem_kv_kernel,
        grid=(bn,),
        in_specs=[pl.BlockSpec((None, m, D_MODEL), lambda b: (b, 0, 0)),
                  _const_spec((1, D_MODEL)),
                  _const_spec((D_MODEL, 2 * D_MEM))],
        out_specs=[pl.BlockSpec((None, m, D_MEM), lambda b: (b, 0, 0))] * 2,
        out_shape=[out, out],
        name="mem_kv",
    )(mem, g, w)


def _in_proj_kernel(x_ref, g_ref, w_ref, wu_ref, wvt_ref, wfl_ref, bfl_ref, spread_ref, ones_ref,
                    q_ref, k_ref, vt_ref, uc_ref, fqa_ref, fka_ref, fend_ref, nq_ref, nk_ref,
                    us_scr, carry_scr, *, tm):
    @pl.when(pl.program_id(1) == 0)
    def _():
        carry_scr[...] = jnp.zeros_like(carry_scr)

    h = _rms(x_ref[...], g_ref[...]).astype(BF16)
    head_of_col = lax.broadcasted_iota(jnp.int32, (D_FOX, LANES), 0) // FOX_HEAD_DIM
    sel = (head_of_col == lax.broadcasted_iota(jnp.int32, (D_FOX, LANES), 1)).astype(BF16)
    sq = lambda a: (a.astype(F32) * a.astype(F32)).astype(BF16)
    blk = 2 * LANES
    nq = nk = None
    for c0 in range(0, D_FOX, blk):
        cols = slice(c0, c0 + blk)
        qb = (_dot(h, w_ref[:, cols]) * (LOG2E * FOX_HEAD_DIM ** -0.5)).astype(BF16)
        kb = _dot(h, w_ref[:, D_FOX + c0:D_FOX + c0 + blk]).astype(BF16)
        q_ref[:, cols] = qb
        k_ref[:, cols] = kb
        nq_c, nk_c = _dot(sq(qb), sel[cols]), _dot(sq(kb), sel[cols])
        nq, nk = (nq_c, nk_c) if nq is None else (nq + nq_c, nk + nk_c)
    nq_ref[...] = jnp.max(nq, axis=0, keepdims=True)
    nk_ref[...] = jnp.max(nk, axis=0, keepdims=True)
    for pair in range(FOX_HEADS // 2):
        vt_ref[pair] = _dot_nt(wvt_ref[pair * LANES:(pair + 1) * LANES, :], h).astype(BF16)

    u = _dot(h, wu_ref[...])
    nc = tm // S5_CHUNK
    for lg in range(S5_LANE_GROUPS):
        us_scr[lg] = u[:, lg * LANES:(lg + 1) * LANES]
        for j in range(S5_CHUNK):
            c0 = lg * S5_CW + j * LANES
            uc_ref[:, c0:c0 + LANES] = us_scr[lg, pl.ds(j, nc, stride=S5_CHUNK), :].astype(BF16)

    z = _dot(h, wfl_ref[...]) + bfl_ref[...]
    logf = (jnp.minimum(z, 0.0) - jnp.log1p(jnp.exp(-jnp.abs(z)))) * LOG2E
    row = lax.broadcasted_iota(jnp.int32, (tm, tm), 0)
    col = lax.broadcasted_iota(jnp.int32, (tm, tm), 1)
    tri = (col <= row).astype(BF16)
    parts = _dot(tri, _split3(logf))
    f = (parts[:, 0:LANES] + parts[:, LANES:2 * LANES]) + parts[:, 2 * LANES:] + carry_scr[...]
    carry_scr[...] = f[tm - 1:tm, :]
    fend_ref[...] = f[tm - 1:tm, :]
    aug = _dot(_split3(f), spread_ref[...]) + ones_ref[...]
    fka_ref[...] = aug[:, 0:LANES].astype(BF16)
    fqa_ref[...] = aug[:, LANES:].astype(BF16)


def _fox_aug_constants():
    spread = np.zeros((3 * LANES, 2 * LANES), np.float32)
    ones = np.zeros((1, 2 * LANES), np.float32)
    half = LANES // 2
    for h in range(FOX_HEADS):
        for i in range(3):
            spread[i * LANES + h, 3 * h + i] = 1.0
            spread[i * LANES + h, LANES + half + 3 * h + i] = 1.0
            ones[0, half + 3 * h + i] = 1.0
            ones[0, LANES + 3 * h + i] = -1.0
    return jnp.asarray(spread, BF16), jnp.asarray(ones, F32)


def _in_proj(x, g, wqk, wrest, wvt, wfl, bfl, *, tm):
    bn, l, _ = x.shape
    nc = tm // S5_CHUNK
    nt = l // tm
    pairs = FOX_HEADS // 2
    spread, ones = _fox_aug_constants()
    tok = lambda width: pl.BlockSpec((None, tm, width), lambda b, i: (b, i, 0))
    per_tile = pl.BlockSpec((None, None, 1, LANES), lambda b, i: (b, i, 0, 0))
    qk = jax.ShapeDtypeStruct((bn, l, D_FOX), BF16)
    aug = jax.ShapeDtypeStruct((bn, l, LANES), BF16)
    stat = jax.ShapeDtypeStruct((bn, nt, 1, LANES), F32)
    return pl.pallas_call(
        functools.partial(_in_proj_kernel, tm=tm),
        grid=(bn, nt),
        in_specs=[tok(D_MODEL),
                  _const_spec((1, D_MODEL)),
                  _const_spec((D_MODEL, 2 * D_FOX)),
                  pl.BlockSpec((D_MODEL, D_S5), lambda *_: (0, R_U // D_S5), pipeline_mode=pl.Buffered(1)),
                  _const_spec((D_FOX, D_MODEL)),
                  _const_spec((D_MODEL, LANES)),
                  _const_spec((1, LANES)),
                  _const_spec((3 * LANES, 2 * LANES)),
                  _const_spec((1, 2 * LANES))],
        out_specs=[tok(D_FOX), tok(D_FOX),
                   pl.BlockSpec((None, pairs, None, LANES, tm), lambda b, i: (b, 0, i, 0, 0)),
                   pl.BlockSpec((None, nc, S5_LANE_GROUPS * S5_CW), lambda b, i: (b, i, 0)),
                   tok(LANES), tok(LANES), per_tile, per_tile, per_tile],
        out_shape=[qk, qk,
                   jax.ShapeDtypeStruct((bn, pairs, nt, LANES, tm), BF16),
                   jax.ShapeDtypeStruct((bn, l // S5_CHUNK, S5_LANE_GROUPS * S5_CW), BF16),
                   aug, aug, stat, stat, stat],
        scratch_shapes=[pltpu.VMEM((S5_LANE_GROUPS, tm, LANES), F32), pltpu.VMEM((1, LANES), F32)],
        compiler_params=pltpu.CompilerParams(
            dimension_semantics=("arbitrary", "arbitrary"), vmem_limit_bytes=VMEM_LIMIT),
        name="in_proj",
    )(x, g, wqk, wrest, wvt, wfl, bfl, spread, ones)


def _fox_kernel(fend_ref, qkb_ref, q_ref, fqa_ref, k_ref, fka_ref, vt_ref, o_ref,
                m_scr, l_scr, alpha_scr, acc_scr, s_scr, p_scr, smax_scr, *, t, nt):
    b = pl.program_id(0)
    pair = pl.program_id(1)
    qi = pl.program_id(2)
    lane = lax.broadcasted_iota(jnp.int32, (1, LANES), 1)
    q = q_ref[...]
    fqa = fqa_ref[...]

    heads = (0, 1)
    q_rows, base = [], []
    for e in heads:
        head = 2 * pair + e
        head_lanes = (lane >= e * FOX_HEAD_DIM) & (lane < (e + 1) * FOX_HEAD_DIM)
        f_lanes = lax.rem(lane, LANES // 2) // 3 == head
        q_rows.append(jnp.concatenate([jnp.where(head_lanes, q, jnp.zeros_like(q)),
                                       jnp.where(f_lanes, fqa, jnp.zeros_like(fqa))], axis=-1))
        base.append((b * FOX_HEADS + head) * nt)
    q_aug = jnp.concatenate(q_rows, axis=0)

    def span(which):
        return slice(which[0] * t, (which[-1] + 1) * t)

    def scores(which, j):
        k0 = pl.multiple_of(jnp.maximum(j, 0) * t, t)
        k_aug = jnp.concatenate([k_ref[pl.ds(k0, t), :], fka_ref[pl.ds(k0, t), :]], axis=-1)
        return _dot_nt(k_aug, q_aug[span(which)])

    def stage_scores(which, j, slot):
        s = scores(which, j)
        s_scr[slot, :, span(which)] = s
        smax_scr[slot, :, span(which)] = jnp.max(s, axis=0, keepdims=True)

    def weighted_values(which, j, slot):
        c = span(which)
        acc_scr[:, c] = alpha_scr[slot, :, c] * acc_scr[:, c] + _dot(vt_ref[j], p_scr[slot, :, c])

    def softmax(which, s_slot, p_slot):
        c = span(which)
        m_old = m_scr[:, c]
        m_new = jnp.maximum(m_old, smax_scr[s_slot, :, c])
        p = jnp.exp2(s_scr[s_slot, :, c] - m_new)
        alpha = jnp.exp2(m_old - m_new)
        m_scr[:, c] = m_new
        l_scr[:, c] = alpha * l_scr[:, c] + jnp.sum(p, axis=0, keepdims=True)
        p_scr[p_slot, :, c] = p.astype(BF16)
        alpha_scr[p_slot, :, c] = alpha

    key_pos = lax.broadcasted_iota(jnp.int32, (t, 2 * t), 0)
    query_pos = lax.rem(lax.broadcasted_iota(jnp.int32, (t, 2 * t), 1), t)
    s = jnp.where(key_pos <= query_pos, scores(heads, qi), NEG)
    s_scr[0] = s
    smax_scr[0] = jnp.max(s, axis=0, keepdims=True)
    m_scr[...] = jnp.full_like(m_scr, NEG)
    l_scr[...] = jnp.zeros_like(l_scr)
    acc_scr[...] = jnp.zeros_like(acc_scr)
    softmax(heads, 0, 0)
    m = m_scr[...]
    slack = [qkb_ref[base[e] + qi] - jnp.min(m[:, span((e,))]) + SKIP_LOG2 for e in heads]
    stage_scores(heads, qi - 1, 0)

    def wanted(e, j):
        f_hi = fend_ref[base[e] + jnp.maximum(qi - 1, 0)]
        reach = f_hi - fend_ref[base[e] + jnp.maximum(j, 0)] + slack[e]
        return jnp.logical_and(j >= 0, reach >= 0.0).astype(jnp.int32)

    def trip(j, which, tiles):
        assert tiles == 1 or tiles % 2 == 0
        for i in range(tiles):
            here, there = i % 2, (i + 1) % 2 if tiles > 1 else 0
            weighted_values(which, j + 1 - i, here)
            softmax(which, here, there)
            stage_scores(which, j - 1 - i, there)

    def walk(j, which, tiles):
        def go(jj):
            flag = wanted(which[0], jj - (tiles - 1))
            for e in which[1:]:
                flag = flag * wanted(e, jj - (tiles - 1))
            return flag

        def body(carry):
            trip(carry[0], which, tiles)
            return carry[0] - tiles, go(carry[0] - tiles)

        return lax.while_loop(lambda c: c[1] > 0, body, (j, go(j)))[0]

    j_both = walk(qi - 1, heads, 1)
    for e in heads:
        j_stop = walk(walk(j_both, (e,), 2), (e,), 1)
        weighted_values((e,), j_stop + 1, 0)
    out = acc_scr[...] * (1.0 / l_scr[...])
    dim = lax.broadcasted_iota(jnp.int32, (LANES, 1), 0)
    o_ref[...] = jnp.where(dim < FOX_HEAD_DIM, out[:, span((0,))], out[:, span((1,))]).T.astype(BF16)


def _fox(q, fqa, k, fka, vt, fend, qkb, *, t):
    bn, l, _ = q.shape
    pairs = FOX_HEADS // 2
    nt = l // t
    query_tile = pl.BlockSpec((None, t, LANES), lambda b, p, i, *_: (b, i, p))
    grid_spec = pltpu.PrefetchScalarGridSpec(
        num_scalar_prefetch=2,
        grid=(bn, pairs, nt),
        in_specs=[query_tile,
                  pl.BlockSpec((None, t, LANES), lambda b, p, i, *_: (b, i, 0)),
                  pl.BlockSpec((None, l, LANES), lambda b, p, i, *_: (b, 0, p)),
                  pl.BlockSpec((None, l, LANES), lambda b, p, i, *_: (b, 0, 0)),
                  pl.BlockSpec((None, None, nt, LANES, t), lambda b, p, i, *_: (b, p, 0, 0, 0))],
        out_specs=query_tile,
        scratch_shapes=[pltpu.VMEM((1, 2 * t), F32), pltpu.VMEM((1, 2 * t), F32), pltpu.VMEM((2, 1, 2 * t), F32),
                        pltpu.VMEM((LANES, 2 * t), F32), pltpu.VMEM((2, t, 2 * t), F32),
                        pltpu.VMEM((2, t, 2 * t), BF16), pltpu.VMEM((2, 1, 2 * t), F32)])
    return pl.pallas_call(
        functools.partial(_fox_kernel, t=t, nt=nt),
        grid_spec=grid_spec,
        out_shape=jax.ShapeDtypeStruct((bn, l, D_FOX), BF16),
        compiler_params=pltpu.CompilerParams(
            dimension_semantics=("arbitrary", "arbitrary", "arbitrary"), vmem_limit_bytes=VMEM_LIMIT),
        name="fox",
    )(fend, qkb, q, fqa, k, fka, vt)


def _fox_skip_tables(fend, nq, nk):
    per_head = lambda a: jnp.swapaxes(a[:, :, 0, :FOX_HEADS], 1, 2)
    qmax = jnp.sqrt(per_head(nq))
    kmax = jnp.sqrt(jnp.max(per_head(nk), axis=2, keepdims=True))
    return per_head(fend).reshape(-1), (NORM_MARGIN * qmax * kmax).reshape(-1)


def _s5_tables_kernel(lam_in_ref, mats_ref, d_ref, wtoe_ref, win_ref, voutt_ref, lam_ref):
    lr, li = lam_in_ref[0], lam_in_ref[1]
    step = jnp.exp(lam_in_ref[2])
    mag = jnp.exp(lr * step)
    ab_re, ab_im = mag * jnp.cos(li * step), mag * jnp.sin(li * step)
    den = lr * lr + li * li
    nr, ni = ab_re - 1.0, ab_im
    f_re = (nr * lr + ni * li) / den
    f_im = (ni * lr - nr * li) / den
    btr, bti = mats_ref[0], mats_ref[1]
    bb_re = f_re * btr - f_im * bti
    bb_im = f_re * bti + f_im * btr
    cr, ci = mats_ref[2], mats_ref[3]
    eye = (lax.broadcasted_iota(jnp.int32, (LANES, LANES), 0)
           == lax.broadcasted_iota(jnp.int32, (LANES, LANES), 1))
    tile = lambda j: slice(j * LANES, (j + 1) * LANES)
    zeros = jnp.zeros((LANES, LANES), BF16)
    for jp in range(S5_CHUNK):
        for j in range(jp):
            wtoe_ref[tile(jp), tile(j)] = zeros
    pr = jnp.ones_like(lr)
    pi = jnp.zeros_like(lr)
    for tau in range(S5_CHUNK + 1):
        a_re = cr * pr - ci * pi
        a_im = cr * pi + ci * pr
        if tau < S5_CHUNK:
            kt = _dot_nt_split(bb_re, a_re) - _dot_nt_split(bb_im, a_im)
            if tau == 0:
                kt = kt + jnp.where(eye, d_ref[...], 0.0)
            kt = kt.astype(BF16)
            for jp in range(S5_CHUNK - tau):
                wtoe_ref[tile(jp), tile(jp + tau)] = kt
            j = S5_CHUNK - 1 - tau
            win_ref[tile(j), 0:S5_HALF] = (pr * bb_re - pi * bb_im).astype(BF16)
            win_ref[tile(j), S5_HALF:] = (pr * bb_im + pi * bb_re).astype(BF16)
        if tau >= 1:
            voutt_ref[tile(tau - 1), 0:S5_HALF] = a_re.astype(BF16)
            voutt_ref[tile(tau - 1), S5_HALF:] = (-a_im).astype(BF16)
        if tau == S5_CHUNK:
            lam_ref[:, 0:S5_HALF] = pr
            lam_ref[:, S5_HALF:] = pi
        pr, pi = pr * ab_re - pi * ab_im, pr * ab_im + pi * ab_re


def _s5_tables(lam_re, lam_im, log_step, b_re, b_im, c_re, c_im, d):
    lg, gp = S5_LANE_GROUPS, S5_GROUPS_PER_LANE_GROUP
    eye = jnp.eye(gp, dtype=F32)

    def block_diag(a):
        a = a.astype(F32).reshape(4, lg, gp, S5_GROUP, S5_STATE)
        return jnp.einsum("kGgcp,gh->kGgchp", a, eye).reshape(4, lg, LANES, S5_HALF)

    step = jnp.broadcast_to(log_step.astype(F32)[:, None], (S5_GROUPS, S5_STATE))
    lam = jnp.stack([lam_re.astype(F32), lam_im.astype(F32), step]).reshape(3, lg, 1, S5_HALF)
    mats = block_diag(jnp.stack([jnp.swapaxes(b_re, 1, 2), jnp.swapaxes(b_im, 1, 2), c_re, c_im]))
    spec = lambda r, c: pl.BlockSpec((None, r, c), lambda g: (g, 0, 0))
    stacked = lambda k, r, c: pl.BlockSpec((k, None, r, c), lambda g: (0, g, 0, 0))
    return pl.pallas_call(
        _s5_tables_kernel,
        grid=(lg,),
        in_specs=[stacked(3, 1, S5_HALF), stacked(4, LANES, S5_HALF), spec(1, LANES)],
        out_specs=[spec(S5_CW, S5_CW), spec(S5_CW, S5_SW), spec(S5_CW, S5_SW), spec(1, S5_SW)],
        out_shape=[jax.ShapeDtypeStruct((lg, S5_CW, S5_CW), BF16),
                   jax.ShapeDtypeStruct((lg, S5_CW, S5_SW), BF16),
                   jax.ShapeDtypeStruct((lg, S5_CW, S5_SW), BF16),
                   jax.ShapeDtypeStruct((lg, 1, S5_SW), F32)],
        compiler_params=pltpu.CompilerParams(
            dimension_semantics=("arbitrary",), vmem_limit_bytes=VMEM_LIMIT),
        name="s5_tables",
    )(lam, mats, d.astype(F32).reshape(lg, 1, LANES))


def _s5_kernel(uc_ref, wtoe_ref, win_ref, voutt_ref, lam_ref, yc_ref, e_scr, hs_scr, *, n):
    x = uc_ref[...]
    e = _dot(x, win_ref[...])
    tiles = S5_SW // LANES
    half = tiles // 2
    lane_tile = lambda c: slice(c * LANES, (c + 1) * LANES)
    for c in range(tiles):
        e_scr[c, 0:n, :] = e[:, lane_tile(c)]
        e_scr[c, n:, :] = jnp.zeros((S5_SEGMENTS, LANES), F32)
    lam = jnp.broadcast_to(lam_ref[...], (S5_SEGMENTS, S5_SW))
    lam_r = [lam[:, lane_tile(c)] for c in range(half)]
    lam_i = [lam[:, lane_tile(half + c)] for c in range(half)]
    cmul = lambda ar, ai, br, bi: (ar * br - ai * bi, ar * bi + ai * br)

    seg = n // S5_SEGMENTS + 1
    run_rows = lambda r: pl.ds(r, S5_SEGMENTS, stride=seg)
    zero = jnp.zeros((S5_SEGMENTS, LANES), F32)
    one = jnp.ones((S5_SEGMENTS, LANES), F32)
    zero_row = jnp.zeros((1, LANES), F32)

    def scan_runs(r, carry):
        h, pw = carry
        h_next, pw_next = [None] * tiles, [None] * tiles
        for c in range(half):
            hs_scr[c, run_rows(r), :] = h[c]
            hs_scr[half + c, run_rows(r), :] = h[half + c]
            nr, ni = cmul(lam_r[c], lam_i[c], h[c], h[half + c])
            h_next[c] = nr + e_scr[c, run_rows(r), :]
            h_next[half + c] = ni + e_scr[half + c, run_rows(r), :]
            pw_next[c], pw_next[half + c] = cmul(lam_r[c], lam_i[c], pw[c], pw[half + c])
        return tuple(h_next), tuple(pw_next)

    unit = (one,) * half + (zero,) * half
    run_end, run_mult = lax.fori_loop(0, seg, scan_runs, ((zero,) * tiles, unit))

    start = [None] * tiles
    for c in range(half):
        sr, si, rows_r, rows_i = zero_row, zero_row, [], []
        for s in range(S5_SEGMENTS):
            rows_r.append(sr)
            rows_i.append(si)
            nr, ni = cmul(run_mult[c][0:1], run_mult[half + c][0:1], sr, si)
            sr, si = nr + run_end[c][s:s + 1], ni + run_end[half + c][s:s + 1]
        start[c], start[half + c] = jnp.concatenate(rows_r, axis=0), jnp.concatenate(rows_i, axis=0)

    def add_run_starts(r, pw):
        pw_next = [None] * tiles
        for c in range(half):
            ar, ai = cmul(pw[c], pw[half + c], start[c], start[half + c])
            hs_scr[c, run_rows(r), :] = hs_scr[c, run_rows(r), :] + ar
            hs_scr[half + c, run_rows(r), :] = hs_scr[half + c, run_rows(r), :] + ai
            pw_next[c], pw_next[half + c] = cmul(lam_r[c], lam_i[c], pw[c], pw[half + c])
        return tuple(pw_next)

    lax.fori_loop(0, seg, add_run_starts, unit)
    hs = jnp.concatenate([hs_scr[c, 0:n, :] for c in range(tiles)], axis=-1).astype(BF16)
    blk = 2 * LANES
    for jb in range(S5_CW // blk):
        cols = slice(jb * blk, (jb + 1) * blk)
        kk = (jb + 1) * blk
        y = _dot(x[:, :kk], wtoe_ref[0:kk, cols]) + _dot_nt(hs, voutt_ref[cols, :])
        yc_ref[:, cols] = y.astype(BF16)


def _s5(uc, wtoe, win, voutt, lam):
    bn, n, _ = uc.shape
    per_lg = lambda r, c: pl.BlockSpec((None, r, c), lambda g, b: (g, 0, 0))
    return pl.pallas_call(
        functools.partial(_s5_kernel, n=n),
        grid=(S5_LANE_GROUPS, bn),
        in_specs=[pl.BlockSpec((None, n, S5_CW), lambda g, b: (b, 0, g)),
                  per_lg(S5_CW, S5_CW), per_lg(S5_CW, S5_SW), per_lg(S5_CW, S5_SW), per_lg(1, S5_SW)],
        out_specs=pl.BlockSpec((None, n, S5_CW), lambda g, b: (b, 0, g)),
        out_shape=jax.ShapeDtypeStruct(uc.shape, BF16),
        scratch_shapes=[pltpu.VMEM((S5_SW // LANES, n + S5_SEGMENTS, LANES), F32)] * 2,
        compiler_params=pltpu.CompilerParams(
            dimension_semantics=("arbitrary", "arbitrary"), vmem_limit_bytes=VMEM_LIMIT),
        name="s5",
    )(uc, wtoe, win, voutt, lam)


def _final_kernel(x_ref, yfox_ref, yc_ref, mk_ref, mv_ref, gn_ref, gfin_ref, wg_ref, bm_ref,
                  wglu_ref, bglu_ref, wpf_ref, wps_ref, wpm_ref, wout_ref, o_ref, ys_scr, *, tm):
    x = x_ref[...]
    h = _rms(x, gn_ref[...]).astype(BF16)

    def proj(c0, width):
        return _dot(h, wg_ref[:, c0:c0 + width])

    def gate(i):
        return _sigmoid(proj(R_GL + i * D_MODEL, D_MODEL) + bm_ref[:, i * D_MODEL:(i + 1) * D_MODEL])

    a = (yfox_ref[...].astype(F32) * _silu(proj(R_GF, D_FOX))).astype(BF16)
    merged = gate(0) * _dot(a, wpf_ref[...])

    nc = tm // S5_CHUNK
    for lg in range(S5_LANE_GROUPS):
        for j in range(S5_CHUNK):
            c0 = lg * S5_CW + j * LANES
            ys_scr[lg, pl.ds(j, nc, stride=S5_CHUNK), :] = yc_ref[:, c0:c0 + LANES].astype(F32)
    y = _gelu_tanh(jnp.concatenate([ys_scr[lg] for lg in range(S5_LANE_GROUPS)], axis=-1))
    y = y * _sigmoid(_dot(y.astype(BF16), wglu_ref[...]) + bglu_ref[...])
    y = (y * _silu(proj(R_GS, D_S5))).astype(BF16)
    merged = merged + gate(1) * _dot(y, wps_ref[...])

    qm = (proj(R_QM, D_MEM) * (MEM_HEAD_DIM ** -0.5)).astype(BF16)
    heads = []
    for hd in range(MEM_HEADS):
        cols = slice(hd * MEM_HEAD_DIM, (hd + 1) * MEM_HEAD_DIM)
        s = _dot_nt(qm[:, cols], mk_ref[:, cols])
        p = jnp.exp(s - jnp.max(s, axis=-1, keepdims=True))
        inv = 1.0 / jnp.sum(p, axis=-1, keepdims=True)
        heads.append(_dot(p.astype(BF16), mv_ref[:, cols]) * inv)
    ymem = jnp.concatenate(heads, axis=-1)
    a = (ymem * _silu(proj(R_GM, D_MEM))).astype(BF16)
    merged = merged + gate(2) * _dot(a, wpm_ref[...])

    out = x + _dot(merged.astype(BF16), wout_ref[...])
    o_ref[...] = _rms(out, gfin_ref[...])


def _final(x, yfox, yc, mk, mv, gn, gfin, wg, bm, wglu, bglu, wpf, wps, wpm, wout, *, tm):
    bn, l, _ = x.shape
    m = mk.shape[1]
    nc = tm // S5_CHUNK
    tok = lambda width: pl.BlockSpec((None, tm, width), lambda b, i: (b, i, 0))
    mem = pl.BlockSpec((None, m, D_MEM), lambda b, i: (b, 0, 0))
    return pl.pallas_call(
        functools.partial(_final_kernel, tm=tm),
        grid=(bn, l // tm),
        in_specs=[tok(D_MODEL), tok(D_FOX),
                  pl.BlockSpec((None, nc, S5_LANE_GROUPS * S5_CW), lambda b, i: (b, i, 0)),
                  mem, mem,
                  _const_spec((1, D_MODEL)), _const_spec((1, D_MODEL)),
                  _const_spec((D_MODEL, R_COLS)), _const_spec((1, 3 * D_MODEL)),
                  _const_spec((D_S5, D_S5)), _const_spec((1, D_S5)),
                  _const_spec((D_FOX, D_MODEL)), _const_spec((D_S5, D_MODEL)),
                  _const_spec((D_MEM, D_MODEL)), _const_spec((D_MODEL, D_MODEL))],
        out_specs=tok(D_MODEL),
        out_shape=jax.ShapeDtypeStruct(x.shape, x.dtype),
        scratch_shapes=[pltpu.VMEM((S5_LANE_GROUPS, tm, LANES), F32)],
        compiler_params=pltpu.CompilerParams(
            dimension_semantics=("arbitrary", "arbitrary"), vmem_limit_bytes=VMEM_LIMIT),
        name="final",
    )(x, yfox, yc, mk, mv, gn, gfin, wg, bm, wglu, bglu, wpf, wps, wpm, wout)


def _layer(x, mem, g_norm, g_mem_norm, w_in, b_forget, b_merge, w_mem_kv, lam_re, lam_im, log_step,
           s5_b_re, s5_b_im, s5_c_re, s5_c_im, s5_d, w_glu, b_glu, w_proj_fox, w_proj_s5, w_proj_mem,
           w_out, g_out, *, tm_in, t_attn, tm_out):
    bn, l, _ = x.shape
    assert tm_in == t_attn, "the per-tile q norms from in_proj are indexed by attention query tile"
    row = lambda a: a.reshape(1, -1).astype(F32)
    wqk, wvt, wfl_pad, wrest = _weight_prep(w_in)
    bfl_pad = jnp.pad(row(b_forget), ((0, 0), (0, LANES - FOX_HEADS)))

    mk, mv = _mem_kv(mem, row(g_mem_norm), w_mem_kv.astype(BF16))
    q, k, vt, uc, fqa, fka, fend, nq, nk = _in_proj(x, row(g_norm), wqk, wrest, wvt, wfl_pad, bfl_pad, tm=tm_in)
    yfox = _fox(q, fqa, k, fka, vt, *_fox_skip_tables(fend, nq, nk), t=t_attn)
    tables = _s5_tables(lam_re, lam_im, log_step, s5_b_re, s5_b_im, s5_c_re, s5_c_im, s5_d)
    yc = _s5(uc, *tables)
    return _final(x, yfox, yc, mk, mv, row(g_norm), row(g_out), wrest, row(b_merge),
                  w_glu.astype(BF16), row(b_glu), w_proj_fox.astype(BF16), w_proj_s5.astype(BF16),
                  w_proj_mem.astype(BF16), w_out.astype(BF16), tm=tm_out)


def kernel(x, mem, g_norm, g_mem_norm, g_final, w_in, b_forget, b_merge, w_mem_kv, lam_re, lam_im, log_step,
           s5_b_re, s5_b_im, s5_c_re, s5_c_im, s5_d, w_glu, b_glu, w_proj_fox, w_proj_s5, w_proj_mem, w_out):
    depth = w_in.shape[0]
    assert depth == 1, "the fused final kernel applies the closing RMSNorm: one layer only"
    l = x.shape[1]
    tm_in = min(512, l)
    t_attn = min(512, l)
    tm_out = min(512, l)
    return _layer(x, mem, g_norm[0], g_mem_norm[0], w_in, b_forget[0], b_merge[0], w_mem_kv[0],
                  lam_re[0], lam_im[0], log_step[0], s5_b_re[0], s5_b_im[0], s5_c_re[0], s5_c_im[0],
                  s5_d[0], w_glu[0], b_glu[0], w_proj_fox[0], w_proj_s5[0], w_proj_mem[0], w_out[0],
                  g_final, tm_in=tm_in, t_attn=t_attn, tm_out=tm_out)
```
